```python
import math
import jax, jax.numpy as jnp
from jax import lax
import numpy as np

D_MODEL = 2048
BATCH = 2
SEQ = 4096
DEPTH = 2
DEC_BATCH = 8
DEC_SEQ = 4
PAST_LEN = 16384
PAGE_SIZE = 128

HEAD_DIM = 128
ATTN_WIDTH = D_MODEL // 2
ATTN_HEADS = ATTN_WIDTH // HEAD_DIM
CONV_CH = D_MODEL - ATTN_WIDTH
CONV_K = 3
CONV_HIST = CONV_K - 1
IN_WIDTH = 3 * ATTN_WIDTH + 3 * CONV_CH
MOBA_BLOCK = 256
MOBA_TOPK = 3
QUERY_CHUNK = 16
ROPE_THETA = 10000.0
POOL_WINDOWS = (2, 4, 8, 16)
POOL_GROUP = D_MODEL // len(POOL_WINDOWS)
POOL_HIST = max(POOL_WINDOWS) - 1
D_FF = -(-(8 * D_MODEL) // (3 * 256)) * 256
N_EVEN = (DEPTH + 1) // 2
N_ODD = DEPTH // 2
RMS_EPS = 1e-6
NEG_INF = -1e30

kernel_name = 'moba_shortconv_pool_hybrid_step'


def rmsnorm(x, g):
    xf = x.astype(jnp.float32)
    y = xf * lax.rsqrt(jnp.mean(xf * xf, axis=-1, keepdims=True) + RMS_EPS)
    return (y * g.astype(jnp.float32)).astype(x.dtype)


def rope(x, pos):
    half = HEAD_DIM // 2
    inv = ROPE_THETA ** (-jnp.arange(half, dtype=jnp.float32) / half)
    ang = pos.astype(jnp.float32)[:, None] * inv[None, :]
    cos = jnp.cos(ang)[None, :, None, :]
    sin = jnp.sin(ang)[None, :, None, :]
    xf = x.astype(jnp.float32)
    x1, x2 = xf[..., :half], xf[..., half:]
    return jnp.concatenate([x1 * cos - x2 * sin, x2 * cos + x1 * sin], axis=-1).astype(x.dtype)


def moba_attention(q, k, v, q_pos):
    B, Tq, H, Dh = q.shape
    L = k.shape[1]
    nb = -(-L // MOBA_BLOCK)
    padw = ((0, 0), (0, nb * MOBA_BLOCK - L), (0, 0), (0, 0))
    kb = jnp.pad(k, padw).reshape(B, nb, MOBA_BLOCK, H, Dh)
    vb = jnp.pad(v, padw).reshape(B, nb, MOBA_BLOCK, H, Dh)
    k_mean = jnp.mean(kb.astype(jnp.float32), axis=2)
    n_sel = min(MOBA_TOPK, nb)
    n_slots = n_sel + 1
    scale = HEAD_DIM ** -0.5
    b_idx = jnp.arange(B)[:, None, None, None]
    h_idx = jnp.arange(H)[None, :, None, None]
    blk = jnp.arange(nb)
    off = jnp.arange(MOBA_BLOCK)

    def attend_chunk(args):
        qc, pc = args
        qn = pc.shape[0]
        own = pc // MOBA_BLOCK
        gate = jnp.einsum('bqhd,bnhd->bhqn', qc.astype(jnp.float32), k_mean)
        gate = jnp.where(blk[None, :] < own[:, None], gate, NEG_INF)
        _, top = lax.top_k(gate, n_sel)
        own_b = jnp.broadcast_to(own[None, None, :, None], (B, H, qn, 1))
        idx = jnp.concatenate([top.astype(jnp.int32), own_b.astype(jnp.int32)], axis=-1)
        slot_ok = jnp.concatenate([jnp.arange(n_sel)[None, :] < own[:, None],
                                   jnp.ones((qn, 1), dtype=bool)], axis=-1)
        kg = kb[b_idx, idx, :, h_idx, :]
        vg = vb[b_idx, idx, :, h_idx, :]
        key_pos = idx[..., None] * MOBA_BLOCK + off
        mask = slot_ok[None, None, :, :, None] & (key_pos <= pc[None, None, :, None, None])
        logits = jnp.einsum('bqhd,bhqsjd->bhqsj', qc, kg).astype(jnp.float32) * scale
        logits = jnp.where(mask, logits, NEG_INF)
        p = jax.nn.softmax(logits.reshape(B, H, qn, n_slots * MOBA_BLOCK), axis=-1)
        p = p.reshape(B, H, qn, n_slots, MOBA_BLOCK).astype(vg.dtype)
        return jnp.einsum('bhqsj,bhqsjd->bqhd', p, vg)

    qc_size = math.gcd(Tq, QUERY_CHUNK)
    nc = Tq // qc_size
    q_chunks = q.reshape(B, nc, qc_size, H, Dh).transpose(1, 0, 2, 3, 4)
    p_chunks = q_pos.reshape(nc, qc_size)
    out = lax.map(attend_chunk, (q_chunks, p_chunks))
    return out.transpose(1, 0, 2, 3, 4).reshape(B, Tq, H * Dh)


def even_mixer(h, pos, k_past, v_past, conv_hist, w_in, conv_w, w_o):
    B, T, _ = h.shape
    proj = h @ w_in
    cuts = [ATTN_WIDTH, 2 * ATTN_WIDTH, 3 * ATTN_WIDTH,
            3 * ATTN_WIDTH + CONV_CH, 3 * ATTN_WIDTH + 2 * CONV_CH]
    q, k, v, gate_b, gate_c, x_conv = jnp.split(proj, cuts, axis=-1)
    q = rope(q.reshape(B, T, ATTN_HEADS, HEAD_DIM), pos)
    k = rope(k.reshape(B, T, ATTN_HEADS, HEAD_DIM), pos)
    v = v.reshape(B, T, ATTN_HEADS, HEAD_DIM)
    k_all = jnp.concatenate([k_past.astype(k.dtype), k], axis=1)
    v_all = jnp.concatenate([v_past.astype(v.dtype), v], axis=1)
    attn = moba_attention(q, k_all, v_all, pos)
    u = gate_c * x_conv
    ext = jnp.concatenate([conv_hist.astype(u.dtype), u], axis=1)
    conv = ext[:, 0:T] * conv_w[0]
    for j in range(1, CONV_K):
        conv = conv + ext[:, j:j + T] * conv_w[j]
    y_conv = gate_b * conv
    out = jnp.concatenate([attn, y_conv], axis=-1) @ w_o
    return out, k, v, ext[:, -CONV_HIST:]


def odd_mixer(h, pos, pool_hist, w_pool, pool_scale):
    B, T, D = h.shape
    ext = jnp.concatenate([pool_hist.astype(h.dtype), h], axis=1)
    cs = jnp.cumsum(ext.astype(jnp.float32), axis=1)
    cs = jnp.concatenate([jnp.zeros((B, 1, D), jnp.float32), cs], axis=1)
    end = cs[:, POOL_HIST + 1:]
    hf = h.astype(jnp.float32)
    outs = []
    for g, w in enumerate(POOL_WINDOWS):
        sl = slice(g * POOL_GROUP, (g + 1) * POOL_GROUP)
        start = cs[:, POOL_HIST + 1 - w:POOL_HIST + 1 - w + T, sl]
        cnt = jnp.minimum(pos + 1, w).astype(jnp.float32)[None, :, None]
        d = (end[..., sl] - start) / cnt - hf[..., sl]
        outs.append(d.astype(h.dtype) @ w_pool[g])
    out = jnp.concatenate(outs, axis=-1) * pool_scale
    return out, ext[:, -POOL_HIST:]


def swiglu(h, w_gate, w_up, w_down):
    return (jax.nn.silu(h @ w_gate) * (h @ w_up)) @ w_down


def trunk(x, pos, past_kv, conv_hist, pool_hist, norm_mix, norm_ffn, norm_final,
          w_in, conv_w, w_o, w_pool, pool_scale, w_gate, w_up, w_down):
    new_k, new_v, new_conv, new_pool = [], [], [], []
    for layer in range(DEPTH):
        h = rmsnorm(x, norm_mix[layer])
        if layer % 2 == 0:
            e = layer // 2
            k_past, v_past = past_kv(e)
            mix, k, v, conv_state = even_mixer(h, pos, k_past, v_past, conv_hist[e],
                                               w_in[e], conv_w[e], w_o[e])
            new_k.append(k)
            new_v.append(v)
            new_conv.append(conv_state)
        else:
            o = layer // 2
            mix, pool_state = odd_mixer(h, pos, pool_hist[o], w_pool[o], pool_scale[o])
            new_pool.append(pool_state)
        x = x + mix
        x = x + swiglu(rmsnorm(x, norm_ffn[layer]), w_gate[layer], w_up[layer], w_down[layer])
    y = rmsnorm(x, norm_final)
    return y, jnp.stack(new_k), jnp.stack(new_v), jnp.stack(new_conv), jnp.stack(new_pool)


def _normal(key, shape, scale):
    return jax.random.normal(key, shape, jnp.float32) * scale


def setup_inputs(seed: int = 0) -> dict:
    key = jax.random.key(seed)
    ks = jax.random.split(key, 18)
    n_pages = PAST_LEN // PAGE_SIZE
    n_pool = (5 * DEC_BATCH * n_pages + 3) // 4
    perm = jax.random.permutation(ks[4], n_pool)
    page_table = perm[:DEC_BATCH * n_pages].reshape(DEC_BATCH, n_pages).astype(jnp.int32)
    return {
        'x_prompt': _normal(ks[0], (BATCH, SEQ, D_MODEL), 1.0),
        'x_sample': _normal(ks[1], (DEC_BATCH, DEC_SEQ, D_MODEL), 1.0),
        'cache_k': _normal(ks[2], (N_EVEN, n_pool, PAGE_SIZE, ATTN_HEADS, HEAD_DIM), 1.0),
        'cache_v': _normal(ks[3], (N_EVEN, n_pool, PAGE_SIZE, ATTN_HEADS, HEAD_DIM), 1.0),
        'page_table': page_table,
        'state_conv': _normal(ks[5], (N_EVEN, DEC_BATCH, CONV_HIST, CONV_CH), 1.0),
        'state_pool': _normal(ks[6], (N_ODD, DEC_BATCH, POOL_HIST, D_MODEL), 1.0),
        'norm_mix': 1.0 + _normal(ks[7], (DEPTH, D_MODEL), 0.1),
        'norm_ffn': 1.0 + _normal(ks[8], (DEPTH, D_MODEL), 0.1),
        'norm_final': 1.0 + _normal(ks[9], (D_MODEL,), 0.1),
        'w_in': _normal(ks[10], (N_EVEN, D_MODEL, IN_WIDTH), D_MODEL ** -0.5),
        'conv_w': _normal(ks[11], (N_EVEN, CONV_K, CONV_CH), CONV_K ** -0.5),
        'w_o': _normal(ks[12], (N_EVEN, ATTN_WIDTH + CONV_CH, D_MODEL), (ATTN_WIDTH + CONV_CH) ** -0.5),
        'w_pool': _normal(ks[13], (N_ODD, len(POOL_WINDOWS), POOL_GROUP, POOL_GROUP), POOL_GROUP ** -0.5),
        'pool_scale': 1.0 + _normal(ks[14], (N_ODD, D_MODEL), 0.1),
        'w_gate': _normal(ks[15], (DEPTH, D_MODEL, D_FF), D_MODEL ** -0.5),
        'w_up': _normal(ks[16], (DEPTH, D_MODEL, D_FF), D_MODEL ** -0.5),
        'w_down': _normal(ks[17], (DEPTH, D_FF, D_MODEL), D_FF ** -0.5),
    }


def reference(x_prompt, x_sample, cache_k, cache_v, page_table, state_conv, state_pool,
              norm_mix, norm_ffn, norm_final, w_in, conv_w, w_o, w_pool, pool_scale,
              w_gate, w_up, w_down):
    n_batch, n_seq, d = x_prompt.shape
    n_dec, n_new, _ = x_sample.shape
    past_len = page_table.shape[1] * cache_k.shape[2]

    empty = jnp.zeros((n_batch, 0, ATTN_HEADS, HEAD_DIM), x_prompt.dtype)

    def prompt_past(e):
        return empty, empty

    def sample_past(e):
        k_past = cache_k[e][page_table].reshape(n_dec, past_len, ATTN_HEADS, HEAD_DIM)
        v_past = cache_v[e][page_table].reshape(n_dec, past_len, ATTN_HEADS, HEAD_DIM)
        return k_past, v_past

    pos_prompt = jnp.arange(n_seq, dtype=jnp.int32)
    pos_sample = past_len + jnp.arange(n_new, dtype=jnp.int32)
    conv_zero = jnp.zeros((N_EVEN, n_batch, CONV_HIST, CONV_CH), x_prompt.dtype)
    pool_zero = jnp.zeros((N_ODD, n_batch, POOL_HIST, d), x_prompt.dtype)

    y_prompt, k_prompt, v_prompt, conv_prompt, pool_prompt = trunk(
        x_prompt, pos_prompt, prompt_past, conv_zero, pool_zero, norm_mix, norm_ffn, norm_final,
        w_in, conv_w, w_o, w_pool, pool_scale, w_gate, w_up, w_down)
    y_sample, k_sample, v_sample, conv_sample, pool_sample = trunk(
        x_sample, pos_sample, sample_past, state_conv, state_pool, norm_mix, norm_ffn, norm_final,
        w_in, conv_w, w_o, w_pool, pool_scale, w_gate, w_up, w_down)
    return (y_prompt, y_sample, k_prompt, v_prompt, k_sample, v_sample,
            conv_prompt, conv_sample, pool_prompt, pool_sample)
```

```python
import functools

import jax
import jax.numpy as jnp
from jax import lax
from jax.experimental import pallas as pl
from jax.experimental.pallas import tpu as pltpu

F32 = jnp.float32
BF16 = jnp.bfloat16

HEAD_DIM = 128
MOBA_BLOCK = 256
MOBA_TOPK = 3
CONV_K = 3
CONV_HIST = CONV_K - 1
ROPE_THETA = 10000.0
POOL_WINDOWS = (2, 4, 8, 16)
POOL_HIST = max(POOL_WINDOWS) - 1
RMS_EPS = 1e-6
NEG_INF = -1e30

SUBLANES = 8
VMEM_LIMIT = 56 * 1024 * 1024
PAGES_PER_STEP = 8

_NT = (((1,), (1,)), ((), ()))


def _rmsnorm(x, g):
    ms = jnp.mean(x * x, axis=-1, keepdims=True)
    return x * lax.rsqrt(ms + RMS_EPS) * g


def _halo_base(rows):
    return -(-rows // SUBLANES) * SUBLANES


def _inproj_kernel(x_ref, nw_ref, w_ref, cos_ref, sin_ref, cw_ref, hist_ref,
                   q_ref, k_ref, v_ref, yc_ref, cst_ref,
                   h_scr, gb_scr, gc_scr, ue_scr, *, bm, tps, stride, n_heads, q_scale):
    m = pl.program_id(0)
    n = pl.program_id(1)
    hs = CONV_HIST * stride
    base = _halo_base(hs)

    @pl.when(n == 0)
    def _():
        h_scr[...] = _rmsnorm(x_ref[...], nw_ref[...]).astype(BF16)

    acc = jnp.dot(h_scr[...], w_ref[...], preferred_element_type=F32)

    def rope(hh):
        seg = acc[:, hh * HEAD_DIM:(hh + 1) * HEAD_DIM]
        return seg * cos_ref[...] + pltpu.roll(seg, HEAD_DIM // 2, axis=1) * sin_ref[...]

    @pl.when(n == 0)
    def _():
        for hh in range(n_heads):
            q_ref[:, hh * HEAD_DIM:(hh + 1) * HEAD_DIM] = (rope(hh) * q_scale).astype(BF16)

    @pl.when(n == 1)
    def _():
        for hh in range(n_heads):
            k_ref[:, hh * HEAD_DIM:(hh + 1) * HEAD_DIM] = rope(hh)

    @pl.when(n == 2)
    def _():
        v_ref[...] = acc

    @pl.when(n == 3)
    def _():
        gb_scr[...] = acc

    @pl.when(n == 4)
    def _():
        gc_scr[...] = acc

    @pl.when(n == 5)
    def _():
        u = gc_scr[...] * acc
        first = (m % tps) == 0

        @pl.when(first)
        def _():
            ue_scr[base - hs:base, :] = hist_ref[...]

        @pl.when(jnp.logical_not(first))
        def _():
            ue_scr[base - hs:base, :] = ue_scr[base + bm - hs:base + bm, :]

        ue_scr[base:base + bm, :] = u
        cw = cw_ref[...]
        conv = u * cw[CONV_K - 1:CONV_K, :]
        for j in range(CONV_K - 1):
            off = base - (CONV_K - 1 - j) * stride
            conv = conv + ue_scr[off:off + bm, :] * cw[j:j + 1, :]
        yc_ref[...] = (gb_scr[...] * conv).astype(BF16)
        cst_ref[...] = ue_scr[base + bm - hs:base + bm, :]


def _inproj(x, nw, w_in, cos, sin, conv_w, hist, *, bm, tps, stride, q_scale):
    M, D = x.shape
    seg = conv_w.shape[1]
    n_seg = w_in.shape[1] // seg
    assert n_seg == 6 and M % bm == 0 and (M // bm) % tps == 0
    n_seq = hist.shape[0]
    hs = CONV_HIST * stride
    assert hist.shape == (n_seq, hs, seg) and bm >= hs
    base = _halo_base(hs)
    kern = functools.partial(_inproj_kernel, bm=bm, tps=tps, stride=stride,
                             n_heads=seg // HEAD_DIM, q_scale=q_scale)
    row = lambda m, n: (m, 0)
    return pl.pallas_call(
        kern,
        grid=(M // bm, n_seg),
        in_specs=[
            pl.BlockSpec((bm, D), row),
            pl.BlockSpec((1, D), lambda m, n: (0, 0)),
            pl.BlockSpec((D, seg), lambda m, n: (0, n)),
            pl.BlockSpec((bm, HEAD_DIM), lambda m, n: (m % tps, 0)),
            pl.BlockSpec((bm, HEAD_DIM), lambda m, n: (m % tps, 0)),
            pl.BlockSpec((CONV_K, seg), lambda m, n: (0, 0)),
            pl.BlockSpec((None, hs, seg), lambda m, n: (m // tps, 0, 0)),
        ],
        out_specs=[
            pl.BlockSpec((bm, seg), row),
            pl.BlockSpec((bm, seg), row),
            pl.BlockSpec((bm, seg), row),
            pl.BlockSpec((bm, seg), row),
            pl.BlockSpec((None, hs, seg), lambda m, n: (m // tps, 0, 0)),
        ],
        out_shape=[
            jax.ShapeDtypeStruct((M, seg), BF16),
            jax.ShapeDtypeStruct((M, seg), F32),
            jax.ShapeDtypeStruct((M, seg), F32),
            jax.ShapeDtypeStruct((M, seg), BF16),
            jax.ShapeDtypeStruct((n_seq, hs, seg), F32),
        ],
        scratch_shapes=[
            pltpu.VMEM((bm, D), BF16),
            pltpu.VMEM((bm, seg), F32),
            pltpu.VMEM((bm, seg), F32),
            pltpu.VMEM((base + bm, seg), F32),
        ],
        compiler_params=pltpu.CompilerParams(
            dimension_semantics=("arbitrary", "arbitrary"), vmem_limit_bytes=VMEM_LIMIT),
        name="inproj",
    )(x, nw, w_in, cos, sin, conv_w, hist)


def _moba_kernel(q_ref, k_ref, v_ref, o_ref, kb_scr, vb_scr, ksh_scr, ksl_scr,
                 m_scr, l_scr, acc_scr, *, nq):
    i = pl.program_id(2)
    blk = MOBA_BLOCK

    @pl.when(i == 0)
    def _():
        kf = k_ref[...]
        kb_scr[...] = kf.astype(BF16)
        vb_scr[...] = v_ref[...].astype(BF16)
        ks = jnp.sum(kf.reshape(nq, blk, HEAD_DIM), axis=1)
        hi = ks.astype(BF16)
        ksh_scr[...] = hi
        ksl_scr[...] = (ks - hi.astype(F32)).astype(BF16)

    q = q_ref[...]
    gate = (lax.dot_general(q, ksh_scr[...], _NT, preferred_element_type=F32)
            + lax.dot_general(q, ksl_scr[...], _NT, preferred_element_type=F32))
    col = lax.broadcasted_iota(jnp.int32, (blk, nq), 1)
    g = jnp.where(col < i, gate, -jnp.inf)
    sel = jnp.zeros((blk, nq), F32)
    for _ in range(MOBA_TOPK):
        mx = jnp.max(g, axis=1, keepdims=True)
        first = jnp.min(jnp.where(g == mx, col, nq), axis=1, keepdims=True)
        hit = col == first
        sel = jnp.where(hit & (first < i), 1.0, sel)
        g = jnp.where(hit, -jnp.inf, g)

    start = pl.multiple_of(i * blk, blk)
    s = lax.dot_general(q, kb_scr[pl.ds(start, blk), :], _NT, preferred_element_type=F32)
    rr = lax.broadcasted_iota(jnp.int32, (blk, blk), 0)
    cc = lax.broadcasted_iota(jnp.int32, (blk, blk), 1)
    s = jnp.where(cc <= rr, s, NEG_INF)
    m0 = jnp.max(s, axis=1, keepdims=True)
    p = jnp.exp(s - m0)
    m_scr[...] = m0
    l_scr[...] = jnp.sum(p, axis=1, keepdims=True)
    acc_scr[...] = jnp.dot(p.astype(BF16), vb_scr[pl.ds(start, blk), :],
                           preferred_element_type=F32)

    def body(j, carry):
        st = pl.multiple_of(j * blk, blk)
        sj = lax.dot_general(q, kb_scr[pl.ds(st, blk), :], _NT, preferred_element_type=F32)
        selj = jnp.max(jnp.where(col == j, sel, 0.0), axis=1, keepdims=True)
        sj = jnp.where(selj > 0.0, sj, NEG_INF)
        m_old = m_scr[...]
        m_new = jnp.maximum(m_old, jnp.max(sj, axis=1, keepdims=True))
        alpha = jnp.exp(m_old - m_new)
        pj = jnp.exp(sj - m_new)
        l_scr[...] = alpha * l_scr[...] + jnp.sum(pj, axis=1, keepdims=True)
        acc_scr[...] = alpha * acc_scr[...] + jnp.dot(
            pj.astype(BF16), vb_scr[pl.ds(st, blk), :], preferred_element_type=F32)
        m_scr[...] = m_new
        return carry

    lax.fori_loop(0, i, body, 0)
    o_ref[...] = (acc_scr[...] / l_scr[...]).astype(BF16)


def _moba_prompt(q, k, v, *, n_batch, seq, n_heads):
    blk = MOBA_BLOCK
    assert seq % blk == 0
    nq = seq // blk
    width = n_heads * HEAD_DIM
    k3 = k.reshape(n_batch, seq, width)
    v3 = v.reshape(n_batch, seq, width)
    kern = functools.partial(_moba_kernel, nq=nq)
    return pl.pallas_call(
        kern,
        grid=(n_batch, n_heads, nq),
        in_specs=[
            pl.BlockSpec((blk, HEAD_DIM), lambda b, h, i: (b * nq + i, h)),
            pl.BlockSpec((None, seq, HEAD_DIM), lambda b, h, i: (b, 0, h)),
            pl.BlockSpec((None, seq, HEAD_DIM), lambda b, h, i: (b, 0, h)),
        ],
        out_specs=pl.BlockSpec((blk, HEAD_DIM), lambda b, h, i: (b * nq + i, h)),
        out_shape=jax.ShapeDtypeStruct((n_batch * seq, width), BF16),
        scratch_shapes=[
            pltpu.VMEM((seq, HEAD_DIM), BF16),
            pltpu.VMEM((seq, HEAD_DIM), BF16),
            pltpu.VMEM((nq, HEAD_DIM), BF16),
            pltpu.VMEM((nq, HEAD_DIM), BF16),
            pltpu.VMEM((blk, 1), F32),
            pltpu.VMEM((blk, 1), F32),
            pltpu.VMEM((blk, HEAD_DIM), F32),
        ],
        compiler_params=pltpu.CompilerParams(
            dimension_semantics=("arbitrary", "arbitrary", "arbitrary"),
            vmem_limit_bytes=VMEM_LIMIT),
        name="moba_prompt",
    )(q, k3, v3)


def _outproj_kernel(a_ref, c_ref, wt_ref, wb_ref, x_ref, o_ref):
    o_ref[...] = (x_ref[...]
                  + jnp.dot(a_ref[...], wt_ref[...], preferred_element_type=F32)
                  + jnp.dot(c_ref[...], wb_ref[...], preferred_element_type=F32))


def _outproj(a, c, w_o, x, *, bm, bn):
    M, D = x.shape
    half = a.shape[1]
    assert w_o.shape == (2 * half, D) and c.shape == a.shape and D % bn == 0 and M % bm == 0
    return pl.pallas_call(
        _outproj_kernel,
        grid=(M // bm, D // bn),
        in_specs=[
            pl.BlockSpec((bm, half), lambda m, n: (m, 0)),
            pl.BlockSpec((bm, half), lambda m, n: (m, 0)),
            pl.BlockSpec((half, bn), lambda m, n: (0, n)),
            pl.BlockSpec((half, bn), lambda m, n: (1, n)),
            pl.BlockSpec((bm, bn), lambda m, n: (m, n)),
        ],
        out_specs=pl.BlockSpec((bm, bn), lambda m, n: (m, n)),
        out_shape=jax.ShapeDtypeStruct((M, D), F32),
        compiler_params=pltpu.CompilerParams(
            dimension_semantics=("arbitrary", "arbitrary"), vmem_limit_bytes=VMEM_LIMIT),
        name="outproj",
    )(a, c, w_o, w_o, x)


def _ffn_kernel(x_ref, nw_ref, wg_ref, wu_ref, wd_ref, fw_ref, o_ref, h_scr, *, final_norm):
    f = pl.program_id(1)

    @pl.when(f == 0)
    def _():
        x = x_ref[...]
        h_scr[...] = _rmsnorm(x, nw_ref[...]).astype(BF16)
        o_ref[...] = x

    h = h_scr[...]
    g = jnp.dot(h, wg_ref[...], preferred_element_type=F32)
    u = jnp.dot(h, wu_ref[...], preferred_element_type=F32)
    a = (g * jax.nn.sigmoid(g) * u).astype(BF16)
    o_ref[...] += jnp.dot(a, wd_ref[...], preferred_element_type=F32)

    if final_norm:
        @pl.when(f == pl.num_programs(1) - 1)
        def _():
            o_ref[...] = _rmsnorm(o_ref[...], fw_ref[...])


def _ffn(x, nw, wg, wu, wd, fw, *, bm, bf, final_norm):
    M, D = x.shape
    FF = wg.shape[1]
    assert FF % bf == 0 and M % bm == 0
    kern = functools.partial(_ffn_kernel, final_norm=final_norm)
    return pl.pallas_call(
        kern,
        grid=(M // bm, FF // bf),
        in_specs=[
            pl.BlockSpec((bm, D), lambda m, f: (m, 0)),
            pl.BlockSpec((1, D), lambda m, f: (0, 0)),
            pl.BlockSpec((D, bf), lambda m, f: (0, f)),
            pl.BlockSpec((D, bf), lambda m, f: (0, f)),
            pl.BlockSpec((bf, D), lambda m, f: (f, 0)),
            pl.BlockSpec((1, D), lambda m, f: (0, 0)),
        ],
        out_specs=pl.BlockSpec((bm, D), lambda m, f: (m, 0)),
        out_shape=jax.ShapeDtypeStruct((M, D), F32),
        scratch_shapes=[pltpu.VMEM((bm, D), BF16)],
        compiler_params=pltpu.CompilerParams(
            dimension_semantics=("arbitrary", "arbitrary"), vmem_limit_bytes=VMEM_LIMIT),
        name="ffn",
    )(x, nw, wg, wu, wd, fw)


def _pool_kernel(x_ref, nw_ref, wp_ref, ps_ref, hist_ref, o_ref, pst_ref, ext_scr,
                 *, bm, tps, stride, pos_base):
    m = pl.program_id(0)
    hs = POOL_HIST * stride
    base = _halo_base(hs)
    grp = wp_ref.shape[1]
    x = x_ref[...]
    h = _rmsnorm(x, nw_ref[...])
    first = (m % tps) == 0

    @pl.when(first)
    def _():
        ext_scr[base - hs:base, :] = hist_ref[...]

    @pl.when(jnp.logical_not(first))
    def _():
        ext_scr[base - hs:base, :] = ext_scr[base + bm - hs:base + bm, :]

    ext_scr[base:base + bm, :] = h
    r = (m % tps) * bm + lax.broadcasted_iota(jnp.int32, (bm, 1), 0)
    pos = pos_base + (r if stride == 1 else lax.shift_right_logical(r, stride.bit_length() - 1))
    outs = []
    for gi, w in enumerate(POOL_WINDOWS):
        c0 = gi * grp
        hg = h[:, c0:c0 + grp]
        tot = hg
        for kk in range(1, w):
            off = base - kk * stride
            tot = tot + ext_scr[off:off + bm, c0:c0 + grp]
        cnt = jnp.minimum(pos + 1, w).astype(F32)
        d = tot / cnt - hg
        outs.append(jnp.dot(d.astype(BF16), wp_ref[gi], preferred_element_type=F32))
    o_ref[...] = x + jnp.concatenate(outs, axis=1) * ps_ref[...]
    pst_ref[...] = ext_scr[base + bm - hs:base + bm, :]


def _pool(x, nw, w_pool, pool_scale, hist, *, bm, tps, stride, pos_base):
    M, D = x.shape
    n_seq = hist.shape[0]
    hs = POOL_HIST * stride
    assert stride & (stride - 1) == 0 and hist.shape == (n_seq, hs, D) and M % bm == 0
    base = _halo_base(hs)
    ng, grp, _ = w_pool.shape
    assert ng == len(POOL_WINDOWS) and ng * grp == D
    kern = functools.partial(_pool_kernel, bm=bm, tps=tps, stride=stride, pos_base=pos_base)
    return pl.pallas_call(
        kern,
        grid=(M // bm,),
        in_specs=[
            pl.BlockSpec((bm, D), lambda m: (m, 0)),
            pl.BlockSpec((1, D), lambda m: (0, 0)),
            pl.BlockSpec((ng, grp, grp), lambda m: (0, 0, 0)),
            pl.BlockSpec((1, D), lambda m: (0, 0)),
            pl.BlockSpec((None, hs, D), lambda m: (m // tps, 0, 0)),
        ],
        out_specs=[
            pl.BlockSpec((bm, D), lambda m: (m, 0)),
            pl.BlockSpec((None, hs, D), lambda m: (m // tps, 0, 0)),
        ],
        out_shape=[
            jax.ShapeDtypeStruct((M, D), F32),
            jax.ShapeDtypeStruct((n_seq, hs, D), F32),
        ],
        scratch_shapes=[pltpu.VMEM((base + bm, D), F32)],
        compiler_params=pltpu.CompilerParams(
            dimension_semantics=("arbitrary",), vmem_limit_bytes=VMEM_LIMIT),
        name="pool",
    )(x, nw, w_pool, pool_scale, hist)


def _head_match(n_rows, n_cols, n_heads, t_pad):
    rh = lax.broadcasted_iota(jnp.int32, (n_rows, n_cols), 0) // t_pad
    ch = lax.broadcasted_iota(jnp.int32, (n_rows, n_cols), 1) % n_heads
    return rh == ch


def _fold_heads(x, n_heads, t_pad):
    out = x[0:t_pad]
    for hh in range(1, n_heads):
        out = out + x[hh * t_pad:(hh + 1) * t_pad]
    return out


def _lane_class(x, n_heads, op):
    sh = n_heads
    while sh < x.shape[-1]:
        x = op(x, pltpu.roll(x, sh, axis=x.ndim - 1))
        sh *= 2
    return x


def _lane_tiles(x, op):
    out = x[:, 0:128]
    for t in range(1, x.shape[1] // 128):
        out = op(out, x[:, t * 128:(t + 1) * 128])
    return out


def _sa_logits_kernel(pt_ref, q_ref, *refs, n_pages, n_heads, t_pad, pages_per_block):
    k_refs = refs[:n_pages]
    r_ref, g_ref = refs[n_pages:]
    qa = q_ref[...]
    rows = k_refs[0].shape[0] * n_heads
    match = _head_match(qa.shape[0], rows, n_heads, t_pad)
    for pg in range(n_pages):
        k2 = k_refs[pg][...].reshape(rows, HEAD_DIM).astype(BF16)
        l2 = lax.dot_general(qa, k2, _NT, preferred_element_type=F32)
        r_ref[pg] = _fold_heads(jnp.where(match, l2, 0.0), n_heads, t_pad)
    for bb in range(n_pages // pages_per_block):
        tot = _lane_tiles(r_ref[bb * pages_per_block], jnp.add)
        for pg in range(1, pages_per_block):
            tot = tot + _lane_tiles(r_ref[bb * pages_per_block + pg], jnp.add)
        g_ref[bb] = _lane_class(tot, n_heads, jnp.add)


def _sa_values_kernel(pt_ref, r_ref, g_ref, q_ref, kn_ref, vn_ref, *refs,
                      n_pages, n_heads, t_pad, pages_per_block):
    v_refs = refs[:n_pages]
    o_ref, p_scr, sel_scr, l_scr, acc_scr = refs[n_pages:]
    c = pl.program_id(1)
    n_blocks = g_ref.shape[0]
    rows = v_refs[0].shape[0] * n_heads
    n_rows = q_ref.shape[0]
    match = _head_match(n_rows, rows, n_heads, t_pad)

    def spread(p):
        return jnp.where(match[:, :p.shape[1]], jnp.concatenate([p] * n_heads, axis=0), 0.0).astype(BF16)

    @pl.when(c == 0)
    def _():
        gates = g_ref[...]
        idx = lax.broadcasted_iota(jnp.int32, gates.shape, 0)
        sel = jnp.zeros(gates.shape, F32)
        for _ in range(MOBA_TOPK):
            mx = jnp.max(gates, axis=0, keepdims=True)
            first = jnp.min(jnp.where(gates == mx, idx, n_blocks), axis=0, keepdims=True)
            hit = idx == first
            sel = jnp.where(hit, 1.0, sel)
            gates = jnp.where(hit, -jnp.inf, gates)
        sel_scr[...] = sel

        own_w = kn_ref.shape[0]
        lo = lax.dot_general(q_ref[...], kn_ref[...].astype(BF16), _NT, preferred_element_type=F32)
        ro = _fold_heads(jnp.where(match[:, :own_w], lo, 0.0), n_heads, t_pad)
        t_key = lax.broadcasted_iota(jnp.int32, ro.shape, 1) // n_heads
        t_qry = lax.broadcasted_iota(jnp.int32, ro.shape, 0)
        ro = jnp.where(t_key <= t_qry, ro, NEG_INF)

        def max_body(b, mx):
            here = _lane_tiles(r_ref[b * pages_per_block], jnp.maximum)
            for pg in range(1, pages_per_block):
                here = jnp.maximum(here, _lane_tiles(r_ref[b * pages_per_block + pg], jnp.maximum))
            return jnp.maximum(mx, jnp.where(sel_scr[b] > 0.0, here, NEG_INF))

        mx = lax.fori_loop(0, n_blocks, max_body, ro)
        mx = _lane_class(mx, n_heads, jnp.maximum)
        width = r_ref.shape[2]
        mx_w = jnp.concatenate([mx] * (width // 128), axis=1)

        def exp_body(b, tot):
            keep = jnp.concatenate([sel_scr[b]] * (width // 128), axis=1) > 0.0
            for pg in range(pages_per_block):
                p = jnp.where(keep, jnp.exp(r_ref[b * pages_per_block + pg] - mx_w), 0.0)
                p_scr[b * pages_per_block + pg] = p
                tot = tot + _lane_tiles(p, jnp.add)
            return tot

        po = jnp.exp(ro - mx)
        tot = lax.fori_loop(0, n_blocks, exp_body, po)
        l_scr[...] = _lane_class(tot, n_heads, jnp.add)
        acc_scr[...] = jnp.dot(spread(po), vn_ref[...].astype(BF16), preferred_element_type=F32)

    acc = acc_scr[...]
    for pg in range(n_pages):
        v2 = v_refs[pg][...].reshape(rows, HEAD_DIM).astype(BF16)
        acc = acc + jnp.dot(spread(p_scr[c * n_pages + pg]), v2, preferred_element_type=F32)
    acc_scr[...] = acc

    @pl.when(c == pl.num_programs(1) - 1)
    def _():
        l = l_scr[...]
        den = jnp.concatenate([l[:, hh:hh + 1] for hh in range(n_heads)], axis=0)
        o_ref[...] = acc_scr[...] / den


def _sample_attention(q_all, k_new, v_new, cache_k, cache_v, page_table, *, layer, n_heads, t_pad):
    n_dec, n_pt = page_table.shape
    page = cache_k.shape[2]
    ppb = MOBA_BLOCK // page
    P = PAGES_PER_STEP
    assert MOBA_BLOCK % page == 0 and n_pt % P == 0 and P % ppb == 0
    n_steps = n_pt // P
    n_blocks = n_pt // ppb
    width = page * n_heads
    n_rows = n_heads * t_pad

    def page_spec(i):
        return pl.BlockSpec((None, None, page, n_heads, HEAD_DIM),
                            lambda b, c, pt: (layer, pt[b, c * P + i], 0, 0, 0))

    params = pltpu.CompilerParams(dimension_semantics=("arbitrary", "arbitrary"),
                                  vmem_limit_bytes=VMEM_LIMIT)
    common = dict(n_pages=P, n_heads=n_heads, t_pad=t_pad, pages_per_block=ppb)
    logits, gates = pl.pallas_call(
        functools.partial(_sa_logits_kernel, **common),
        grid_spec=pltpu.PrefetchScalarGridSpec(
            num_scalar_prefetch=1,
            grid=(n_dec, n_steps),
            in_specs=[pl.BlockSpec((None, n_rows, HEAD_DIM), lambda b, c, pt: (b, 0, 0))]
                     + [page_spec(i) for i in range(P)],
            out_specs=[
                pl.BlockSpec((None, P, t_pad, width), lambda b, c, pt: (b, c, 0, 0)),
                pl.BlockSpec((None, P // ppb, t_pad, 128), lambda b, c, pt: (b, c, 0, 0)),
            ],
        ),
        out_shape=[
            jax.ShapeDtypeStruct((n_dec, n_pt, t_pad, width), F32),
            jax.ShapeDtypeStruct((n_dec, n_blocks, t_pad, 128), F32),
        ],
        compiler_params=params,
        name="sample_logits",
    )(page_table, q_all, *([cache_k] * P))

    own = k_new.shape[1]
    return pl.pallas_call(
        functools.partial(_sa_values_kernel, **common),
        grid_spec=pltpu.PrefetchScalarGridSpec(
            num_scalar_prefetch=1,
            grid=(n_dec, n_steps),
            in_specs=[
                pl.BlockSpec((None, n_pt, t_pad, width), lambda b, c, pt: (b, 0, 0, 0)),
                pl.BlockSpec((None, n_blocks, t_pad, 128), lambda b, c, pt: (b, 0, 0, 0)),
                pl.BlockSpec((None, n_rows, HEAD_DIM), lambda b, c, pt: (b, 0, 0)),
                pl.BlockSpec((None, own, HEAD_DIM), lambda b, c, pt: (b, 0, 0)),
                pl.BlockSpec((None, own, HEAD_DIM), lambda b, c, pt: (b, 0, 0)),
            ] + [page_spec(i) for i in range(P)],
            out_specs=pl.BlockSpec((None, n_rows, HEAD_DIM), lambda b, c, pt: (b, 0, 0)),
            scratch_shapes=[
                pltpu.VMEM((n_pt, t_pad, width), F32),
                pltpu.VMEM((n_blocks, t_pad, 128), F32),
                pltpu.VMEM((t_pad, 128), F32),
                pltpu.VMEM((n_rows, HEAD_DIM), F32),
            ],
        ),
        out_shape=jax.ShapeDtypeStruct((n_dec, n_rows, HEAD_DIM), F32),
        compiler_params=params,
        name="sample_values",
    )(page_table, logits, gates, q_all, k_new, v_new, *([cache_v] * P))


def _rope_tables(pos):
    half = HEAD_DIM // 2
    inv = ROPE_THETA ** (-jnp.arange(half, dtype=F32) / half)
    ang = pos.astype(F32)[:, None] * inv[None, :]
    cos, sin = jnp.cos(ang), jnp.sin(ang)
    return jnp.concatenate([cos, cos], axis=1), jnp.concatenate([-sin, sin], axis=1)


def kernel(x_prompt, x_sample, cache_k, cache_v, page_table, state_conv, state_pool,
           norm_mix, norm_ffn, norm_final, w_in, conv_w, w_o, w_pool, pool_scale,
           w_gate, w_up, w_down):
    n_batch, n_seq, d = x_prompt.shape
    n_dec, n_new, _ = x_sample.shape
    depth = norm_mix.shape[0]
    seg = conv_w.shape[2]
    n_heads = seg // HEAD_DIM
    page = cache_k.shape[2]
    past_len = page_table.shape[1] * page
    assert n_dec == SUBLANES and n_new <= SUBLANES and past_len % MOBA_BLOCK == 0
    assert cache_k.shape[3] == n_heads and cache_k.shape[4] == HEAD_DIM
    q_scale = HEAD_DIM ** -0.5
    t_pad = SUBLANES
    own_keys = 128 // n_heads
    assert n_new <= own_keys

    bf = lambda a: a.astype(BF16)
    w_in_b, w_o_b, w_pool_b = bf(w_in), bf(w_o), bf(w_pool)
    w_gate_b, w_up_b, w_down_b = bf(w_gate), bf(w_up), bf(w_down)
    row = lambda a: a.reshape(1, -1)

    xp = x_prompt.reshape(n_batch * n_seq, d)
    xs = x_sample.transpose(1, 0, 2).reshape(n_new * n_dec, d)
    ms = n_new * n_dec
    bm = 512
    tps = n_seq // bm
    cos_p, sin_p = _rope_tables(jnp.arange(n_seq, dtype=jnp.int32))
    pos_s = past_len + jnp.repeat(jnp.arange(n_new, dtype=jnp.int32), n_dec)
    cos_s, sin_s = _rope_tables(pos_s)

    def tm(a):
        return a.transpose(1, 0, 2).reshape(1, a.shape[1] * n_dec, a.shape[2])

    def untm(a, rows):
        return a.reshape(rows, n_dec, a.shape[-1]).transpose(1, 0, 2)

    k_p, v_p, k_s, v_s, conv_p, conv_s, pool_p, pool_s = [], [], [], [], [], [], [], []
    for layer in range(depth):
        if layer % 2 == 0:
            e = layer // 2
            q, k, v, yc, cst = _inproj(
                xp, row(norm_mix[layer]), w_in_b[e], cos_p, sin_p, conv_w[e],
                jnp.zeros((n_batch, CONV_HIST, seg), F32),
                bm=bm, tps=tps, stride=1, q_scale=q_scale)
            attn = _moba_prompt(q, k, v, n_batch=n_batch, seq=n_seq, n_heads=n_heads)
            xp = _outproj(attn, yc, w_o_b[e], xp, bm=bm, bn=1024)
            k_p.append(k.reshape(n_batch, n_seq, n_heads, HEAD_DIM))
            v_p.append(v.reshape(n_batch, n_seq, n_heads, HEAD_DIM))
            conv_p.append(cst)
            q, k, v, yc, cst = _inproj(
                xs, row(norm_mix[layer]), w_in_b[e], cos_s, sin_s, conv_w[e], tm(state_conv[e]),
                bm=ms, tps=1, stride=n_dec, q_scale=q_scale)
            heads = lambda a: a.reshape(n_new, n_dec, n_heads, HEAD_DIM)
            q_all = jnp.pad(heads(q).transpose(1, 2, 0, 3), ((0, 0), (0, 0), (0, t_pad - n_new), (0, 0)))
            q_all = q_all.reshape(n_dec, n_heads * t_pad, HEAD_DIM)
            k_new = heads(k).transpose(1, 0, 2, 3)
            v_new = heads(v).transpose(1, 0, 2, 3)
            own = lambda a: jnp.pad(a, ((0, 0), (0, own_keys - n_new), (0, 0), (0, 0))).reshape(
                n_dec, own_keys * n_heads, HEAD_DIM)
            o = _sample_attention(q_all, own(k_new), own(v_new), cache_k, cache_v, page_table,
                                  layer=e, n_heads=n_heads, t_pad=t_pad)
            o = o.reshape(n_dec, n_heads, t_pad, HEAD_DIM)[:, :, :n_new]
            attn = o.transpose(2, 0, 1, 3).reshape(ms, seg).astype(BF16)
            xs = _outproj(attn, yc, w_o_b[e], xs, bm=ms, bn=1024)
            k_s.append(k_new)
            v_s.append(v_new)
            conv_s.append(untm(cst[0], CONV_HIST))
        else:
            o_ = layer // 2
            xp, pst = _pool(xp, row(norm_mix[layer]), w_pool_b[o_], row(pool_scale[o_]),
                            jnp.zeros((n_batch, POOL_HIST, d), F32),
                            bm=bm, tps=tps, stride=1, pos_base=0)
            pool_p.append(pst)
            xs, pst = _pool(xs, row(norm_mix[layer]), w_pool_b[o_], row(pool_scale[o_]),
                            tm(state_pool[o_]), bm=ms, tps=1, stride=n_dec, pos_base=past_len)
            pool_s.append(untm(pst[0], POOL_HIST))
        last = layer == depth - 1
        ffn = functools.partial(_ffn, nw=row(norm_ffn[layer]), wg=w_gate_b[layer], wu=w_up_b[layer],
                                wd=w_down_b[layer], fw=row(norm_final), bf=512, final_norm=last)
        xp = ffn(xp, bm=bm)
        xs = ffn(xs, bm=ms)

    y_prompt = xp.reshape(n_batch, n_seq, d)
    y_sample = xs.reshape(n_new, n_dec, d).transpose(1, 0, 2)
    return (y_prompt, y_sample, jnp.stack(k_p), jnp.stack(v_p), jnp.stack(k_s), jnp.stack(v_s),
            jnp.stack(conv_p), jnp.stack(conv_s), jnp.stack(pool_p), jnp.stack(pool_s))
```

```python
import functools

import jax
import jax.numpy as jnp
from jax import lax
from jax.experimental import pallas as pl
from jax.experimental.pallas import tpu as pltpu

F32 = jnp.float32
BF16 = jnp.bfloat16

HEAD_DIM = 128
MOBA_BLOCK = 256
MOBA_TOPK = 3
CONV_K = 3
CONV_HIST = CONV_K - 1
ROPE_THETA = 10000.0
POOL_WINDOWS = (2, 4, 8, 16)
POOL_HIST = max(POOL_WINDOWS) - 1
RMS_EPS = 1e-6
NEG_INF = -1e30

SUBLANES = 8
VMEM_LIMIT = 56 * 1024 * 1024
PAGES_PER_STEP = 8

_NT = (((1,), (1,)), ((), ()))


def _rmsnorm(x, g):
    ms = jnp.mean(x * x, axis=-1, keepdims=True)
    return x * lax.rsqrt(ms + RMS_EPS) * g


def _halo_base(rows):
    return -(-rows // SUBLANES) * SUBLANES


def _inproj_kernel(x_ref, nw_ref, w_ref, cos_ref, sin_ref, cw_ref, hist_ref,
                   q_ref, k_ref, v_ref, yc_ref, cst_ref,
                   h_scr, gb_scr, gc_scr, ue_scr, *, bm, tps, stride, n_heads, q_scale):
    m = pl.program_id(0)
    n = pl.program_id(1)
    hs = CONV_HIST * stride
    base = _halo_base(hs)

    @pl.when(n == 0)
    def _():
        h_scr[...] = _rmsnorm(x_ref[...], nw_ref[...]).astype(BF16)

    acc = jnp.dot(h_scr[...], w_ref[...], preferred_element_type=F32)

    def rope(hh):
        seg = acc[:, hh * HEAD_DIM:(hh + 1) * HEAD_DIM]
        return seg * cos_ref[...] + pltpu.roll(seg, HEAD_DIM // 2, axis=1) * sin_ref[...]

    @pl.when(n == 0)
    def _():
        for hh in range(n_heads):
            q_ref[:, hh * HEAD_DIM:(hh + 1) * HEAD_DIM] = (rope(hh) * q_scale).astype(BF16)

    @pl.when(n == 1)
    def _():
        for hh in range(n_heads):
            k_ref[:, hh * HEAD_DIM:(hh + 1) * HEAD_DIM] = rope(hh)

    @pl.when(n == 2)
    def _():
        v_ref[...] = acc

    @pl.when(n == 3)
    def _():
        gb_scr[...] = acc

    @pl.when(n == 4)
    def _():
        gc_scr[...] = acc

    @pl.when(n == 5)
    def _():
        u = gc_scr[...] * acc
        first = (m % tps) == 0

        @pl.when(first)
        def _():
            ue_scr[base - hs:base, :] = hist_ref[...]

        @pl.when(jnp.logical_not(first))
        def _():
            ue_scr[base - hs:base, :] = ue_scr[base + bm - hs:base + bm, :]

        ue_scr[base:base + bm, :] = u
        cw = cw_ref[...]
        conv = u * cw[CONV_K - 1:CONV_K, :]
        for j in range(CONV_K - 1):
            off = base - (CONV_K - 1 - j) * stride
            conv = conv + ue_scr[off:off + bm, :] * cw[j:j + 1, :]
        yc_ref[...] = (gb_scr[...] * conv).astype(BF16)
        cst_ref[...] = ue_scr[base + bm - hs:base + bm, :]


def _inproj(x, nw, w_in, cos, sin, conv_w, hist, *, bm, tps, stride, q_scale):
    M, D = x.shape
    seg = conv_w.shape[1]
    n_seg = w_in.shape[1] // seg
    assert n_seg == 6 and M % bm == 0 and (M // bm) % tps == 0
    n_seq = hist.shape[0]
    hs = CONV_HIST * stride
    assert hist.shape == (n_seq, hs, seg) and bm >= hs
    base = _halo_base(hs)
    kern = functools.partial(_inproj_kernel, bm=bm, tps=tps, stride=stride,
                             n_heads=seg // HEAD_DIM, q_scale=q_scale)
    row = lambda m, n: (m, 0)
    return pl.pallas_call(
        kern,
        grid=(M // bm, n_seg),
        in_specs=[
            pl.BlockSpec((bm, D), row),
            pl.BlockSpec((1, D), lambda m, n: (0, 0)),
            pl.BlockSpec((D, seg), lambda m, n: (0, n)),
            pl.BlockSpec((bm, HEAD_DIM), lambda m, n: (m % tps, 0)),
            pl.BlockSpec((bm, HEAD_DIM), lambda m, n: (m % tps, 0)),
            pl.BlockSpec((CONV_K, seg), lambda m, n: (0, 0)),
            pl.BlockSpec((None, hs, seg), lambda m, n: (m // tps, 0, 0)),
        ],
        out_specs=[
            pl.BlockSpec((bm, seg), row),
            pl.BlockSpec((bm, seg), row),
            pl.BlockSpec((bm, seg), row),
            pl.BlockSpec((bm, seg), row),
            pl.BlockSpec((None, hs, seg), lambda m, n: (m // tps, 0, 0)),
        ],
        out_shape=[
            jax.ShapeDtypeStruct((M, seg), BF16),
            jax.ShapeDtypeStruct((M, seg), F32),
            jax.ShapeDtypeStruct((M, seg), F32),
            jax.ShapeDtypeStruct((M, seg), BF16),
            jax.ShapeDtypeStruct((n_seq, hs, seg), F32),
        ],
        scratch_shapes=[
            pltpu.VMEM((bm, D), BF16),
            pltpu.VMEM((bm, seg), F32),
            pltpu.VMEM((bm, seg), F32),
            pltpu.VMEM((base + bm, seg), F32),
        ],
        compiler_params=pltpu.CompilerParams(
            dimension_semantics=("arbitrary", "arbitrary"), vmem_limit_bytes=VMEM_LIMIT),
        name="inproj",
    )(x, nw, w_in, cos, sin, conv_w, hist)


def _group_reduce(x, op):
    return op(x.reshape(x.shape[0] // SUBLANES, SUBLANES, x.shape[1]), axis=0)


def _moba_kernel(q_ref, k_ref, v_ref, o_ref, kb_scr, vt_scr, ks_scr, ksh_scr, ksl_scr,
                 qt_scr, sel_scr, s_scr, acc_scr, *, nq):
    i = pl.program_id(2)
    blk = MOBA_BLOCK
    n_gen = nq - 1
    owns = (i, nq - 1 - i)

    @pl.when(i == 0)
    def _():
        for j in range(nq):
            kf = k_ref[j * blk:(j + 1) * blk, :]
            kb_scr[j] = kf.astype(BF16)
            vt_scr[j] = v_ref[j * blk:(j + 1) * blk, :].T.astype(BF16)
            ks_scr[j:j + 1, :] = jnp.sum(kf, axis=0, keepdims=True)
        ks = ks_scr[...]
        hi = ks.astype(BF16)
        ksh_scr[...] = hi
        ksl_scr[...] = (ks - hi.astype(F32)).astype(BF16)

    blk_id = lax.broadcasted_iota(jnp.int32, (nq, blk), 0)
    for w, own in enumerate(owns):
        start = pl.multiple_of(own * blk, blk)
        qt = q_ref[pl.ds(start, blk), :].astype(F32).T.astype(BF16)
        qt_scr[w] = qt
        gate = (jnp.dot(ksh_scr[...], qt, preferred_element_type=F32)
                + jnp.dot(ksl_scr[...], qt, preferred_element_type=F32))
        g = jnp.where(blk_id < own, gate, -jnp.inf)
        sel = jnp.zeros((nq, blk), F32)
        for _ in range(MOBA_TOPK):
            mx = jnp.max(g, axis=0, keepdims=True)
            first = jnp.min(jnp.where(g == mx, blk_id, nq), axis=0, keepdims=True)
            hit = blk_id == first
            sel = jnp.where(hit & (first < own), 1.0, sel)
            g = jnp.where(hit, -jnp.inf, g)
        sel_scr[w] = sel

    def tile_of(t):
        is_a = t < i
        return is_a, jnp.where(is_a, 0, 1), jnp.where(is_a, t, t - i)

    mx = [jnp.full((SUBLANES, blk), NEG_INF, F32) for _ in owns]
    for t in range(n_gen):
        is_a, w, j = tile_of(t)
        s = jnp.dot(kb_scr[j], qt_scr[w], preferred_element_type=F32)
        s = jnp.where(sel_scr[w, pl.ds(j, 1), :] > 0.0, s, NEG_INF)
        s_scr[t] = s
        cm = _group_reduce(s, jnp.max)
        mx = [jnp.where(is_a, jnp.maximum(mx[0], cm), mx[0]),
              jnp.where(is_a, mx[1], jnp.maximum(mx[1], cm))]
    key_id = lax.broadcasted_iota(jnp.int32, (blk, blk), 0)
    qry_id = lax.broadcasted_iota(jnp.int32, (blk, blk), 1)
    for w, own in enumerate(owns):
        s = jnp.dot(kb_scr[own], qt_scr[w], preferred_element_type=F32)
        s = jnp.where(key_id <= qry_id, s, NEG_INF)
        s_scr[n_gen + w] = s
        mx[w] = jnp.maximum(mx[w], _group_reduce(s, jnp.max))
    mrow = [jnp.max(m, axis=0, keepdims=True) for m in mx]

    for w in range(len(owns)):
        acc_scr[w] = jnp.zeros((HEAD_DIM, blk), F32)
    den = [jnp.zeros((SUBLANES, blk), F32) for _ in owns]
    for t in range(n_gen):
        is_a, w, j = tile_of(t)
        p = jnp.exp(s_scr[t] - jnp.where(is_a, mrow[0], mrow[1]))
        cs = _group_reduce(p, jnp.sum)
        den = [den[0] + jnp.where(is_a, cs, 0.0), den[1] + jnp.where(is_a, 0.0, cs)]
        acc_scr[w] += jnp.dot(vt_scr[j], p.astype(BF16), preferred_element_type=F32)
    for w, own in enumerate(owns):
        p = jnp.exp(s_scr[n_gen + w] - mrow[w])
        tot = jnp.sum(den[w] + _group_reduce(p, jnp.sum), axis=0, keepdims=True)
        acc = acc_scr[w] + jnp.dot(vt_scr[own], p.astype(BF16), preferred_element_type=F32)
        start = pl.multiple_of(own * blk, blk)
        o_ref[pl.ds(start, blk), :] = (acc / tot).T.astype(BF16)


def _moba_prompt(q, k, v, *, n_batch, seq, n_heads):
    blk = MOBA_BLOCK
    assert seq % (2 * blk) == 0
    nq = seq // blk
    width = n_heads * HEAD_DIM
    col = pl.BlockSpec((None, seq, HEAD_DIM), lambda b, h, i: (b, 0, h))
    out = pl.pallas_call(
        functools.partial(_moba_kernel, nq=nq),
        grid=(n_batch, n_heads, nq // 2),
        in_specs=[col, col, col],
        out_specs=col,
        out_shape=jax.ShapeDtypeStruct((n_batch, seq, width), BF16),
        scratch_shapes=[
            pltpu.VMEM((nq, blk, HEAD_DIM), BF16),
            pltpu.VMEM((nq, HEAD_DIM, blk), BF16),
            pltpu.VMEM((nq, HEAD_DIM), F32),
            pltpu.VMEM((nq, HEAD_DIM), BF16),
            pltpu.VMEM((nq, HEAD_DIM), BF16),
            pltpu.VMEM((2, HEAD_DIM, blk), BF16),
            pltpu.VMEM((2, nq, blk), F32),
            pltpu.VMEM((nq + 1, blk, blk), F32),
            pltpu.VMEM((2, HEAD_DIM, blk), F32),
        ],
        compiler_params=pltpu.CompilerParams(
            dimension_semantics=("arbitrary", "arbitrary", "arbitrary"),
            vmem_limit_bytes=VMEM_LIMIT),
        name="moba_prompt",
    )(q.reshape(n_batch, seq, width), k.reshape(n_batch, seq, width),
      v.reshape(n_batch, seq, width))
    return out.reshape(n_batch * seq, width)


def _outproj_kernel(a_ref, c_ref, wt_ref, wb_ref, x_ref, o_ref):
    o_ref[...] = (x_ref[...]
                  + jnp.dot(a_ref[...], wt_ref[...], preferred_element_type=F32)
                  + jnp.dot(c_ref[...], wb_ref[...], preferred_element_type=F32))


def _outproj(a, c, w_o, x, *, bm, bn):
    M, D = x.shape
    half = a.shape[1]
    assert w_o.shape == (2 * half, D) and c.shape == a.shape and D % bn == 0 and M % bm == 0
    return pl.pallas_call(
        _outproj_kernel,
        grid=(M // bm, D // bn),
        in_specs=[
            pl.BlockSpec((bm, half), lambda m, n: (m, 0)),
            pl.BlockSpec((bm, half), lambda m, n: (m, 0)),
            pl.BlockSpec((half, bn), lambda m, n: (0, n)),
            pl.BlockSpec((half, bn), lambda m, n: (1, n)),
            pl.BlockSpec((bm, bn), lambda m, n: (m, n)),
        ],
        out_specs=pl.BlockSpec((bm, bn), lambda m, n: (m, n)),
        out_shape=jax.ShapeDtypeStruct((M, D), F32),
        compiler_params=pltpu.CompilerParams(
            dimension_semantics=("arbitrary", "arbitrary"), vmem_limit_bytes=VMEM_LIMIT),
        name="outproj",
    )(a, c, w_o, w_o, x)


def _ffn_kernel(x_ref, nw_ref, wg_ref, wu_ref, wd_ref, fw_ref, o_ref, h_scr, *, final_norm):
    f = pl.program_id(1)

    @pl.when(f == 0)
    def _():
        x = x_ref[...]
        h_scr[...] = _rmsnorm(x, nw_ref[...]).astype(BF16)
        o_ref[...] = x

    h = h_scr[...]
    g = jnp.dot(h, wg_ref[...], preferred_element_type=F32)
    u = jnp.dot(h, wu_ref[...], preferred_element_type=F32)
    a = (g * jax.nn.sigmoid(g) * u).astype(BF16)
    o_ref[...] += jnp.dot(a, wd_ref[...], preferred_element_type=F32)

    if final_norm:
        @pl.when(f == pl.num_programs(1) - 1)
        def _():
            o_ref[...] = _rmsnorm(o_ref[...], fw_ref[...])


def _ffn(x, nw, wg, wu, wd, fw, *, bm, bf, final_norm):
    M, D = x.shape
    FF = wg.shape[1]
    assert FF % bf == 0 and M % bm == 0
    kern = functools.partial(_ffn_kernel, final_norm=final_norm)
    return pl.pallas_call(
        kern,
        grid=(M // bm, FF // bf),
        in_specs=[
            pl.BlockSpec((bm, D), lambda m, f: (m, 0)),
            pl.BlockSpec((1, D), lambda m, f: (0, 0)),
            pl.BlockSpec((D, bf), lambda m, f: (0, f)),
            pl.BlockSpec((D, bf), lambda m, f: (0, f)),
            pl.BlockSpec((bf, D), lambda m, f: (f, 0)),
            pl.BlockSpec((1, D), lambda m, f: (0, 0)),
        ],
        out_specs=pl.BlockSpec((bm, D), lambda m, f: (m, 0)),
        out_shape=jax.ShapeDtypeStruct((M, D), F32),
        scratch_shapes=[pltpu.VMEM((bm, D), BF16)],
        compiler_params=pltpu.CompilerParams(
            dimension_semantics=("arbitrary", "arbitrary"), vmem_limit_bytes=VMEM_LIMIT),
        name="ffn",
    )(x, nw, wg, wu, wd, fw)


def _pool_kernel(x_ref, nw_ref, wp_ref, ps_ref, hist_ref, o_ref, pst_ref, ext_scr,
                 *, bm, tps, stride, pos_base):
    m = pl.program_id(0)
    hs = POOL_HIST * stride
    base = _halo_base(hs)
    grp = wp_ref.shape[1]
    x = x_ref[...]
    h = _rmsnorm(x, nw_ref[...])
    first = (m % tps) == 0

    @pl.when(first)
    def _():
        ext_scr[base - hs:base, :] = hist_ref[...]

    @pl.when(jnp.logical_not(first))
    def _():
        ext_scr[base - hs:base, :] = ext_scr[base + bm - hs:base + bm, :]

    ext_scr[base:base + bm, :] = h
    r = (m % tps) * bm + lax.broadcasted_iota(jnp.int32, (bm, 1), 0)
    pos = pos_base + (r if stride == 1 else lax.shift_right_logical(r, stride.bit_length() - 1))
    outs = []
    for gi, w in enumerate(POOL_WINDOWS):
        c0 = gi * grp
        hg = h[:, c0:c0 + grp]
        tot = hg
        for kk in range(1, w):
            off = base - kk * stride
            tot = tot + ext_scr[off:off + bm, c0:c0 + grp]
        cnt = jnp.minimum(pos + 1, w).astype(F32)
        d = tot / cnt - hg
        outs.append(jnp.dot(d.astype(BF16), wp_ref[gi], preferred_element_type=F32))
    o_ref[...] = x + jnp.concatenate(outs, axis=1) * ps_ref[...]
    pst_ref[...] = ext_scr[base + bm - hs:base + bm, :]


def _pool(x, nw, w_pool, pool_scale, hist, *, bm, tps, stride, pos_base):
    M, D = x.shape
    n_seq = hist.shape[0]
    hs = POOL_HIST * stride
    assert stride & (stride - 1) == 0 and hist.shape == (n_seq, hs, D) and M % bm == 0
    base = _halo_base(hs)
    ng, grp, _ = w_pool.shape
    assert ng == len(POOL_WINDOWS) and ng * grp == D
    kern = functools.partial(_pool_kernel, bm=bm, tps=tps, stride=stride, pos_base=pos_base)
    return pl.pallas_call(
        kern,
        grid=(M // bm,),
        in_specs=[
            pl.BlockSpec((bm, D), lambda m: (m, 0)),
            pl.BlockSpec((1, D), lambda m: (0, 0)),
            pl.BlockSpec((ng, grp, grp), lambda m: (0, 0, 0)),
            pl.BlockSpec((1, D), lambda m: (0, 0)),
            pl.BlockSpec((None, hs, D), lambda m: (m // tps, 0, 0)),
        ],
        out_specs=[
            pl.BlockSpec((bm, D), lambda m: (m, 0)),
            pl.BlockSpec((None, hs, D), lambda m: (m // tps, 0, 0)),
        ],
        out_shape=[
            jax.ShapeDtypeStruct((M, D), F32),
            jax.ShapeDtypeStruct((n_seq, hs, D), F32),
        ],
        scratch_shapes=[pltpu.VMEM((base + bm, D), F32)],
        compiler_params=pltpu.CompilerParams(
            dimension_semantics=("arbitrary",), vmem_limit_bytes=VMEM_LIMIT),
        name="pool",
    )(x, nw, w_pool, pool_scale, hist)


def _head_match(n_rows, n_cols, n_heads, t_pad):
    rh = lax.broadcasted_iota(jnp.int32, (n_rows, n_cols), 0) // t_pad
    ch = lax.broadcasted_iota(jnp.int32, (n_rows, n_cols), 1) % n_heads
    return rh == ch


def _fold_heads(x, n_heads, t_pad):
    out = x[0:t_pad]
    for hh in range(1, n_heads):
        out = out + x[hh * t_pad:(hh + 1) * t_pad]
    return out


def _lane_class(x, n_heads, op):
    sh = n_heads
    while sh < x.shape[-1]:
        x = op(x, pltpu.roll(x, sh, axis=x.ndim - 1))
        sh *= 2
    return x


def _lane_tiles(x, op):
    out = x[:, 0:128]
    for t in range(1, x.shape[1] // 128):
        out = op(out, x[:, t * 128:(t + 1) * 128])
    return out


def _sa_logits_kernel(pt_ref, q_ref, *refs, n_pages, n_heads, t_pad, pages_per_block):
    k_refs = refs[:n_pages]
    r_ref, g_ref = refs[n_pages:]
    qa = q_ref[...]
    rows = k_refs[0].shape[0] * n_heads
    match = _head_match(qa.shape[0], rows, n_heads, t_pad)
    for pg in range(n_pages):
        k2 = k_refs[pg][...].reshape(rows, HEAD_DIM).astype(BF16)
        l2 = lax.dot_general(qa, k2, _NT, preferred_element_type=F32)
        r_ref[pg] = _fold_heads(jnp.where(match, l2, 0.0), n_heads, t_pad)
    for bb in range(n_pages // pages_per_block):
        tot = _lane_tiles(r_ref[bb * pages_per_block], jnp.add)
        for pg in range(1, pages_per_block):
            tot = tot + _lane_tiles(r_ref[bb * pages_per_block + pg], jnp.add)
        g_ref[bb] = _lane_class(tot, n_heads, jnp.add)


def _sa_values_kernel(pt_ref, r_ref, g_ref, q_ref, kn_ref, vn_ref, *refs,
                      n_pages, n_heads, t_pad, pages_per_block):
    v_refs = refs[:n_pages]
    o_ref, p_scr, sel_scr, l_scr, acc_scr = refs[n_pages:]
    c = pl.program_id(1)
    n_blocks = g_ref.shape[0]
    rows = v_refs[0].shape[0] * n_heads
    n_rows = q_ref.shape[0]
    match = _head_match(n_rows, rows, n_heads, t_pad)

    def spread(p):
        return jnp.where(match[:, :p.shape[1]], jnp.concatenate([p] * n_heads, axis=0), 0.0).astype(BF16)

    @pl.when(c == 0)
    def _():
        gates = g_ref[...]
        idx = lax.broadcasted_iota(jnp.int32, gates.shape, 0)
        sel = jnp.zeros(gates.shape, F32)
        for _ in range(MOBA_TOPK):
            mx = jnp.max(gates, axis=0, keepdims=True)
            first = jnp.min(jnp.where(gates == mx, idx, n_blocks), axis=0, keepdims=True)
            hit = idx == first
            sel = jnp.where(hit, 1.0, sel)
            gates = jnp.where(hit, -jnp.inf, gates)
        sel_scr[...] = sel

        own_w = kn_ref.shape[0]
        lo = lax.dot_general(q_ref[...], kn_ref[...].astype(BF16), _NT, preferred_element_type=F32)
        ro = _fold_heads(jnp.where(match[:, :own_w], lo, 0.0), n_heads, t_pad)
        t_key = lax.broadcasted_iota(jnp.int32, ro.shape, 1) // n_heads
        t_qry = lax.broadcasted_iota(jnp.int32, ro.shape, 0)
        ro = jnp.where(t_key <= t_qry, ro, NEG_INF)

        def max_body(b, mx):
            here = _lane_tiles(r_ref[b * pages_per_block], jnp.maximum)
            for pg in range(1, pages_per_block):
                here = jnp.maximum(here, _lane_tiles(r_ref[b * pages_per_block + pg], jnp.maximum))
            return jnp.maximum(mx, jnp.where(sel_scr[b] > 0.0, here, NEG_INF))

        mx = lax.fori_loop(0, n_blocks, max_body, ro)
        mx = _lane_class(mx, n_heads, jnp.maximum)
        width = r_ref.shape[2]
        mx_w = jnp.concatenate([mx] * (width // 128), axis=1)

        def exp_body(b, tot):
            keep = jnp.concatenate([sel_scr[b]] * (width // 128), axis=1) > 0.0
            for pg in range(pages_per_block):
                p = jnp.where(keep, jnp.exp(r_ref[b * pages_per_block + pg] - mx_w), 0.0)
                p_scr[b * pages_per_block + pg] = p
                tot = tot + _lane_tiles(p, jnp.add)
            return tot

        po = jnp.exp(ro - mx)
        tot = lax.fori_loop(0, n_blocks, exp_body, po)
        l_scr[...] = _lane_class(tot, n_heads, jnp.add)
        acc_scr[...] = jnp.dot(spread(po), vn_ref[...].astype(BF16), preferred_element_type=F32)

    acc = acc_scr[...]
    for pg in range(n_pages):
        v2 = v_refs[pg][...].reshape(rows, HEAD_DIM).astype(BF16)
        acc = acc + jnp.dot(spread(p_scr[c * n_pages + pg]), v2, preferred_element_type=F32)
    acc_scr[...] = acc

    @pl.when(c == pl.num_programs(1) - 1)
    def _():
        l = l_scr[...]
        den = jnp.concatenate([l[:, hh:hh + 1] for hh in range(n_heads)], axis=0)
        o_ref[...] = acc_scr[...] / den


def _sample_attention(q_all, k_new, v_new, cache_k, cache_v, page_table, *, layer, n_heads, t_pad):
    n_dec, n_pt = page_table.shape
    page = cache_k.shape[2]
    ppb = MOBA_BLOCK // page
    P = PAGES_PER_STEP
    assert MOBA_BLOCK % page == 0 and n_pt % P == 0 and P % ppb == 0
    n_steps = n_pt // P
    n_blocks = n_pt // ppb
    width = page * n_heads
    n_rows = n_heads * t_pad

    def page_spec(i):
        return pl.BlockSpec((None, None, page, n_heads, HEAD_DIM),
                            lambda b, c, pt: (layer, pt[b, c * P + i], 0, 0, 0))

    params = pltpu.CompilerParams(dimension_semantics=("arbitrary", "arbitrary"),
                                  vmem_limit_bytes=VMEM_LIMIT)
    common = dict(n_pages=P, n_heads=n_heads, t_pad=t_pad, pages_per_block=ppb)
    logits, gates = pl.pallas_call(
        functools.partial(_sa_logits_kernel, **common),
        grid_spec=pltpu.PrefetchScalarGridSpec(
            num_scalar_prefetch=1,
            grid=(n_dec, n_steps),
            in_specs=[pl.BlockSpec((None, n_rows, HEAD_DIM), lambda b, c, pt: (b, 0, 0))]
                     + [page_spec(i) for i in range(P)],
            out_specs=[
                pl.BlockSpec((None, P, t_pad, width), lambda b, c, pt: (b, c, 0, 0)),
                pl.BlockSpec((None, P // ppb, t_pad, 128), lambda b, c, pt: (b, c, 0, 0)),
            ],
        ),
        out_shape=[
            jax.ShapeDtypeStruct((n_dec, n_pt, t_pad, width), F32),
            jax.ShapeDtypeStruct((n_dec, n_blocks, t_pad, 128), F32),
        ],
        compiler_params=params,
        name="sample_logits",
    )(page_table, q_all, *([cache_k] * P))

    own = k_new.shape[1]
    return pl.pallas_call(
        functools.partial(_sa_values_kernel, **common),
        grid_spec=pltpu.PrefetchScalarGridSpec(
            num_scalar_prefetch=1,
            grid=(n_dec, n_steps),
            in_specs=[
                pl.BlockSpec((None, n_pt, t_pad, width), lambda b, c, pt: (b, 0, 0, 0)),
                pl.BlockSpec((None, n_blocks, t_pad, 128), lambda b, c, pt: (b, 0, 0, 0)),
                pl.BlockSpec((None, n_rows, HEAD_DIM), lambda b, c, pt: (b, 0, 0)),
                pl.BlockSpec((None, own, HEAD_DIM), lambda b, c, pt: (b, 0, 0)),
                pl.BlockSpec((None, own, HEAD_DIM), lambda b, c, pt: (b, 0, 0)),
            ] + [page_spec(i) for i in range(P)],
            out_specs=pl.BlockSpec((None, n_rows, HEAD_DIM), lambda b, c, pt: (b, 0, 0)),
            scratch_shapes=[
                pltpu.VMEM((n_pt, t_pad, width), F32),
                pltpu.VMEM((n_blocks, t_pad, 128), F32),
                pltpu.VMEM((t_pad, 128), F32),
                pltpu.VMEM((n_rows, HEAD_DIM), F32),
            ],
        ),
        out_shape=jax.ShapeDtypeStruct((n_dec, n_rows, HEAD_DIM), F32),
        compiler_params=params,
        name="sample_values",
    )(page_table, logits, gates, q_all, k_new, v_new, *([cache_v] * P))


def _rope_tables(pos):
    half = HEAD_DIM // 2
    inv = ROPE_THETA ** (-jnp.arange(half, dtype=F32) / half)
    ang = pos.astype(F32)[:, None] * inv[None, :]
    cos, sin = jnp.cos(ang), jnp.sin(ang)
    return jnp.concatenate([cos, cos], axis=1), jnp.concatenate([-sin, sin], axis=1)


def kernel(x_prompt, x_sample, cache_k, cache_v, page_table, state_conv, state_pool,
           norm_mix, norm_ffn, norm_final, w_in, conv_w, w_o, w_pool, pool_scale,
           w_gate, w_up, w_down):
    n_batch, n_seq, d = x_prompt.shape
    n_dec, n_new, _ = x_sample.shape
    depth = norm_mix.shape[0]
    seg = conv_w.shape[2]
    n_heads = seg // HEAD_DIM
    page = cache_k.shape[2]
    past_len = page_table.shape[1] * page
    assert n_dec == SUBLANES and n_new <= SUBLANES and past_len % MOBA_BLOCK == 0
    assert cache_k.shape[3] == n_heads and cache_k.shape[4] == HEAD_DIM
    q_scale = HEAD_DIM ** -0.5
    t_pad = SUBLANES
    own_keys = 128 // n_heads
    assert n_new <= own_keys

    bf = lambda a: a.astype(BF16)
    row = lambda a: a.reshape(1, -1)

    xp = x_prompt.reshape(n_batch * n_seq, d)
    xs = x_sample.transpose(1, 0, 2).reshape(n_new * n_dec, d)
    ms = n_new * n_dec
    bm = 512
    tps = n_seq // bm
    cos_p, sin_p = _rope_tables(jnp.arange(n_seq, dtype=jnp.int32))
    pos_s = past_len + jnp.repeat(jnp.arange(n_new, dtype=jnp.int32), n_dec)
    cos_s, sin_s = _rope_tables(pos_s)

    def tm(a):
        return a.transpose(1, 0, 2).reshape(1, a.shape[1] * n_dec, a.shape[2])

    def untm(a, rows):
        return a.reshape(rows, n_dec, a.shape[-1]).transpose(1, 0, 2)

    k_p, v_p, k_s, v_s, conv_p, conv_s, pool_p, pool_s = [], [], [], [], [], [], [], []
    for layer in range(depth):
        if layer % 2 == 0:
            e = layer // 2
            w_in_e, w_o_e = bf(w_in[e]), bf(w_o[e])
            q, k, v, yc, cst = _inproj(
                xp, row(norm_mix[layer]), w_in_e, cos_p, sin_p, conv_w[e],
                jnp.zeros((n_batch, CONV_HIST, seg), F32),
                bm=bm, tps=tps, stride=1, q_scale=q_scale)
            attn = _moba_prompt(q, k, v, n_batch=n_batch, seq=n_seq, n_heads=n_heads)
            xp = _outproj(attn, yc, w_o_e, xp, bm=bm, bn=1024)
            k_p.append(k.reshape(n_batch, n_seq, n_heads, HEAD_DIM))
            v_p.append(v.reshape(n_batch, n_seq, n_heads, HEAD_DIM))
            conv_p.append(cst)
            q, k, v, yc, cst = _inproj(
                xs, row(norm_mix[layer]), w_in_e, cos_s, sin_s, conv_w[e], tm(state_conv[e]),
                bm=ms, tps=1, stride=n_dec, q_scale=q_scale)
            heads = lambda a: a.reshape(n_new, n_dec, n_heads, HEAD_DIM)
            q_all = jnp.pad(heads(q).transpose(1, 2, 0, 3), ((0, 0), (0, 0), (0, t_pad - n_new), (0, 0)))
            q_all = q_all.reshape(n_dec, n_heads * t_pad, HEAD_DIM)
            k_new = heads(k).transpose(1, 0, 2, 3)
            v_new = heads(v).transpose(1, 0, 2, 3)
            own = lambda a: jnp.pad(a, ((0, 0), (0, own_keys - n_new), (0, 0), (0, 0))).reshape(
                n_dec, own_keys * n_heads, HEAD_DIM)
            o = _sample_attention(q_all, own(k_new), own(v_new), cache_k, cache_v, page_table,
                                  layer=e, n_heads=n_heads, t_pad=t_pad)
            o = o.reshape(n_dec, n_heads, t_pad, HEAD_DIM)[:, :, :n_new]
            attn = o.transpose(2, 0, 1, 3).reshape(ms, seg).astype(BF16)
            xs = _outproj(attn, yc, w_o_e, xs, bm=ms, bn=1024)
            k_s.append(k_new)
            v_s.append(v_new)
            conv_s.append(untm(cst[0], CONV_HIST))
        else:
            o_ = layer // 2
            w_pool_o = bf(w_pool[o_])
            xp, pst = _pool(xp, row(norm_mix[layer]), w_pool_o, row(pool_scale[o_]),
                            jnp.zeros((n_batch, POOL_HIST, d), F32),
                            bm=bm, tps=tps, stride=1, pos_base=0)
            pool_p.append(pst)
            xs, pst = _pool(xs, row(norm_mix[layer]), w_pool_o, row(pool_scale[o_]),
                            tm(state_pool[o_]), bm=ms, tps=1, stride=n_dec, pos_base=past_len)
            pool_s.append(untm(pst[0], POOL_HIST))
        last = layer == depth - 1
        ffn = functools.partial(_ffn, nw=row(norm_ffn[layer]), wg=bf(w_gate[layer]), wu=bf(w_up[layer]),
                                wd=bf(w_down[layer]), fw=row(norm_final), bf=512, final_norm=last)
        xp = ffn(xp, bm=bm)
        xs = ffn(xs, bm=ms)

    y_prompt = xp.reshape(n_batch, n_seq, d)
    y_sample = xs.reshape(n_new, n_dec, d).transpose(1, 0, 2)
    return (y_prompt, y_sample, jnp.stack(k_p), jnp.stack(v_p), jnp.stack(k_s), jnp.stack(v_s),
            jnp.stack(conv_p), jnp.stack(conv_s), jnp.stack(pool_p), jnp.stack(pool_s))
```

```python
import functools

import jax
import jax.numpy as jnp
from jax import lax
from jax.experimental import pallas as pl
from jax.experimental.pallas import tpu as pltpu

F32 = jnp.float32
BF16 = jnp.bfloat16

HEAD_DIM = 128
MOBA_BLOCK = 256
MOBA_TOPK = 3
CONV_K = 3
CONV_HIST = CONV_K - 1
ROPE_THETA = 10000.0
POOL_WINDOWS = (2, 4, 8, 16)
POOL_HIST = max(POOL_WINDOWS) - 1
RMS_EPS = 1e-6
NEG_INF = -1e30
LOG2_E = 1.4426950408889634

SUBLANES = 8
VMEM_LIMIT = 56 * 1024 * 1024
PAGES_PER_STEP = 32

_NT = (((1,), (1,)), ((), ()))


def _rmsnorm(x, g):
    ms = jnp.mean(x * x, axis=-1, keepdims=True)
    return x * lax.rsqrt(ms + RMS_EPS) * g


def _halo_base(rows):
    return -(-rows // SUBLANES) * SUBLANES


def _qkv_kernel(x_ref, nw_ref, w_ref, cos_ref, sin_ref, q_ref, k_ref, v_ref, *, n_heads, q_scale):
    seg = n_heads * HEAD_DIM
    h = _rmsnorm(x_ref[...], nw_ref[...]).astype(BF16)
    cos = cos_ref[...]
    sin = sin_ref[...]

    def rope(a, hh):
        part = a[:, hh * HEAD_DIM:(hh + 1) * HEAD_DIM]
        return part * cos + pltpu.roll(part, HEAD_DIM // 2, axis=1) * sin

    aq = jnp.dot(h, w_ref[:, 0:seg], preferred_element_type=F32)
    for hh in range(n_heads):
        q_ref[:, hh * HEAD_DIM:(hh + 1) * HEAD_DIM] = (rope(aq, hh) * q_scale).astype(BF16)
    ak = jnp.dot(h, w_ref[:, seg:2 * seg], preferred_element_type=F32)
    for hh in range(n_heads):
        k_ref[:, hh * HEAD_DIM:(hh + 1) * HEAD_DIM] = rope(ak, hh)
    v_ref[...] = jnp.dot(h, w_ref[:, 2 * seg:3 * seg], preferred_element_type=F32)


def _gconv_kernel(x_ref, nw_ref, w_ref, cw_ref, hist_ref, yc_ref, cst_ref, ue_scr,
                  *, bm, tps, stride):
    m = pl.program_id(0)
    seg = yc_ref.shape[1]
    hs = CONV_HIST * stride
    base = _halo_base(hs)
    h = _rmsnorm(x_ref[...], nw_ref[...]).astype(BF16)
    u = (jnp.dot(h, w_ref[:, seg:2 * seg], preferred_element_type=F32)
         * jnp.dot(h, w_ref[:, 2 * seg:3 * seg], preferred_element_type=F32))
    first = (m % tps) == 0

    @pl.when(first)
    def _():
        ue_scr[base - hs:base, :] = hist_ref[...]

    @pl.when(jnp.logical_not(first))
    def _():
        ue_scr[base - hs:base, :] = ue_scr[base + bm - hs:base + bm, :]

    ue_scr[base:base + bm, :] = u
    cw = cw_ref[...]
    conv = u * cw[CONV_K - 1:CONV_K, :]
    for j in range(CONV_K - 1):
        off = base - (CONV_K - 1 - j) * stride
        conv = conv + ue_scr[off:off + bm, :] * cw[j:j + 1, :]
    gate_b = jnp.dot(h, w_ref[:, 0:seg], preferred_element_type=F32)
    yc_ref[...] = (gate_b * conv).astype(BF16)
    cst_ref[...] = ue_scr[base + bm - hs:base + bm, :]


def _inproj(x, nw, w_in, cos, sin, conv_w, hist, *, layer, bm, tps, stride, q_scale):
    M, D = x.shape
    seg = conv_w.shape[1]
    assert w_in.shape[2] == 6 * seg and M % bm == 0 and (M // bm) % tps == 0
    n_seq = hist.shape[0]
    hs = CONV_HIST * stride
    assert hist.shape == (n_seq, hs, seg) and bm >= hs
    base = _halo_base(hs)
    params = pltpu.CompilerParams(dimension_semantics=("arbitrary",), vmem_limit_bytes=VMEM_LIMIT)
    rows = lambda width: pl.BlockSpec((bm, width), lambda m: (m, 0))
    half = lambda n: pl.BlockSpec((None, D, 3 * seg), lambda m: (layer, 0, n),
                                  pipeline_mode=pl.Buffered(1))
    norm = pl.BlockSpec((1, D), lambda m: (0, 0))
    table = pl.BlockSpec((bm, HEAD_DIM), lambda m: (m % tps, 0))
    q, k, v = pl.pallas_call(
        functools.partial(_qkv_kernel, n_heads=seg // HEAD_DIM, q_scale=q_scale),
        grid=(M // bm,),
        in_specs=[rows(D), norm, half(0), table, table],
        out_specs=[rows(seg), rows(seg), rows(seg)],
        out_shape=[
            jax.ShapeDtypeStruct((M, seg), BF16),
            jax.ShapeDtypeStruct((M, seg), F32),
            jax.ShapeDtypeStruct((M, seg), F32),
        ],
        compiler_params=params,
        name="inproj_qkv",
    )(x, nw, w_in, cos, sin)
    state = pl.BlockSpec((None, hs, seg), lambda m: (m // tps, 0, 0))
    yc, cst = pl.pallas_call(
        functools.partial(_gconv_kernel, bm=bm, tps=tps, stride=stride),
        grid=(M // bm,),
        in_specs=[rows(D), norm, half(1), pl.BlockSpec((CONV_K, seg), lambda m: (0, 0)), state],
        out_specs=[rows(seg), state],
        out_shape=[
            jax.ShapeDtypeStruct((M, seg), BF16),
            jax.ShapeDtypeStruct((n_seq, hs, seg), F32),
        ],
        scratch_shapes=[pltpu.VMEM((base + bm, seg), F32)],
        compiler_params=params,
        name="inproj_gconv",
    )(x, nw, w_in, conv_w, hist)
    return q, k, v, yc, cst


def _group_reduce(x, op):
    return op(x.reshape(x.shape[0] // SUBLANES, SUBLANES, x.shape[1]), axis=0)


def _moba_kernel(q_ref, k_ref, v_ref, o_ref, kb_scr, vt_scr, ks_scr, ksh_scr, ksl_scr,
                 qt_scr, sel_scr, s_scr, acc_scr, *, nq):
    i = pl.program_id(2)
    blk = MOBA_BLOCK
    n_gen = nq - 1
    owns = (i, nq - 1 - i)

    @pl.when(i == 0)
    def _():
        for j in range(nq):
            kf = k_ref[j * blk:(j + 1) * blk, :]
            kb_scr[j] = kf.astype(BF16)
            vt_scr[j] = v_ref[j * blk:(j + 1) * blk, :].T.astype(BF16)
            ks_scr[j:j + 1, :] = jnp.sum(kf, axis=0, keepdims=True)
        ks = ks_scr[...]
        hi = ks.astype(BF16)
        ksh_scr[...] = hi
        ksl_scr[...] = (ks - hi.astype(F32)).astype(BF16)

    blk_id = lax.broadcasted_iota(jnp.int32, (nq, blk), 0)
    for w, own in enumerate(owns):
        start = pl.multiple_of(own * blk, blk)
        qt = q_ref[pl.ds(start, blk), :].astype(F32).T.astype(BF16)
        qt_scr[w] = qt
        gate = (jnp.dot(ksh_scr[...], qt, preferred_element_type=F32)
                + jnp.dot(ksl_scr[...], qt, preferred_element_type=F32))
        g = jnp.where(blk_id < own, gate, -jnp.inf)
        sel = jnp.zeros((nq, blk), F32)
        for _ in range(MOBA_TOPK):
            mx = jnp.max(g, axis=0, keepdims=True)
            first = jnp.min(jnp.where(g == mx, blk_id, nq), axis=0, keepdims=True)
            hit = blk_id == first
            sel = jnp.where(hit & (first < own), 1.0, sel)
            g = jnp.where(hit, -jnp.inf, g)
        sel_scr[w] = sel

    def tile_of(t):
        is_a = t < i
        return is_a, jnp.where(is_a, 0, 1), jnp.where(is_a, t, t - i)

    mx = [jnp.full((SUBLANES, blk), NEG_INF, F32) for _ in owns]
    for t in range(n_gen):
        is_a, w, j = tile_of(t)
        s = jnp.dot(kb_scr[j], qt_scr[w], preferred_element_type=F32)
        s = jnp.where(sel_scr[w, pl.ds(j, 1), :] > 0.0, s, NEG_INF)
        s_scr[t] = s
        cm = _group_reduce(s, jnp.max)
        mx = [jnp.where(is_a, jnp.maximum(mx[0], cm), mx[0]),
              jnp.where(is_a, mx[1], jnp.maximum(mx[1], cm))]
    key_id = lax.broadcasted_iota(jnp.int32, (blk, blk), 0)
    qry_id = lax.broadcasted_iota(jnp.int32, (blk, blk), 1)
    for w, own in enumerate(owns):
        s = jnp.dot(kb_scr[own], qt_scr[w], preferred_element_type=F32)
        s = jnp.where(key_id <= qry_id, s, NEG_INF)
        s_scr[n_gen + w] = s
        mx[w] = jnp.maximum(mx[w], _group_reduce(s, jnp.max))
    mrow = [jnp.max(m, axis=0, keepdims=True) for m in mx]

    for w in range(len(owns)):
        acc_scr[w] = jnp.zeros((HEAD_DIM, blk), F32)
    den = [jnp.zeros((SUBLANES, blk), F32) for _ in owns]
    for t in range(n_gen):
        is_a, w, j = tile_of(t)
        p = jnp.exp2(s_scr[t] - jnp.where(is_a, mrow[0], mrow[1]))
        cs = _group_reduce(p, jnp.sum)
        den = [den[0] + jnp.where(is_a, cs, 0.0), den[1] + jnp.where(is_a, 0.0, cs)]
        acc_scr[w] += jnp.dot(vt_scr[j], p.astype(BF16), preferred_element_type=F32)
    for w, own in enumerate(owns):
        p = jnp.exp2(s_scr[n_gen + w] - mrow[w])
        tot = jnp.sum(den[w] + _group_reduce(p, jnp.sum), axis=0, keepdims=True)
        acc = acc_scr[w] + jnp.dot(vt_scr[own], p.astype(BF16), preferred_element_type=F32)
        start = pl.multiple_of(own * blk, blk)
        o_ref[pl.ds(start, blk), :] = (acc / tot).T.astype(BF16)


def _moba_prompt(q, k, v, *, n_batch, seq, n_heads):
    blk = MOBA_BLOCK
    assert seq % (2 * blk) == 0
    nq = seq // blk
    width = n_heads * HEAD_DIM
    col = pl.BlockSpec((None, seq, HEAD_DIM), lambda b, h, i: (b, 0, h))
    out = pl.pallas_call(
        functools.partial(_moba_kernel, nq=nq),
        grid=(n_batch, n_heads, nq // 2),
        in_specs=[col, col, col],
        out_specs=col,
        out_shape=jax.ShapeDtypeStruct((n_batch, seq, width), BF16),
        scratch_shapes=[
            pltpu.VMEM((nq, blk, HEAD_DIM), BF16),
            pltpu.VMEM((nq, HEAD_DIM, blk), BF16),
            pltpu.VMEM((nq, HEAD_DIM), F32),
            pltpu.VMEM((nq, HEAD_DIM), BF16),
            pltpu.VMEM((nq, HEAD_DIM), BF16),
            pltpu.VMEM((2, HEAD_DIM, blk), BF16),
            pltpu.VMEM((2, nq, blk), F32),
            pltpu.VMEM((nq + 1, blk, blk), F32),
            pltpu.VMEM((2, HEAD_DIM, blk), F32),
        ],
        compiler_params=pltpu.CompilerParams(
            dimension_semantics=("arbitrary", "arbitrary", "arbitrary"),
            vmem_limit_bytes=VMEM_LIMIT),
        name="moba_prompt",
    )(q.reshape(n_batch, seq, width), k.reshape(n_batch, seq, width),
      v.reshape(n_batch, seq, width))
    return out.reshape(n_batch * seq, width)


def _outproj_kernel(a_ref, c_ref, wt_ref, wb_ref, x_ref, o_ref):
    o_ref[...] = (x_ref[...]
                  + jnp.dot(a_ref[...], wt_ref[...], preferred_element_type=F32)
                  + jnp.dot(c_ref[...], wb_ref[...], preferred_element_type=F32))


def _outproj(a, c, w_o, x, *, layer, bm, bn):
    M, D = x.shape
    half = a.shape[1]
    assert w_o.shape[1:] == (2 * half, D) and c.shape == a.shape and D % bn == 0 and M % bm == 0
    return pl.pallas_call(
        _outproj_kernel,
        grid=(M // bm, D // bn),
        in_specs=[
            pl.BlockSpec((bm, half), lambda m, n: (m, 0)),
            pl.BlockSpec((bm, half), lambda m, n: (m, 0)),
            pl.BlockSpec((None, half, bn), lambda m, n: (layer, 0, n)),
            pl.BlockSpec((None, half, bn), lambda m, n: (layer, 1, n)),
            pl.BlockSpec((bm, bn), lambda m, n: (m, n)),
        ],
        out_specs=pl.BlockSpec((bm, bn), lambda m, n: (m, n)),
        out_shape=jax.ShapeDtypeStruct((M, D), F32),
        compiler_params=pltpu.CompilerParams(
            dimension_semantics=("arbitrary", "arbitrary"), vmem_limit_bytes=VMEM_LIMIT),
        name="outproj",
    )(a, c, w_o, w_o, x)


def _ffn_kernel(x_ref, nw_ref, wg_ref, wu_ref, wd_ref, fw_ref, o_ref, h_scr, *, final_norm):
    f = pl.program_id(1)

    @pl.when(f == 0)
    def _():
        x = x_ref[...]
        h_scr[...] = _rmsnorm(x, nw_ref[...]).astype(BF16)
        o_ref[...] = x

    h = h_scr[...]
    g = jnp.dot(h, wg_ref[...], preferred_element_type=F32)
    u = jnp.dot(h, wu_ref[...], preferred_element_type=F32)
    a = (g * jax.nn.sigmoid(g) * u).astype(BF16)
    o_ref[...] += jnp.dot(a, wd_ref[...], preferred_element_type=F32)

    if final_norm:
        @pl.when(f == pl.num_programs(1) - 1)
        def _():
            o_ref[...] = _rmsnorm(o_ref[...], fw_ref[...])


def _ffn(x, nw, wg, wu, wd, fw, *, layer, bm, bf, final_norm):
    M, D = x.shape
    FF = wg.shape[2]
    assert FF % bf == 0 and M % bm == 0
    kern = functools.partial(_ffn_kernel, final_norm=final_norm)
    return pl.pallas_call(
        kern,
        grid=(M // bm, FF // bf),
        in_specs=[
            pl.BlockSpec((bm, D), lambda m, f: (m, 0)),
            pl.BlockSpec((1, D), lambda m, f: (0, 0)),
            pl.BlockSpec((None, D, bf), lambda m, f: (layer, 0, f)),
            pl.BlockSpec((None, D, bf), lambda m, f: (layer, 0, f)),
            pl.BlockSpec((None, bf, D), lambda m, f: (layer, f, 0)),
            pl.BlockSpec((1, D), lambda m, f: (0, 0)),
        ],
        out_specs=pl.BlockSpec((bm, D), lambda m, f: (m, 0)),
        out_shape=jax.ShapeDtypeStruct((M, D), F32),
        scratch_shapes=[pltpu.VMEM((bm, D), BF16)],
        compiler_params=pltpu.CompilerParams(
            dimension_semantics=("arbitrary", "arbitrary"), vmem_limit_bytes=VMEM_LIMIT),
        name="ffn",
    )(x, nw, wg, wu, wd, fw)


def _pool_kernel(x_ref, nw_ref, wp_ref, ps_ref, hist_ref, o_ref, pst_ref, ext_scr,
                 *, bm, tps, stride, pos_base):
    m = pl.program_id(0)
    hs = POOL_HIST * stride
    base = _halo_base(hs)
    grp = wp_ref.shape[1]
    x = x_ref[...]
    h = _rmsnorm(x, nw_ref[...])
    first = (m % tps) == 0

    @pl.when(first)
    def _():
        ext_scr[base - hs:base, :] = hist_ref[...]

    @pl.when(jnp.logical_not(first))
    def _():
        ext_scr[base - hs:base, :] = ext_scr[base + bm - hs:base + bm, :]

    ext_scr[base:base + bm, :] = h
    r = (m % tps) * bm + lax.broadcasted_iota(jnp.int32, (bm, 1), 0)
    pos = pos_base + (r if stride == 1 else lax.shift_right_logical(r, stride.bit_length() - 1))
    outs = []
    for gi, w in enumerate(POOL_WINDOWS):
        c0 = gi * grp
        hg = h[:, c0:c0 + grp]
        tot = hg
        for kk in range(1, w):
            off = base - kk * stride
            tot = tot + ext_scr[off:off + bm, c0:c0 + grp]
        cnt = jnp.minimum(pos + 1, w).astype(F32)
        d = tot / cnt - hg
        outs.append(jnp.dot(d.astype(BF16), wp_ref[gi], preferred_element_type=F32))
    o_ref[...] = x + jnp.concatenate(outs, axis=1) * ps_ref[...]
    pst_ref[...] = ext_scr[base + bm - hs:base + bm, :]


def _pool(x, nw, w_pool, pool_scale, hist, *, layer, bm, tps, stride, pos_base):
    M, D = x.shape
    n_seq = hist.shape[0]
    hs = POOL_HIST * stride
    assert stride & (stride - 1) == 0 and hist.shape == (n_seq, hs, D) and M % bm == 0
    base = _halo_base(hs)
    _, ng, grp, _ = w_pool.shape
    assert ng == len(POOL_WINDOWS) and ng * grp == D
    kern = functools.partial(_pool_kernel, bm=bm, tps=tps, stride=stride, pos_base=pos_base)
    return pl.pallas_call(
        kern,
        grid=(M // bm,),
        in_specs=[
            pl.BlockSpec((bm, D), lambda m: (m, 0)),
            pl.BlockSpec((1, D), lambda m: (0, 0)),
            pl.BlockSpec((None, ng, grp, grp), lambda m: (layer, 0, 0, 0)),
            pl.BlockSpec((1, D), lambda m: (0, 0)),
            pl.BlockSpec((None, hs, D), lambda m: (m // tps, 0, 0)),
        ],
        out_specs=[
            pl.BlockSpec((bm, D), lambda m: (m, 0)),
            pl.BlockSpec((None, hs, D), lambda m: (m // tps, 0, 0)),
        ],
        out_shape=[
            jax.ShapeDtypeStruct((M, D), F32),
            jax.ShapeDtypeStruct((n_seq, hs, D), F32),
        ],
        scratch_shapes=[pltpu.VMEM((base + bm, D), F32)],
        compiler_params=pltpu.CompilerParams(
            dimension_semantics=("arbitrary",), vmem_limit_bytes=VMEM_LIMIT),
        name="pool",
    )(x, nw, w_pool, pool_scale, hist)


def _head_match(n_rows, n_cols, n_heads, t_pad):
    rh = lax.broadcasted_iota(jnp.int32, (n_rows, n_cols), 0) // t_pad
    ch = lax.broadcasted_iota(jnp.int32, (n_rows, n_cols), 1) % n_heads
    return rh == ch


def _fold_heads(x, n_heads, t_pad):
    out = x[0:t_pad]
    for hh in range(1, n_heads):
        out = out + x[hh * t_pad:(hh + 1) * t_pad]
    return out


def _lane_class(x, n_heads, op):
    sh = n_heads
    while sh < x.shape[-1]:
        x = op(x, pltpu.roll(x, sh, axis=x.ndim - 1))
        sh *= 2
    return x


def _lane_tiles(x, op):
    out = x[:, 0:128]
    for t in range(1, x.shape[1] // 128):
        out = op(out, x[:, t * 128:(t + 1) * 128])
    return out


def _sa_logits_kernel(pt_ref, q_ref, *refs, n_pages, n_heads, t_pad, pages_per_block):
    k_refs = refs[:n_pages]
    r_ref, g_ref = refs[n_pages:]
    qa = q_ref[...]
    rows = k_refs[0].shape[0] * n_heads
    match = _head_match(qa.shape[0], rows, n_heads, t_pad)
    for pg in range(n_pages):
        k2 = k_refs[pg][...].reshape(rows, HEAD_DIM).astype(BF16)
        l2 = lax.dot_general(qa, k2, _NT, preferred_element_type=F32)
        r_ref[pg] = _fold_heads(jnp.where(match, l2, 0.0), n_heads, t_pad)
    for bb in range(n_pages // pages_per_block):
        tot = _lane_tiles(r_ref[bb * pages_per_block], jnp.add)
        for pg in range(1, pages_per_block):
            tot = tot + _lane_tiles(r_ref[bb * pages_per_block + pg], jnp.add)
        g_ref[bb] = _lane_class(tot, n_heads, jnp.add)


def _sa_values_kernel(pt_ref, r_ref, g_ref, q_ref, kn_ref, vn_ref, *refs,
                      n_pages, n_heads, t_pad, pages_per_block):
    v_refs = refs[:n_pages]
    o_ref, p_scr, sel_scr, l_scr, acc_scr = refs[n_pages:]
    c = pl.program_id(1)
    n_blocks = g_ref.shape[0]
    rows = v_refs[0].shape[0] * n_heads
    n_rows = q_ref.shape[0]
    match = _head_match(n_rows, rows, n_heads, t_pad)

    def spread(p):
        return jnp.where(match[:, :p.shape[1]], jnp.concatenate([p] * n_heads, axis=0), 0.0).astype(BF16)

    @pl.when(c == 0)
    def _():
        gates = g_ref[...]
        idx = lax.broadcasted_iota(jnp.int32, gates.shape, 0)
        sel = jnp.zeros(gates.shape, F32)
        for _ in range(MOBA_TOPK):
            mx = jnp.max(gates, axis=0, keepdims=True)
            first = jnp.min(jnp.where(gates == mx, idx, n_blocks), axis=0, keepdims=True)
            hit = idx == first
            sel = jnp.where(hit, 1.0, sel)
            gates = jnp.where(hit, -jnp.inf, gates)
        sel_scr[...] = sel

        own_w = kn_ref.shape[0]
        lo = lax.dot_general(q_ref[...], kn_ref[...].astype(BF16), _NT, preferred_element_type=F32)
        ro = _fold_heads(jnp.where(match[:, :own_w], lo, 0.0), n_heads, t_pad)
        t_key = lax.broadcasted_iota(jnp.int32, ro.shape, 1) // n_heads
        t_qry = lax.broadcasted_iota(jnp.int32, ro.shape, 0)
        ro = jnp.where(t_key <= t_qry, ro, NEG_INF)

        def max_body(b, mx):
            here = _lane_tiles(r_ref[b * pages_per_block], jnp.maximum)
            for pg in range(1, pages_per_block):
                here = jnp.maximum(here, _lane_tiles(r_ref[b * pages_per_block + pg], jnp.maximum))
            return jnp.maximum(mx, jnp.where(sel_scr[b] > 0.0, here, NEG_INF))

        mx = lax.fori_loop(0, n_blocks, max_body, ro)
        mx = _lane_class(mx, n_heads, jnp.maximum)
        width = r_ref.shape[2]
        mx_w = jnp.concatenate([mx] * (width // 128), axis=1)

        def exp_body(b, tot):
            keep = jnp.concatenate([sel_scr[b]] * (width // 128), axis=1) > 0.0
            for pg in range(pages_per_block):
                p = jnp.where(keep, jnp.exp(r_ref[b * pages_per_block + pg] - mx_w), 0.0)
                p_scr[b * pages_per_block + pg] = p
                tot = tot + _lane_tiles(p, jnp.add)
            return tot

        po = jnp.exp(ro - mx)
        tot = lax.fori_loop(0, n_blocks, exp_body, po)
        l_scr[...] = _lane_class(tot, n_heads, jnp.add)
        acc_scr[...] = jnp.dot(spread(po), vn_ref[...].astype(BF16), preferred_element_type=F32)

    acc = acc_scr[...]
    for pg in range(n_pages):
        v2 = v_refs[pg][...].reshape(rows, HEAD_DIM).astype(BF16)
        acc = acc + jnp.dot(spread(p_scr[c * n_pages + pg]), v2, preferred_element_type=F32)
    acc_scr[...] = acc

    @pl.when(c == pl.num_programs(1) - 1)
    def _():
        l = l_scr[...]
        den = jnp.concatenate([l[:, hh:hh + 1] for hh in range(n_heads)], axis=0)
        o_ref[...] = acc_scr[...] / den


def _sample_attention(q_all, k_new, v_new, cache_k, cache_v, page_table, *, layer, n_heads, t_pad):
    n_dec, n_pt = page_table.shape
    page = cache_k.shape[2]
    ppb = MOBA_BLOCK // page
    P = PAGES_PER_STEP
    assert MOBA_BLOCK % page == 0 and n_pt % P == 0 and P % ppb == 0
    n_steps = n_pt // P
    n_blocks = n_pt // ppb
    width = page * n_heads
    n_rows = n_heads * t_pad

    def page_spec(i):
        return pl.BlockSpec((None, None, page, n_heads, HEAD_DIM),
                            lambda b, c, pt: (layer, pt[b, c * P + i], 0, 0, 0))

    params = pltpu.CompilerParams(dimension_semantics=("arbitrary", "arbitrary"),
                                  vmem_limit_bytes=VMEM_LIMIT)
    common = dict(n_pages=P, n_heads=n_heads, t_pad=t_pad, pages_per_block=ppb)
    logits, gates = pl.pallas_call(
        functools.partial(_sa_logits_kernel, **common),
        grid_spec=pltpu.PrefetchScalarGridSpec(
            num_scalar_prefetch=1,
            grid=(n_dec, n_steps),
            in_specs=[pl.BlockSpec((None, n_rows, HEAD_DIM), lambda b, c, pt: (b, 0, 0))]
                     + [page_spec(i) for i in range(P)],
            out_specs=[
                pl.BlockSpec((None, P, t_pad, width), lambda b, c, pt: (b, c, 0, 0)),
                pl.BlockSpec((None, P // ppb, t_pad, 128), lambda b, c, pt: (b, c, 0, 0)),
            ],
        ),
        out_shape=[
            jax.ShapeDtypeStruct((n_dec, n_pt, t_pad, width), F32),
            jax.ShapeDtypeStruct((n_dec, n_blocks, t_pad, 128), F32),
        ],
        compiler_params=params,
        name="sample_logits",
    )(page_table, q_all, *([cache_k] * P))

    own = k_new.shape[1]
    return pl.pallas_call(
        functools.partial(_sa_values_kernel, **common),
        grid_spec=pltpu.PrefetchScalarGridSpec(
            num_scalar_prefetch=1,
            grid=(n_dec, n_steps),
            in_specs=[
                pl.BlockSpec((None, n_pt, t_pad, width), lambda b, c, pt: (b, 0, 0, 0)),
                pl.BlockSpec((None, n_blocks, t_pad, 128), lambda b, c, pt: (b, 0, 0, 0)),
                pl.BlockSpec((None, n_rows, HEAD_DIM), lambda b, c, pt: (b, 0, 0)),
                pl.BlockSpec((None, own, HEAD_DIM), lambda b, c, pt: (b, 0, 0)),
                pl.BlockSpec((None, own, HEAD_DIM), lambda b, c, pt: (b, 0, 0)),
            ] + [page_spec(i) for i in range(P)],
            out_specs=pl.BlockSpec((None, n_rows, HEAD_DIM), lambda b, c, pt: (b, 0, 0)),
            scratch_shapes=[
                pltpu.VMEM((n_pt, t_pad, width), F32),
                pltpu.VMEM((n_blocks, t_pad, 128), F32),
                pltpu.VMEM((t_pad, 128), F32),
                pltpu.VMEM((n_rows, HEAD_DIM), F32),
            ],
        ),
        out_shape=jax.ShapeDtypeStruct((n_dec, n_rows, HEAD_DIM), F32),
        compiler_params=params,
        name="sample_values",
    )(page_table, logits, gates, q_all, k_new, v_new, *([cache_v] * P))


def _rope_tables(pos):
    half = HEAD_DIM // 2
    inv = ROPE_THETA ** (-jnp.arange(half, dtype=F32) / half)
    ang = pos.astype(F32)[:, None] * inv[None, :]
    cos, sin = jnp.cos(ang), jnp.sin(ang)
    return jnp.concatenate([cos, cos], axis=1), jnp.concatenate([-sin, sin], axis=1)


def kernel(x_prompt, x_sample, cache_k, cache_v, page_table, state_conv, state_pool,
           norm_mix, norm_ffn, norm_final, w_in, conv_w, w_o, w_pool, pool_scale,
           w_gate, w_up, w_down):
    n_batch, n_seq, d = x_prompt.shape
    n_dec, n_new, _ = x_sample.shape
    depth = norm_mix.shape[0]
    seg = conv_w.shape[2]
    n_heads = seg // HEAD_DIM
    page = cache_k.shape[2]
    past_len = page_table.shape[1] * page
    assert n_dec == SUBLANES and n_new <= SUBLANES and past_len % MOBA_BLOCK == 0
    assert cache_k.shape[3] == n_heads and cache_k.shape[4] == HEAD_DIM
    q_scale = HEAD_DIM ** -0.5
    q_scale_log2 = q_scale * LOG2_E
    t_pad = SUBLANES
    own_keys = 128 // n_heads
    assert n_new <= own_keys

    w_in_b, w_o_b, w_pool_b = w_in.astype(BF16), w_o.astype(BF16), w_pool.astype(BF16)
    w_gate_b, w_up_b, w_down_b = w_gate.astype(BF16), w_up.astype(BF16), w_down.astype(BF16)
    row = lambda a: a.reshape(1, -1)

    xp = x_prompt.reshape(n_batch * n_seq, d)
    xs = x_sample.transpose(1, 0, 2).reshape(n_new * n_dec, d)
    ms = n_new * n_dec
    bm = 512
    tps = n_seq // bm
    cos_p, sin_p = _rope_tables(jnp.arange(n_seq, dtype=jnp.int32))
    pos_s = past_len + jnp.repeat(jnp.arange(n_new, dtype=jnp.int32), n_dec)
    cos_s, sin_s = _rope_tables(pos_s)

    def tm(a):
        return a.transpose(1, 0, 2).reshape(1, a.shape[1] * n_dec, a.shape[2])

    def untm(a, rows):
        return a.reshape(rows, n_dec, a.shape[-1]).transpose(1, 0, 2)

    k_p, v_p, k_s, v_s, conv_p, conv_s, pool_p, pool_s = [], [], [], [], [], [], [], []
    for layer in range(depth):
        if layer % 2 == 0:
            e = layer // 2
            q, k, v, yc, cst = _inproj(
                xp, row(norm_mix[layer]), w_in_b, cos_p, sin_p, conv_w[e],
                jnp.zeros((n_batch, CONV_HIST, seg), F32),
                layer=e, bm=bm, tps=tps, stride=1, q_scale=q_scale_log2)
            attn = _moba_prompt(q, k, v, n_batch=n_batch, seq=n_seq, n_heads=n_heads)
            xp = _outproj(attn, yc, w_o_b, xp, layer=e, bm=bm, bn=1024)
            k_p.append(k.reshape(n_batch, n_seq, n_heads, HEAD_DIM))
            v_p.append(v.reshape(n_batch, n_seq, n_heads, HEAD_DIM))
            conv_p.append(cst)
            q, k, v, yc, cst = _inproj(
                xs, row(norm_mix[layer]), w_in_b, cos_s, sin_s, conv_w[e], tm(state_conv[e]),
                layer=e, bm=ms, tps=1, stride=n_dec, q_scale=q_scale)
            heads = lambda a: a.reshape(n_new, n_dec, n_heads, HEAD_DIM)
            q_all = jnp.pad(heads(q).transpose(1, 2, 0, 3), ((0, 0), (0, 0), (0, t_pad - n_new), (0, 0)))
            q_all = q_all.reshape(n_dec, n_heads * t_pad, HEAD_DIM)
            k_new = heads(k).transpose(1, 0, 2, 3)
            v_new = heads(v).transpose(1, 0, 2, 3)
            own = lambda a: jnp.pad(a, ((0, 0), (0, own_keys - n_new), (0, 0), (0, 0))).reshape(
                n_dec, own_keys * n_heads, HEAD_DIM)
            o = _sample_attention(q_all, own(k_new), own(v_new), cache_k, cache_v, page_table,
                                  layer=e, n_heads=n_heads, t_pad=t_pad)
            o = o.reshape(n_dec, n_heads, t_pad, HEAD_DIM)[:, :, :n_new]
            attn = o.transpose(2, 0, 1, 3).reshape(ms, seg).astype(BF16)
            xs = _outproj(attn, yc, w_o_b, xs, layer=e, bm=ms, bn=1024)
            k_s.append(k_new)
            v_s.append(v_new)
            conv_s.append(untm(cst[0], CONV_HIST))
        else:
            o_ = layer // 2
            xp, pst = _pool(xp, row(norm_mix[layer]), w_pool_b, row(pool_scale[o_]),
                            jnp.zeros((n_batch, POOL_HIST, d), F32),
                            layer=o_, bm=bm, tps=tps, stride=1, pos_base=0)
            pool_p.append(pst)
            xs, pst = _pool(xs, row(norm_mix[layer]), w_pool_b, row(pool_scale[o_]),
                            tm(state_pool[o_]), layer=o_, bm=ms, tps=1, stride=n_dec,
                            pos_base=past_len)
            pool_s.append(untm(pst[0], POOL_HIST))
        last = layer == depth - 1
        ffn = functools.partial(_ffn, nw=row(norm_ffn[layer]), wg=w_gate_b, wu=w_up_b, wd=w_down_b,
                                fw=row(norm_final), layer=layer, bf=512, final_norm=last)
        xp = ffn(xp, bm=bm)
        xs = ffn(xs, bm=ms)

    y_prompt = xp.reshape(n_batch, n_seq, d)
    y_sample = xs.reshape(n_new, n_dec, d).transpose(1, 0, 2)
    return (y_prompt, y_sample, jnp.stack(k_p), jnp.stack(v_p), jnp.stack(k_s), jnp.stack(v_s),
            jnp.stack(conv_p), jnp.stack(conv_s), jnp.stack(pool_p), jnp.stack(pool_s))
```

```python
import functools

import jax
import jax.numpy as jnp
from jax import lax
from jax.experimental import pallas as pl
from jax.experimental.pallas import tpu as pltpu

F32 = jnp.float32
BF16 = jnp.bfloat16

HEAD_DIM = 128
MOBA_BLOCK = 256
MOBA_TOPK = 3
CONV_K = 3
CONV_HIST = CONV_K - 1
ROPE_THETA = 10000.0
POOL_WINDOWS = (2, 4, 8, 16)
POOL_HIST = max(POOL_WINDOWS) - 1
RMS_EPS = 1e-6
NEG_INF = -1e30
LOG2_E = 1.4426950408889634

SUBLANES = 8
MXU_DIM = 256
VMEM_LIMIT = 56 * 1024 * 1024
PAGES_PER_STEP = 32

_NT = (((1,), (1,)), ((), ()))


def _rmsnorm(x, g):
    ms = jnp.mean(x * x, axis=-1, keepdims=True)
    return x * lax.rsqrt(ms + RMS_EPS) * g


def _halo_base(rows):
    return -(-rows // SUBLANES) * SUBLANES


def _qkv_kernel(x_ref, nw_ref, w_ref, cos_ref, sin_ref, q_ref, k_ref, v_ref, *, n_heads, q_scale):
    seg = n_heads * HEAD_DIM
    h = _rmsnorm(x_ref[...], nw_ref[...]).astype(BF16)
    cos = cos_ref[...]
    sin = sin_ref[...]

    def rope(a, hh):
        part = a[:, hh * HEAD_DIM:(hh + 1) * HEAD_DIM]
        return part * cos + pltpu.roll(part, HEAD_DIM // 2, axis=1) * sin

    aq = jnp.dot(h, w_ref[:, 0:seg], preferred_element_type=F32)
    for hh in range(n_heads):
        q_ref[:, hh * HEAD_DIM:(hh + 1) * HEAD_DIM] = (rope(aq, hh) * q_scale).astype(BF16)
    ak = jnp.dot(h, w_ref[:, seg:2 * seg], preferred_element_type=F32)
    for hh in range(n_heads):
        k_ref[:, hh * HEAD_DIM:(hh + 1) * HEAD_DIM] = rope(ak, hh)
    v_ref[...] = jnp.dot(h, w_ref[:, 2 * seg:3 * seg], preferred_element_type=F32)


def _gconv_kernel(x_ref, nw_ref, w_ref, cw_ref, hist_ref, yc_ref, cst_ref, ue_scr,
                  *, bm, tps, stride):
    m = pl.program_id(0)
    seg = yc_ref.shape[1]
    hs = CONV_HIST * stride
    base = _halo_base(hs)
    h = _rmsnorm(x_ref[...], nw_ref[...]).astype(BF16)
    u = (jnp.dot(h, w_ref[:, seg:2 * seg], preferred_element_type=F32)
         * jnp.dot(h, w_ref[:, 2 * seg:3 * seg], preferred_element_type=F32))
    first = (m % tps) == 0

    @pl.when(first)
    def _():
        ue_scr[base - hs:base, :] = hist_ref[...]

    @pl.when(jnp.logical_not(first))
    def _():
        ue_scr[base - hs:base, :] = ue_scr[base + bm - hs:base + bm, :]

    ue_scr[base:base + bm, :] = u
    cw = cw_ref[...]
    conv = u * cw[CONV_K - 1:CONV_K, :]
    for j in range(CONV_K - 1):
        off = base - (CONV_K - 1 - j) * stride
        conv = conv + ue_scr[off:off + bm, :] * cw[j:j + 1, :]
    gate_b = jnp.dot(h, w_ref[:, 0:seg], preferred_element_type=F32)
    yc_ref[...] = (gate_b * conv).astype(BF16)
    cst_ref[...] = ue_scr[base + bm - hs:base + bm, :]


def _inproj(x, nw, w_in, cos, sin, conv_w, hist, *, layer, bm, tps, stride, q_scale):
    M, D = x.shape
    seg = conv_w.shape[1]
    assert w_in.shape[2] == 6 * seg and M % bm == 0 and (M // bm) % tps == 0
    n_seq = hist.shape[0]
    hs = CONV_HIST * stride
    assert hist.shape == (n_seq, hs, seg) and bm >= hs
    base = _halo_base(hs)
    params = pltpu.CompilerParams(dimension_semantics=("arbitrary",), vmem_limit_bytes=VMEM_LIMIT)
    rows = lambda width: pl.BlockSpec((bm, width), lambda m: (m, 0))
    half = lambda n: pl.BlockSpec((None, D, 3 * seg), lambda m: (layer, 0, n),
                                  pipeline_mode=pl.Buffered(1))
    norm = pl.BlockSpec((1, D), lambda m: (0, 0))
    table = pl.BlockSpec((bm, HEAD_DIM), lambda m: (m % tps, 0))
    q, k, v = pl.pallas_call(
        functools.partial(_qkv_kernel, n_heads=seg // HEAD_DIM, q_scale=q_scale),
        grid=(M // bm,),
        in_specs=[rows(D), norm, half(0), table, table],
        out_specs=[rows(seg), rows(seg), rows(seg)],
        out_shape=[
            jax.ShapeDtypeStruct((M, seg), BF16),
            jax.ShapeDtypeStruct((M, seg), F32),
            jax.ShapeDtypeStruct((M, seg), F32),
        ],
        compiler_params=params,
        name="inproj_qkv",
    )(x, nw, w_in, cos, sin)
    state = pl.BlockSpec((None, hs, seg), lambda m: (m // tps, 0, 0))
    yc, cst = pl.pallas_call(
        functools.partial(_gconv_kernel, bm=bm, tps=tps, stride=stride),
        grid=(M // bm,),
        in_specs=[rows(D), norm, half(1), pl.BlockSpec((CONV_K, seg), lambda m: (0, 0)), state],
        out_specs=[rows(seg), state],
        out_shape=[
            jax.ShapeDtypeStruct((M, seg), BF16),
            jax.ShapeDtypeStruct((n_seq, hs, seg), F32),
        ],
        scratch_shapes=[pltpu.VMEM((base + bm, seg), F32)],
        compiler_params=params,
        name="inproj_gconv",
    )(x, nw, w_in, conv_w, hist)
    return q, k, v, yc, cst


def _group_reduce(x, op):
    return op(x.reshape(x.shape[0] // SUBLANES, SUBLANES, x.shape[1]), axis=0)


def _moba_kernel(q_ref, k_ref, v_ref, o_ref, kb_scr, vt_scr, ks_scr, ksh_scr, ksl_scr,
                 qt_scr, s_scr, *, nq):
    i = pl.program_id(2)
    blk = MOBA_BLOCK

    @pl.when(i == 0)
    def _():
        for j in range(nq):
            kf = k_ref[j * blk:(j + 1) * blk, :]
            kb_scr[j] = kf.astype(BF16)
            vt_scr[j] = v_ref[j * blk:(j + 1) * blk, :].T.astype(BF16)
            ks_scr[j:j + 1, :] = jnp.sum(kf, axis=0, keepdims=True)
        ks = ks_scr[...]
        hi = ks.astype(BF16)
        ksh_scr[...] = hi
        ksl_scr[...] = (ks - hi.astype(F32)).astype(BF16)

    for pair in range(nq // 2):
        @pl.when(i == pair)
        def _(pair=pair):
            _moba_pair(pair, q_ref, o_ref, kb_scr, vt_scr, ksh_scr, ksl_scr, qt_scr, s_scr, nq=nq)


def _moba_pair(pair, q_ref, o_ref, kb_scr, vt_scr, ksh_scr, ksl_scr, qt_scr, s_scr, *, nq):
    blk = MOBA_BLOCK
    owns = (pair, nq - 1 - pair)
    blk_id = lax.broadcasted_iota(jnp.int32, (nq, blk), 0)
    sels = []
    for w, own in enumerate(owns):
        qt = q_ref[own * blk:(own + 1) * blk, :].astype(F32).T.astype(BF16)
        qt_scr[w] = qt
        gate = (jnp.dot(ksh_scr[...], qt, preferred_element_type=F32)
                + jnp.dot(ksl_scr[...], qt, preferred_element_type=F32))
        g = jnp.where(blk_id < own, gate, -jnp.inf)
        sel = jnp.zeros((nq, blk), jnp.bool_)
        for _ in range(min(MOBA_TOPK, own)):
            top = jnp.max(g, axis=0, keepdims=True)
            first = jnp.min(jnp.where(g == top, blk_id, nq), axis=0, keepdims=True)
            hit = blk_id == first
            sel = sel | hit
            g = jnp.where(hit, -jnp.inf, g)
        sels.append(jnp.where(sel, 1.0, 0.0))

    tiles = [(w, j) for w, own in enumerate(owns) for j in range(own)]
    n_gen = len(tiles)

    mx = [None, None]
    for t, (w, j) in enumerate(tiles):
        s = jnp.dot(kb_scr[j], qt_scr[w], preferred_element_type=F32)
        s = jnp.where(sels[w][j:j + 1, :] > 0.0, s, NEG_INF)
        s_scr[t] = s
        cm = _group_reduce(s, jnp.max)
        mx[w] = cm if mx[w] is None else jnp.maximum(mx[w], cm)
    key_id = lax.broadcasted_iota(jnp.int32, (blk, blk), 0)
    qry_id = lax.broadcasted_iota(jnp.int32, (blk, blk), 1)
    for w, own in enumerate(owns):
        s = jnp.dot(kb_scr[own], qt_scr[w], preferred_element_type=F32)
        s = jnp.where(key_id <= qry_id, s, NEG_INF)
        s_scr[n_gen + w] = s
        cm = _group_reduce(s, jnp.max)
        mx[w] = cm if mx[w] is None else jnp.maximum(mx[w], cm)
    mrow = [jnp.max(m, axis=0, keepdims=True) for m in mx]

    acc = [None, None]
    den = [None, None]
    order = list(enumerate(tiles)) + [(n_gen + w, (w, own)) for w, own in enumerate(owns)]
    for t, (w, j) in order:
        p = jnp.exp2(s_scr[t] - mrow[w])
        cs = _group_reduce(p, jnp.sum)
        den[w] = cs if den[w] is None else den[w] + cs
        r = jnp.dot(vt_scr[j], p.astype(BF16), preferred_element_type=F32)
        acc[w] = r if acc[w] is None else acc[w] + r
    for w, own in enumerate(owns):
        tot = jnp.sum(den[w], axis=0, keepdims=True)
        o_ref[own * blk:(own + 1) * blk, :] = (acc[w] / tot).T.astype(BF16)


def _moba_prompt(q, k, v, *, n_batch, seq, n_heads):
    blk = MOBA_BLOCK
    assert seq % (2 * blk) == 0
    nq = seq // blk
    width = n_heads * HEAD_DIM
    col = pl.BlockSpec((None, seq, HEAD_DIM), lambda b, h, i: (b, 0, h))
    out = pl.pallas_call(
        functools.partial(_moba_kernel, nq=nq),
        grid=(n_batch, n_heads, nq // 2),
        in_specs=[col, col, col],
        out_specs=col,
        out_shape=jax.ShapeDtypeStruct((n_batch, seq, width), BF16),
        scratch_shapes=[
            pltpu.VMEM((nq, blk, HEAD_DIM), BF16),
            pltpu.VMEM((nq, HEAD_DIM, blk), BF16),
            pltpu.VMEM((nq, HEAD_DIM), F32),
            pltpu.VMEM((nq, HEAD_DIM), BF16),
            pltpu.VMEM((nq, HEAD_DIM), BF16),
            pltpu.VMEM((2, HEAD_DIM, blk), BF16),
            pltpu.VMEM((nq + 1, blk, blk), F32),
        ],
        compiler_params=pltpu.CompilerParams(
            dimension_semantics=("arbitrary", "arbitrary", "arbitrary"),
            vmem_limit_bytes=VMEM_LIMIT),
        name="moba_prompt",
    )(q.reshape(n_batch, seq, width), k.reshape(n_batch, seq, width),
      v.reshape(n_batch, seq, width))
    return out.reshape(n_batch * seq, width)


def _outproj_kernel(a_ref, c_ref, wt_ref, wb_ref, x_ref, o_ref):
    o_ref[...] = (x_ref[...]
                  + jnp.dot(a_ref[...], wt_ref[...], preferred_element_type=F32)
                  + jnp.dot(c_ref[...], wb_ref[...], preferred_element_type=F32))


def _outproj(a, c, w_o, x, *, layer, bm, bn):
    M, D = x.shape
    half = a.shape[1]
    assert w_o.shape[1:] == (2 * half, D) and c.shape == a.shape and D % bn == 0 and M % bm == 0
    w_mode = pl.Buffered(1) if bn == D else None
    return pl.pallas_call(
        _outproj_kernel,
        grid=(M // bm, D // bn),
        in_specs=[
            pl.BlockSpec((bm, half), lambda m, n: (m, 0)),
            pl.BlockSpec((bm, half), lambda m, n: (m, 0)),
            pl.BlockSpec((None, half, bn), lambda m, n: (layer, 0, n), pipeline_mode=w_mode),
            pl.BlockSpec((None, half, bn), lambda m, n: (layer, 1, n), pipeline_mode=w_mode),
            pl.BlockSpec((bm, bn), lambda m, n: (m, n)),
        ],
        out_specs=pl.BlockSpec((bm, bn), lambda m, n: (m, n)),
        out_shape=jax.ShapeDtypeStruct((M, D), F32),
        compiler_params=pltpu.CompilerParams(
            dimension_semantics=("arbitrary", "arbitrary"), vmem_limit_bytes=VMEM_LIMIT),
        name="outproj",
    )(a, c, w_o, w_o, x)


def _ffn_kernel(x_ref, nw_ref, wg_ref, wu_ref, wd_ref, fw_ref, o_ref, h_scr, *, final_norm):
    f = pl.program_id(1)

    @pl.when(f == 0)
    def _():
        x = x_ref[...]
        h_scr[...] = _rmsnorm(x, nw_ref[...]).astype(BF16)
        o_ref[...] = x

    h = h_scr[...]
    g = jnp.dot(h, wg_ref[...], preferred_element_type=F32)
    u = jnp.dot(h, wu_ref[...], preferred_element_type=F32)
    a = (g * jax.nn.sigmoid(g) * u).astype(BF16)
    o_ref[...] += jnp.dot(a, wd_ref[...], preferred_element_type=F32)

    if final_norm:
        @pl.when(f == pl.num_programs(1) - 1)
        def _():
            o_ref[...] = _rmsnorm(o_ref[...], fw_ref[...])


def _ffn(x, nw, wg, wu, wd, fw, *, layer, bm, bf, final_norm):
    M, D = x.shape
    FF = wg.shape[2]
    assert FF % bf == 0 and M % bm == 0
    kern = functools.partial(_ffn_kernel, final_norm=final_norm)
    return pl.pallas_call(
        kern,
        grid=(M // bm, FF // bf),
        in_specs=[
            pl.BlockSpec((bm, D), lambda m, f: (m, 0)),
            pl.BlockSpec((1, D), lambda m, f: (0, 0)),
            pl.BlockSpec((None, D, bf), lambda m, f: (layer, 0, f)),
            pl.BlockSpec((None, D, bf), lambda m, f: (layer, 0, f)),
            pl.BlockSpec((None, bf, D), lambda m, f: (layer, f, 0)),
            pl.BlockSpec((1, D), lambda m, f: (0, 0)),
        ],
        out_specs=pl.BlockSpec((bm, D), lambda m, f: (m, 0)),
        out_shape=jax.ShapeDtypeStruct((M, D), F32),
        scratch_shapes=[pltpu.VMEM((bm, D), BF16)],
        compiler_params=pltpu.CompilerParams(
            dimension_semantics=("arbitrary", "arbitrary"), vmem_limit_bytes=VMEM_LIMIT),
        name="ffn",
    )(x, nw, wg, wu, wd, fw)


def _pool_kernel(x_ref, nw_ref, wp_ref, ps_ref, hist_ref, o_ref, pst_ref, ext_scr, band_scr,
                 *, bm, tps, stride, pos_base):
    m = pl.program_id(0)
    hs = POOL_HIST * stride
    base = _halo_base(hs)
    rb = band_scr.shape[1]
    nd = min(rb, base)
    assert bm % rb == 0
    assert nd == rb or (stride == 1 and nd % SUBLANES == 0 and nd >= max(POOL_WINDOWS))
    grp = wp_ref.shape[1]
    first = (m % tps) == 0

    if nd < rb:
        @pl.when(m == 0)
        def _():
            lag = (lax.broadcasted_iota(jnp.int32, (rb, rb), 0)
                   - lax.broadcasted_iota(jnp.int32, (rb, rb), 1))
            for gi, w in enumerate(POOL_WINDOWS):
                band_scr[gi] = jnp.where((lag >= 0) & (lag < w), 1.0, 0.0).astype(BF16)

    @pl.when(first)
    def _():
        ext_scr[base - hs:base, :] = hist_ref[...]

    @pl.when(jnp.logical_not(first))
    def _():
        ext_scr[base - hs:base, :] = ext_scr[base + bm - hs:base + bm, :]

    ext_scr[base:base + bm, :] = _rmsnorm(x_ref[...], nw_ref[...])
    for r0 in range(0, bm, rb):
        r = (m % tps) * bm + r0 + lax.broadcasted_iota(jnp.int32, (nd, 1), 0)
        pos = pos_base + (r if stride == 1 else lax.shift_right_logical(r, stride.bit_length() - 1))
        for gi, w in enumerate(POOL_WINDOWS):
            cols = slice(gi * grp, (gi + 1) * grp)
            hb = ext_scr[base + r0:base + r0 + rb, cols]
            tot = hb[:nd]
            for kk in range(1, w):
                off = base + r0 - kk * stride
                tot = tot + ext_scr[off:off + nd, cols]
            d = tot / jnp.minimum(pos + 1, w).astype(F32) - hb[:nd]
            if nd < rb:
                hi = hb.astype(BF16)
                lo = (hb - hi.astype(F32)).astype(BF16)
                band = band_scr[gi]
                full = (jnp.dot(band, hi, preferred_element_type=F32)
                        + jnp.dot(band, lo, preferred_element_type=F32))
                d = jnp.concatenate([d, full[nd:] * (1.0 / w) - hb[nd:]], axis=0)
            out = jnp.dot(d.astype(BF16), wp_ref[gi], preferred_element_type=F32)
            o_ref[r0:r0 + rb, cols] = x_ref[r0:r0 + rb, cols] + out * ps_ref[:, cols]
    pst_ref[...] = ext_scr[base + bm - hs:base + bm, :]


def _pool(x, nw, w_pool, pool_scale, hist, *, layer, bm, tps, stride, pos_base):
    M, D = x.shape
    n_seq = hist.shape[0]
    hs = POOL_HIST * stride
    assert stride & (stride - 1) == 0 and hist.shape == (n_seq, hs, D) and M % bm == 0
    base = _halo_base(hs)
    _, ng, grp, _ = w_pool.shape
    assert ng == len(POOL_WINDOWS) and ng * grp == D
    kern = functools.partial(_pool_kernel, bm=bm, tps=tps, stride=stride, pos_base=pos_base)
    return pl.pallas_call(
        kern,
        grid=(M // bm,),
        in_specs=[
            pl.BlockSpec((bm, D), lambda m: (m, 0)),
            pl.BlockSpec((1, D), lambda m: (0, 0)),
            pl.BlockSpec((None, ng, grp, grp), lambda m: (layer, 0, 0, 0)),
            pl.BlockSpec((1, D), lambda m: (0, 0)),
            pl.BlockSpec((None, hs, D), lambda m: (m // tps, 0, 0)),
        ],
        out_specs=[
            pl.BlockSpec((bm, D), lambda m: (m, 0)),
            pl.BlockSpec((None, hs, D), lambda m: (m // tps, 0, 0)),
        ],
        out_shape=[
            jax.ShapeDtypeStruct((M, D), F32),
            jax.ShapeDtypeStruct((n_seq, hs, D), F32),
        ],
        scratch_shapes=[pltpu.VMEM((base + bm, D), F32),
                        pltpu.VMEM((ng, min(bm, MXU_DIM), min(bm, MXU_DIM)), BF16)],
        compiler_params=pltpu.CompilerParams(
            dimension_semantics=("arbitrary",), vmem_limit_bytes=VMEM_LIMIT),
        name="pool",
    )(x, nw, w_pool, pool_scale, hist)


def _head_match(n_rows, n_cols, n_heads, t_pad):
    rh = lax.broadcasted_iota(jnp.int32, (n_rows, n_cols), 0) // t_pad
    ch = lax.broadcasted_iota(jnp.int32, (n_rows, n_cols), 1) % n_heads
    return rh == ch


def _fold_heads(x, n_heads, t_pad):
    out = x[0:t_pad]
    for hh in range(1, n_heads):
        out = out + x[hh * t_pad:(hh + 1) * t_pad]
    return out


def _lane_class(x, n_heads, op):
    sh = n_heads
    while sh < x.shape[-1]:
        x = op(x, pltpu.roll(x, sh, axis=x.ndim - 1))
        sh *= 2
    return x


def _lane_tiles(x, op):
    out = x[:, 0:128]
    for t in range(1, x.shape[1] // 128):
        out = op(out, x[:, t * 128:(t + 1) * 128])
    return out


def _sa_logits_kernel(pt_ref, q_ref, *refs, n_pages, n_heads, t_pad, pages_per_block):
    k_refs = refs[:n_pages]
    r_ref, g_ref = refs[n_pages:]
    qa = q_ref[...]
    rows = k_refs[0].shape[0] * n_heads
    match = _head_match(qa.shape[0], rows, n_heads, t_pad)
    for pg in range(n_pages):
        k2 = k_refs[pg][...].reshape(rows, HEAD_DIM).astype(BF16)
        l2 = lax.dot_general(qa, k2, _NT, preferred_element_type=F32)
        r_ref[pg] = _fold_heads(jnp.where(match, l2, 0.0), n_heads, t_pad)
    for bb in range(n_pages // pages_per_block):
        tot = _lane_tiles(r_ref[bb * pages_per_block], jnp.add)
        for pg in range(1, pages_per_block):
            tot = tot + _lane_tiles(r_ref[bb * pages_per_block + pg], jnp.add)
        g_ref[bb] = _lane_class(tot, n_heads, jnp.add)


def _sa_values_kernel(pt_ref, r_ref, g_ref, q_ref, kn_ref, vn_ref, *refs,
                      n_pages, n_heads, t_pad, pages_per_block):
    v_refs = refs[:n_pages]
    o_ref, p_scr, sel_scr, l_scr, acc_scr = refs[n_pages:]
    c = pl.program_id(1)
    n_blocks = g_ref.shape[0]
    rows = v_refs[0].shape[0] * n_heads
    n_rows = q_ref.shape[0]
    match = _head_match(n_rows, rows, n_heads, t_pad)

    def spread(p):
        return jnp.where(match[:, :p.shape[1]], jnp.concatenate([p] * n_heads, axis=0), 0.0).astype(BF16)

    @pl.when(c == 0)
    def _():
        gates = g_ref[...]
        idx = lax.broadcasted_iota(jnp.int32, gates.shape, 0)
        sel = jnp.zeros(gates.shape, F32)
        for _ in range(MOBA_TOPK):
            mx = jnp.max(gates, axis=0, keepdims=True)
            first = jnp.min(jnp.where(gates == mx, idx, n_blocks), axis=0, keepdims=True)
            hit = idx == first
            sel = jnp.where(hit, 1.0, sel)
            gates = jnp.where(hit, -jnp.inf, gates)
        sel_scr[...] = sel

        own_w = kn_ref.shape[0]
        lo = lax.dot_general(q_ref[...], kn_ref[...].astype(BF16), _NT, preferred_element_type=F32)
        ro = _fold_heads(jnp.where(match[:, :own_w], lo, 0.0), n_heads, t_pad)
        t_key = lax.broadcasted_iota(jnp.int32, ro.shape, 1) // n_heads
        t_qry = lax.broadcasted_iota(jnp.int32, ro.shape, 0)
        ro = jnp.where(t_key <= t_qry, ro, NEG_INF)

        def max_body(b, mx):
            here = _lane_tiles(r_ref[b * pages_per_block], jnp.maximum)
            for pg in range(1, pages_per_block):
                here = jnp.maximum(here, _lane_tiles(r_ref[b * pages_per_block + pg], jnp.maximum))
            return jnp.maximum(mx, jnp.where(sel_scr[b] > 0.0, here, NEG_INF))

        mx = lax.fori_loop(0, n_blocks, max_body, ro)
        mx = _lane_class(mx, n_heads, jnp.maximum)
        width = r_ref.shape[2]
        mx_w = jnp.concatenate([mx] * (width // 128), axis=1)

        def exp_body(b, tot):
            keep = jnp.concatenate([sel_scr[b]] * (width // 128), axis=1) > 0.0
            for pg in range(pages_per_block):
                p = jnp.where(keep, jnp.exp(r_ref[b * pages_per_block + pg] - mx_w), 0.0)
                p_scr[b * pages_per_block + pg] = p
                tot = tot + _lane_tiles(p, jnp.add)
            return tot

        po = jnp.exp(ro - mx)
        tot = lax.fori_loop(0, n_blocks, exp_body, po)
        l_scr[...] = _lane_class(tot, n_heads, jnp.add)
        acc_scr[...] = jnp.dot(spread(po), vn_ref[...].astype(BF16), preferred_element_type=F32)

    acc = acc_scr[...]
    for pg in range(n_pages):
        v2 = v_refs[pg][...].reshape(rows, HEAD_DIM).astype(BF16)
        acc = acc + jnp.dot(spread(p_scr[c * n_pages + pg]), v2, preferred_element_type=F32)
    acc_scr[...] = acc

    @pl.when(c == pl.num_programs(1) - 1)
    def _():
        l = l_scr[...]
        den = jnp.concatenate([l[:, hh:hh + 1] for hh in range(n_heads)], axis=0)
        o_ref[...] = acc_scr[...] / den


def _sample_attention(q_all, k_new, v_new, cache_k, cache_v, page_table, *, layer, n_heads, t_pad):
    n_dec, n_pt = page_table.shape
    page = cache_k.shape[2]
    ppb = MOBA_BLOCK // page
    P = PAGES_PER_STEP
    assert MOBA_BLOCK % page == 0 and n_pt % P == 0 and P % ppb == 0
    n_steps = n_pt // P
    n_blocks = n_pt // ppb
    width = page * n_heads
    n_rows = n_heads * t_pad

    def page_spec(i):
        return pl.BlockSpec((None, None, page, n_heads, HEAD_DIM),
                            lambda b, c, pt: (layer, pt[b, c * P + i], 0, 0, 0))

    params = pltpu.CompilerParams(dimension_semantics=("arbitrary", "arbitrary"),
                                  vmem_limit_bytes=VMEM_LIMIT)
    common = dict(n_pages=P, n_heads=n_heads, t_pad=t_pad, pages_per_block=ppb)
    logits, gates = pl.pallas_call(
        functools.partial(_sa_logits_kernel, **common),
        grid_spec=pltpu.PrefetchScalarGridSpec(
            num_scalar_prefetch=1,
            grid=(n_dec, n_steps),
            in_specs=[pl.BlockSpec((None, n_rows, HEAD_DIM), lambda b, c, pt: (b, 0, 0))]
                     + [page_spec(i) for i in range(P)],
            out_specs=[
                pl.BlockSpec((None, P, t_pad, width), lambda b, c, pt: (b, c, 0, 0)),
                pl.BlockSpec((None, P // ppb, t_pad, 128), lambda b, c, pt: (b, c, 0, 0)),
            ],
        ),
        out_shape=[
            jax.ShapeDtypeStruct((n_dec, n_pt, t_pad, width), F32),
            jax.ShapeDtypeStruct((n_dec, n_blocks, t_pad, 128), F32),
        ],
        compiler_params=params,
        name="sample_logits",
    )(page_table, q_all, *([cache_k] * P))

    own = k_new.shape[1]
    return pl.pallas_call(
        functools.partial(_sa_values_kernel, **common),
        grid_spec=pltpu.PrefetchScalarGridSpec(
            num_scalar_prefetch=1,
            grid=(n_dec, n_steps),
            in_specs=[
                pl.BlockSpec((None, n_pt, t_pad, width), lambda b, c, pt: (b, 0, 0, 0)),
                pl.BlockSpec((None, n_blocks, t_pad, 128), lambda b, c, pt: (b, 0, 0, 0)),
                pl.BlockSpec((None, n_rows, HEAD_DIM), lambda b, c, pt: (b, 0, 0)),
                pl.BlockSpec((None, own, HEAD_DIM), lambda b, c, pt: (b, 0, 0)),
                pl.BlockSpec((None, own, HEAD_DIM), lambda b, c, pt: (b, 0, 0)),
            ] + [page_spec(i) for i in range(P)],
            out_specs=pl.BlockSpec((None, n_rows, HEAD_DIM), lambda b, c, pt: (b, 0, 0)),
            scratch_shapes=[
                pltpu.VMEM((n_pt, t_pad, width), F32),
                pltpu.VMEM((n_blocks, t_pad, 128), F32),
                pltpu.VMEM((t_pad, 128), F32),
                pltpu.VMEM((n_rows, HEAD_DIM), F32),
            ],
        ),
        out_shape=jax.ShapeDtypeStruct((n_dec, n_rows, HEAD_DIM), F32),
        compiler_params=params,
        name="sample_values",
    )(page_table, logits, gates, q_all, k_new, v_new, *([cache_v] * P))


def _rope_tables(pos):
    half = HEAD_DIM // 2
    inv = ROPE_THETA ** (-jnp.arange(half, dtype=F32) / half)
    ang = pos.astype(F32)[:, None] * inv[None, :]
    cos, sin = jnp.cos(ang), jnp.sin(ang)
    return jnp.concatenate([cos, cos], axis=1), jnp.concatenate([-sin, sin], axis=1)


def kernel(x_prompt, x_sample, cache_k, cache_v, page_table, state_conv, state_pool,
           norm_mix, norm_ffn, norm_final, w_in, conv_w, w_o, w_pool, pool_scale,
           w_gate, w_up, w_down):
    n_batch, n_seq, d = x_prompt.shape
    n_dec, n_new, _ = x_sample.shape
    depth = norm_mix.shape[0]
    seg = conv_w.shape[2]
    n_heads = seg // HEAD_DIM
    page = cache_k.shape[2]
    past_len = page_table.shape[1] * page
    assert n_dec == SUBLANES and n_new <= SUBLANES and past_len % MOBA_BLOCK == 0
    assert cache_k.shape[3] == n_heads and cache_k.shape[4] == HEAD_DIM
    q_scale = HEAD_DIM ** -0.5
    q_scale_log2 = q_scale * LOG2_E
    t_pad = SUBLANES
    own_keys = 128 // n_heads
    assert n_new <= own_keys

    w_in_b, w_o_b, w_pool_b = w_in.astype(BF16), w_o.astype(BF16), w_pool.astype(BF16)
    w_gate_b, w_up_b, w_down_b = w_gate.astype(BF16), w_up.astype(BF16), w_down.astype(BF16)
    row = lambda a: a.reshape(1, -1)

    xp = x_prompt.reshape(n_batch * n_seq, d)
    xs = x_sample.transpose(1, 0, 2).reshape(n_new * n_dec, d)
    ms = n_new * n_dec
    bm = 512
    tps = n_seq // bm
    cos_p, sin_p = _rope_tables(jnp.arange(n_seq, dtype=jnp.int32))
    pos_s = past_len + jnp.repeat(jnp.arange(n_new, dtype=jnp.int32), n_dec)
    cos_s, sin_s = _rope_tables(pos_s)

    def tm(a):
        return a.transpose(1, 0, 2).reshape(1, a.shape[1] * n_dec, a.shape[2])

    def untm(a, rows):
        return a.reshape(rows, n_dec, a.shape[-1]).transpose(1, 0, 2)

    k_p, v_p, k_s, v_s, conv_p, conv_s, pool_p, pool_s = [], [], [], [], [], [], [], []
    for layer in range(depth):
        if layer % 2 == 0:
            e = layer // 2
            q, k, v, yc, cst = _inproj(
                xp, row(norm_mix[layer]), w_in_b, cos_p, sin_p, conv_w[e],
                jnp.zeros((n_batch, CONV_HIST, seg), F32),
                layer=e, bm=bm, tps=tps, stride=1, q_scale=q_scale_log2)
            attn = _moba_prompt(q, k, v, n_batch=n_batch, seq=n_seq, n_heads=n_heads)
            xp = _outproj(attn, yc, w_o_b, xp, layer=e, bm=bm, bn=d)
            k_p.append(k.reshape(n_batch, n_seq, n_heads, HEAD_DIM))
            v_p.append(v.reshape(n_batch, n_seq, n_heads, HEAD_DIM))
            conv_p.append(cst)
            q, k, v, yc, cst = _inproj(
                xs, row(norm_mix[layer]), w_in_b, cos_s, sin_s, conv_w[e], tm(state_conv[e]),
                layer=e, bm=ms, tps=1, stride=n_dec, q_scale=q_scale)
            heads = lambda a: a.reshape(n_new, n_dec, n_heads, HEAD_DIM)
            q_all = jnp.pad(heads(q).transpose(1, 2, 0, 3), ((0, 0), (0, 0), (0, t_pad - n_new), (0, 0)))
            q_all = q_all.reshape(n_dec, n_heads * t_pad, HEAD_DIM)
            k_new = heads(k).transpose(1, 0, 2, 3)
            v_new = heads(v).transpose(1, 0, 2, 3)
            own = lambda a: jnp.pad(a, ((0, 0), (0, own_keys - n_new), (0, 0), (0, 0))).reshape(
                n_dec, own_keys * n_heads, HEAD_DIM)
            o = _sample_attention(q_all, own(k_new), own(v_new), cache_k, cache_v, page_table,
                                  layer=e, n_heads=n_heads, t_pad=t_pad)
            o = o.reshape(n_dec, n_heads, t_pad, HEAD_DIM)[:, :, :n_new]
            attn = o.transpose(2, 0, 1, 3).reshape(ms, seg).astype(BF16)
            xs = _outproj(attn, yc, w_o_b, xs, layer=e, bm=ms, bn=d)
            k_s.append(k_new)
            v_s.append(v_new)
            conv_s.append(untm(cst[0], CONV_HIST))
        else:
            o_ = layer // 2
            xp, pst = _pool(xp, row(norm_mix[layer]), w_pool_b, row(pool_scale[o_]),
                            jnp.zeros((n_batch, POOL_HIST, d), F32),
                            layer=o_, bm=bm, tps=tps, stride=1, pos_base=0)
            pool_p.append(pst)
            xs, pst = _pool(xs, row(norm_mix[layer]), w_pool_b, row(pool_scale[o_]),
                            tm(state_pool[o_]), layer=o_, bm=ms, tps=1, stride=n_dec,
                            pos_base=past_len)
            pool_s.append(untm(pst[0], POOL_HIST))
        last = layer == depth - 1
        ffn = functools.partial(_ffn, nw=row(norm_ffn[layer]), wg=w_gate_b, wu=w_up_b, wd=w_down_b,
                                fw=row(norm_final), layer=layer, bf=512, final_norm=last)
        xp = ffn(xp, bm=bm)
        xs = ffn(xs, bm=ms)

    y_prompt = xp.reshape(n_batch, n_seq, d)
    y_sample = xs.reshape(n_new, n_dec, d).transpose(1, 0, 2)
    return (y_prompt, y_sample, jnp.stack(k_p), jnp.stack(v_p), jnp.stack(k_s), jnp.stack(v_s),
            jnp.stack(conv_p), jnp.stack(conv_s), jnp.stack(pool_p), jnp.stack(pool_s))
```

```python
import functools

import jax
import jax.numpy as jnp
from jax import lax
from jax.experimental import pallas as pl
from jax.experimental.pallas import tpu as pltpu

F32 = jnp.float32
BF16 = jnp.bfloat16

HEAD_DIM = 128
MOBA_BLOCK = 256
MOBA_TOPK = 3
CONV_K = 3
CONV_HIST = CONV_K - 1
ROPE_THETA = 10000.0
POOL_WINDOWS = (2, 4, 8, 16)
POOL_HIST = max(POOL_WINDOWS) - 1
RMS_EPS = 1e-6
NEG_INF = -1e30
LOG2_E = 1.4426950408889634

SUBLANES = 8
MXU_DIM = 256
VMEM_LIMIT = 56 * 1024 * 1024
PAGES_PER_STEP = 32

_NT = (((1,), (1,)), ((), ()))


def _rmsnorm(x, g):
    ms = jnp.mean(x * x, axis=-1, keepdims=True)
    return x * lax.rsqrt(ms + RMS_EPS) * g


def _halo_base(rows):
    return -(-rows // SUBLANES) * SUBLANES


def _qkv_kernel(x_ref, nw_ref, w_ref, cos_ref, sin_ref, q_ref, k_ref, v_ref, *, n_heads, q_scale):
    seg = n_heads * HEAD_DIM
    h = _rmsnorm(x_ref[...], nw_ref[...]).astype(BF16)
    cos = cos_ref[...]
    sin = sin_ref[...]

    def rope(a, hh):
        part = a[:, hh * HEAD_DIM:(hh + 1) * HEAD_DIM]
        return part * cos + pltpu.roll(part, HEAD_DIM // 2, axis=1) * sin

    aq = jnp.dot(h, w_ref[:, 0:seg], preferred_element_type=F32)
    for hh in range(n_heads):
        q_ref[:, hh * HEAD_DIM:(hh + 1) * HEAD_DIM] = (rope(aq, hh) * q_scale).astype(BF16)
    ak = jnp.dot(h, w_ref[:, seg:2 * seg], preferred_element_type=F32)
    for hh in range(n_heads):
        k_ref[:, hh * HEAD_DIM:(hh + 1) * HEAD_DIM] = rope(ak, hh)
    v_ref[...] = jnp.dot(h, w_ref[:, 2 * seg:3 * seg], preferred_element_type=F32)


def _gconv_kernel(x_ref, nw_ref, w_ref, cw_ref, hist_ref, yc_ref, cst_ref, ue_scr,
                  *, bm, tps, stride):
    m = pl.program_id(0)
    seg = yc_ref.shape[1]
    hs = CONV_HIST * stride
    base = _halo_base(hs)
    h = _rmsnorm(x_ref[...], nw_ref[...]).astype(BF16)
    u = (jnp.dot(h, w_ref[:, seg:2 * seg], preferred_element_type=F32)
         * jnp.dot(h, w_ref[:, 2 * seg:3 * seg], preferred_element_type=F32))
    first = (m % tps) == 0

    @pl.when(first)
    def _():
        ue_scr[base - hs:base, :] = hist_ref[...]

    @pl.when(jnp.logical_not(first))
    def _():
        ue_scr[base - hs:base, :] = ue_scr[base + bm - hs:base + bm, :]

    ue_scr[base:base + bm, :] = u
    cw = cw_ref[...]
    conv = u * cw[CONV_K - 1:CONV_K, :]
    for j in range(CONV_K - 1):
        off = base - (CONV_K - 1 - j) * stride
        conv = conv + ue_scr[off:off + bm, :] * cw[j:j + 1, :]
    gate_b = jnp.dot(h, w_ref[:, 0:seg], preferred_element_type=F32)
    yc_ref[...] = (gate_b * conv).astype(BF16)
    cst_ref[...] = ue_scr[base + bm - hs:base + bm, :]


def _inproj(x, nw, w_in, cos, sin, conv_w, hist, *, layer, bm, tps, stride, q_scale):
    M, D = x.shape
    seg = conv_w.shape[1]
    assert w_in.shape[2] == 6 * seg and M % bm == 0 and (M // bm) % tps == 0
    n_seq = hist.shape[0]
    hs = CONV_HIST * stride
    assert hist.shape == (n_seq, hs, seg) and bm >= hs
    base = _halo_base(hs)
    params = pltpu.CompilerParams(dimension_semantics=("arbitrary",), vmem_limit_bytes=VMEM_LIMIT)
    rows = lambda width: pl.BlockSpec((bm, width), lambda m: (m, 0))
    half = lambda n: pl.BlockSpec((None, D, 3 * seg), lambda m: (layer, 0, n),
                                  pipeline_mode=pl.Buffered(1))
    norm = pl.BlockSpec((1, D), lambda m: (0, 0))
    table = pl.BlockSpec((bm, HEAD_DIM), lambda m: (m % tps, 0))
    q, k, v = pl.pallas_call(
        functools.partial(_qkv_kernel, n_heads=seg // HEAD_DIM, q_scale=q_scale),
        grid=(M // bm,),
        in_specs=[rows(D), norm, half(0), table, table],
        out_specs=[rows(seg), rows(seg), rows(seg)],
        out_shape=[
            jax.ShapeDtypeStruct((M, seg), BF16),
            jax.ShapeDtypeStruct((M, seg), F32),
            jax.ShapeDtypeStruct((M, seg), F32),
        ],
        compiler_params=params,
        name="inproj_qkv",
    )(x, nw, w_in, cos, sin)
    state = pl.BlockSpec((None, hs, seg), lambda m: (m // tps, 0, 0))
    yc, cst = pl.pallas_call(
        functools.partial(_gconv_kernel, bm=bm, tps=tps, stride=stride),
        grid=(M // bm,),
        in_specs=[rows(D), norm, half(1), pl.BlockSpec((CONV_K, seg), lambda m: (0, 0)), state],
        out_specs=[rows(seg), state],
        out_shape=[
            jax.ShapeDtypeStruct((M, seg), BF16),
            jax.ShapeDtypeStruct((n_seq, hs, seg), F32),
        ],
        scratch_shapes=[pltpu.VMEM((base + bm, seg), F32)],
        compiler_params=params,
        name="inproj_gconv",
    )(x, nw, w_in, conv_w, hist)
    return q, k, v, yc, cst


def _group_reduce(x, op):
    return op(x.reshape(x.shape[0] // SUBLANES, SUBLANES, x.shape[1]), axis=0)


def _moba_kernel(q_ref, k_ref, v_ref, o_ref, kb_scr, vt_scr, ks_scr, ksh_scr, ksl_scr,
                 qt_scr, s_scr, *, nq):
    i = pl.program_id(2)
    blk = MOBA_BLOCK

    @pl.when(i == 0)
    def _():
        for j in range(nq):
            kf = k_ref[j * blk:(j + 1) * blk, :]
            kb_scr[j] = kf.astype(BF16)
            vt_scr[j] = v_ref[j * blk:(j + 1) * blk, :].T.astype(BF16)
            ks_scr[j:j + 1, :] = jnp.sum(kf, axis=0, keepdims=True)
        ks = ks_scr[...]
        hi = ks.astype(BF16)
        ksh_scr[...] = hi
        ksl_scr[...] = (ks - hi.astype(F32)).astype(BF16)

    for pair in range(nq // 2):
        @pl.when(i == pair)
        def _(pair=pair):
            _moba_pair(pair, q_ref, o_ref, kb_scr, vt_scr, ksh_scr, ksl_scr, qt_scr, s_scr, nq=nq)


def _moba_pair(pair, q_ref, o_ref, kb_scr, vt_scr, ksh_scr, ksl_scr, qt_scr, s_scr, *, nq):
    blk = MOBA_BLOCK
    owns = (pair, nq - 1 - pair)
    blk_id = lax.broadcasted_iota(jnp.int32, (nq, blk), 0)
    sels = []
    for w, own in enumerate(owns):
        qt = q_ref[own * blk:(own + 1) * blk, :].astype(F32).T.astype(BF16)
        qt_scr[w] = qt
        gate = (jnp.dot(ksh_scr[...], qt, preferred_element_type=F32)
                + jnp.dot(ksl_scr[...], qt, preferred_element_type=F32))
        g = jnp.where(blk_id < own, gate, -jnp.inf)
        sel = jnp.zeros((nq, blk), jnp.bool_)
        for _ in range(min(MOBA_TOPK, own)):
            top = jnp.max(g, axis=0, keepdims=True)
            first = jnp.min(jnp.where(g == top, blk_id, nq), axis=0, keepdims=True)
            hit = blk_id == first
            sel = sel | hit
            g = jnp.where(hit, -jnp.inf, g)
        sels.append(jnp.where(sel, 1.0, 0.0))

    tiles = [(w, j) for w, own in enumerate(owns) for j in range(own)]
    n_gen = len(tiles)

    mx = [None, None]
    for t, (w, j) in enumerate(tiles):
        s = jnp.dot(kb_scr[j], qt_scr[w], preferred_element_type=F32)
        s = jnp.where(sels[w][j:j + 1, :] > 0.0, s, NEG_INF)
        s_scr[t] = s
        cm = _group_reduce(s, jnp.max)
        mx[w] = cm if mx[w] is None else jnp.maximum(mx[w], cm)
    key_id = lax.broadcasted_iota(jnp.int32, (blk, blk), 0)
    qry_id = lax.broadcasted_iota(jnp.int32, (blk, blk), 1)
    for w, own in enumerate(owns):
        s = jnp.dot(kb_scr[own], qt_scr[w], preferred_element_type=F32)
        s = jnp.where(key_id <= qry_id, s, NEG_INF)
        s_scr[n_gen + w] = s
        cm = _group_reduce(s, jnp.max)
        mx[w] = cm if mx[w] is None else jnp.maximum(mx[w], cm)
    mrow = [jnp.max(m, axis=0, keepdims=True) for m in mx]

    acc = [None, None]
    den = [None, None]
    order = list(enumerate(tiles)) + [(n_gen + w, (w, own)) for w, own in enumerate(owns)]
    for t, (w, j) in order:
        p = jnp.exp2(s_scr[t] - mrow[w])
        cs = _group_reduce(p, jnp.sum)
        den[w] = cs if den[w] is None else den[w] + cs
        r = jnp.dot(vt_scr[j], p.astype(BF16), preferred_element_type=F32)
        acc[w] = r if acc[w] is None else acc[w] + r
    for w, own in enumerate(owns):
        tot = jnp.sum(den[w], axis=0, keepdims=True)
        o_ref[own * blk:(own + 1) * blk, :] = (acc[w] / tot).T.astype(BF16)


def _moba_prompt(q, k, v, *, n_batch, seq, n_heads):
    blk = MOBA_BLOCK
    assert seq % (2 * blk) == 0
    nq = seq // blk
    width = n_heads * HEAD_DIM
    col = pl.BlockSpec((None, seq, HEAD_DIM), lambda b, h, i: (b, 0, h))
    out = pl.pallas_call(
        functools.partial(_moba_kernel, nq=nq),
        grid=(n_batch, n_heads, nq // 2),
        in_specs=[col, col, col],
        out_specs=col,
        out_shape=jax.ShapeDtypeStruct((n_batch, seq, width), BF16),
        scratch_shapes=[
            pltpu.VMEM((nq, blk, HEAD_DIM), BF16),
            pltpu.VMEM((nq, HEAD_DIM, blk), BF16),
            pltpu.VMEM((nq, HEAD_DIM), F32),
            pltpu.VMEM((nq, HEAD_DIM), BF16),
            pltpu.VMEM((nq, HEAD_DIM), BF16),
            pltpu.VMEM((2, HEAD_DIM, blk), BF16),
            pltpu.VMEM((nq + 1, blk, blk), F32),
        ],
        compiler_params=pltpu.CompilerParams(
            dimension_semantics=("arbitrary", "arbitrary", "arbitrary"),
            vmem_limit_bytes=VMEM_LIMIT),
        name="moba_prompt",
    )(q.reshape(n_batch, seq, width), k.reshape(n_batch, seq, width),
      v.reshape(n_batch, seq, width))
    return out.reshape(n_batch * seq, width)


def _outproj_kernel(a_ref, c_ref, wt_ref, wb_ref, x_ref, o_ref):
    o_ref[...] = (x_ref[...]
                  + jnp.dot(a_ref[...], wt_ref[...], preferred_element_type=F32)
                  + jnp.dot(c_ref[...], wb_ref[...], preferred_element_type=F32))


def _outproj(a, c, w_o, x, *, layer, bm, bn):
    M, D = x.shape
    half = a.shape[1]
    assert w_o.shape[1:] == (2 * half, D) and c.shape == a.shape and D % bn == 0 and M % bm == 0
    w_mode = pl.Buffered(1) if bn == D else None
    return pl.pallas_call(
        _outproj_kernel,
        grid=(M // bm, D // bn),
        in_specs=[
            pl.BlockSpec((bm, half), lambda m, n: (m, 0)),
            pl.BlockSpec((bm, half), lambda m, n: (m, 0)),
            pl.BlockSpec((None, half, bn), lambda m, n: (layer, 0, n), pipeline_mode=w_mode),
            pl.BlockSpec((None, half, bn), lambda m, n: (layer, 1, n), pipeline_mode=w_mode),
            pl.BlockSpec((bm, bn), lambda m, n: (m, n)),
        ],
        out_specs=pl.BlockSpec((bm, bn), lambda m, n: (m, n)),
        out_shape=jax.ShapeDtypeStruct((M, D), F32),
        compiler_params=pltpu.CompilerParams(
            dimension_semantics=("arbitrary", "arbitrary"), vmem_limit_bytes=VMEM_LIMIT),
        name="outproj",
    )(a, c, w_o, w_o, x)


def _ffn_kernel(x_ref, nw_ref, wg_ref, wu_ref, wd_ref, fw_ref, o_ref, h_scr, *, final_norm):
    f = pl.program_id(1)

    @pl.when(f == 0)
    def _():
        x = x_ref[...]
        h_scr[...] = _rmsnorm(x, nw_ref[...]).astype(BF16)
        o_ref[...] = x

    h = h_scr[...]
    g = jnp.dot(h, wg_ref[...].astype(BF16), preferred_element_type=F32)
    u = jnp.dot(h, wu_ref[...].astype(BF16), preferred_element_type=F32)
    a = (g * jax.nn.sigmoid(g) * u).astype(BF16)
    o_ref[...] += jnp.dot(a, wd_ref[...].astype(BF16), preferred_element_type=F32)

    if final_norm:
        @pl.when(f == pl.num_programs(1) - 1)
        def _():
            o_ref[...] = _rmsnorm(o_ref[...], fw_ref[...])


def _ffn(x, nw, wg, wu, wd, fw, *, layer, bm, bf, final_norm):
    M, D = x.shape
    FF = wg.shape[2]
    assert FF % bf == 0 and M % bm == 0
    kern = functools.partial(_ffn_kernel, final_norm=final_norm)
    return pl.pallas_call(
        kern,
        grid=(M // bm, FF // bf),
        in_specs=[
            pl.BlockSpec((bm, D), lambda m, f: (m, 0), pipeline_mode=pl.Buffered(1)),
            pl.BlockSpec((1, D), lambda m, f: (0, 0)),
            pl.BlockSpec((None, D, bf), lambda m, f: (layer, 0, f)),
            pl.BlockSpec((None, D, bf), lambda m, f: (layer, 0, f)),
            pl.BlockSpec((None, bf, D), lambda m, f: (layer, f, 0)),
            pl.BlockSpec((1, D), lambda m, f: (0, 0)),
        ],
        out_specs=pl.BlockSpec((bm, D), lambda m, f: (m, 0)),
        out_shape=jax.ShapeDtypeStruct((M, D), F32),
        scratch_shapes=[pltpu.VMEM((bm, D), BF16)],
        compiler_params=pltpu.CompilerParams(
            dimension_semantics=("arbitrary", "arbitrary"), vmem_limit_bytes=VMEM_LIMIT),
        name="ffn",
    )(x, nw, wg, wu, wd, fw)


def _pool_kernel(x_ref, nw_ref, wp_ref, ps_ref, hist_ref, o_ref, pst_ref, ext_scr, band_scr,
                 *, bm, tps, stride, pos_base):
    m = pl.program_id(0)
    hs = POOL_HIST * stride
    base = _halo_base(hs)
    rb = band_scr.shape[1]
    nd = min(rb, base)
    assert bm % rb == 0
    assert nd == rb or (stride == 1 and nd % SUBLANES == 0 and nd >= max(POOL_WINDOWS))
    grp = wp_ref.shape[1]
    first = (m % tps) == 0

    if nd < rb:
        @pl.when(m == 0)
        def _():
            lag = (lax.broadcasted_iota(jnp.int32, (rb, rb), 0)
                   - lax.broadcasted_iota(jnp.int32, (rb, rb), 1))
            for gi, w in enumerate(POOL_WINDOWS):
                band_scr[gi] = jnp.where((lag >= 0) & (lag < w), 1.0, 0.0).astype(BF16)

    @pl.when(first)
    def _():
        ext_scr[base - hs:base, :] = hist_ref[...]

    @pl.when(jnp.logical_not(first))
    def _():
        ext_scr[base - hs:base, :] = ext_scr[base + bm - hs:base + bm, :]

    ext_scr[base:base + bm, :] = _rmsnorm(x_ref[...], nw_ref[...])
    for r0 in range(0, bm, rb):
        r = (m % tps) * bm + r0 + lax.broadcasted_iota(jnp.int32, (nd, 1), 0)
        pos = pos_base + (r if stride == 1 else lax.shift_right_logical(r, stride.bit_length() - 1))
        for gi, w in enumerate(POOL_WINDOWS):
            cols = slice(gi * grp, (gi + 1) * grp)
            hb = ext_scr[base + r0:base + r0 + rb, cols]
            tot = hb[:nd]
            for kk in range(1, w):
                off = base + r0 - kk * stride
                tot = tot + ext_scr[off:off + nd, cols]
            d = tot / jnp.minimum(pos + 1, w).astype(F32) - hb[:nd]
            if nd < rb:
                hi = hb.astype(BF16)
                lo = (hb - hi.astype(F32)).astype(BF16)
                band = band_scr[gi]
                full = (jnp.dot(band, hi, preferred_element_type=F32)
                        + jnp.dot(band, lo, preferred_element_type=F32))
                d = jnp.concatenate([d, full[nd:] * (1.0 / w) - hb[nd:]], axis=0)
            out = jnp.dot(d.astype(BF16), wp_ref[gi], preferred_element_type=F32)
            o_ref[r0:r0 + rb, cols] = x_ref[r0:r0 + rb, cols] + out * ps_ref[:, cols]
    pst_ref[...] = ext_scr[base + bm - hs:base + bm, :]


def _pool(x, nw, w_pool, pool_scale, hist, *, layer, bm, tps, stride, pos_base):
    M, D = x.shape
    n_seq = hist.shape[0]
    hs = POOL_HIST * stride
    assert stride & (stride - 1) == 0 and hist.shape == (n_seq, hs, D) and M % bm == 0
    base = _halo_base(hs)
    _, ng, grp, _ = w_pool.shape
    assert ng == len(POOL_WINDOWS) and ng * grp == D
    kern = functools.partial(_pool_kernel, bm=bm, tps=tps, stride=stride, pos_base=pos_base)
    return pl.pallas_call(
        kern,
        grid=(M // bm,),
        in_specs=[
            pl.BlockSpec((bm, D), lambda m: (m, 0)),
            pl.BlockSpec((1, D), lambda m: (0, 0)),
            pl.BlockSpec((None, ng, grp, grp), lambda m: (layer, 0, 0, 0)),
            pl.BlockSpec((1, D), lambda m: (0, 0)),
            pl.BlockSpec((None, hs, D), lambda m: (m // tps, 0, 0)),
        ],
        out_specs=[
            pl.BlockSpec((bm, D), lambda m: (m, 0)),
            pl.BlockSpec((None, hs, D), lambda m: (m // tps, 0, 0)),
        ],
        out_shape=[
            jax.ShapeDtypeStruct((M, D), F32),
            jax.ShapeDtypeStruct((n_seq, hs, D), F32),
        ],
        scratch_shapes=[pltpu.VMEM((base + bm, D), F32),
                        pltpu.VMEM((ng, min(bm, MXU_DIM), min(bm, MXU_DIM)), BF16)],
        compiler_params=pltpu.CompilerParams(
            dimension_semantics=("arbitrary",), vmem_limit_bytes=VMEM_LIMIT),
        name="pool",
    )(x, nw, w_pool, pool_scale, hist)


def _head_match(n_rows, n_cols, n_heads, t_pad):
    rh = lax.broadcasted_iota(jnp.int32, (n_rows, n_cols), 0) // t_pad
    ch = lax.broadcasted_iota(jnp.int32, (n_rows, n_cols), 1) % n_heads
    return rh == ch


def _fold_heads(x, n_heads, t_pad):
    out = x[0:t_pad]
    for hh in range(1, n_heads):
        out = out + x[hh * t_pad:(hh + 1) * t_pad]
    return out


def _lane_class(x, n_heads, op):
    sh = n_heads
    while sh < x.shape[-1]:
        x = op(x, pltpu.roll(x, sh, axis=x.ndim - 1))
        sh *= 2
    return x


def _lane_tiles(x, op):
    out = x[:, 0:128]
    for t in range(1, x.shape[1] // 128):
        out = op(out, x[:, t * 128:(t + 1) * 128])
    return out


def _sa_logits_kernel(pt_ref, q_ref, *refs, n_pages, n_heads, t_pad, pages_per_block):
    k_refs = refs[:n_pages]
    r_ref, g_ref = refs[n_pages:]
    qa = q_ref[...]
    rows = k_refs[0].shape[0] * n_heads
    match = _head_match(qa.shape[0], rows, n_heads, t_pad)
    for pg in range(n_pages):
        k2 = k_refs[pg][...].reshape(rows, HEAD_DIM).astype(BF16)
        l2 = lax.dot_general(qa, k2, _NT, preferred_element_type=F32)
        r_ref[pg] = _fold_heads(jnp.where(match, l2, 0.0), n_heads, t_pad)
    for bb in range(n_pages // pages_per_block):
        tot = _lane_tiles(r_ref[bb * pages_per_block], jnp.add)
        for pg in range(1, pages_per_block):
            tot = tot + _lane_tiles(r_ref[bb * pages_per_block + pg], jnp.add)
        g_ref[bb] = _lane_class(tot, n_heads, jnp.add)


def _sa_values_kernel(pt_ref, r_ref, g_ref, q_ref, kn_ref, vn_ref, *refs,
                      n_pages, n_heads, t_pad, pages_per_block):
    v_refs = refs[:n_pages]
    o_ref, p_scr, sel_scr, l_scr, acc_scr = refs[n_pages:]
    c = pl.program_id(1)
    n_blocks = g_ref.shape[0]
    rows = v_refs[0].shape[0] * n_heads
    n_rows = q_ref.shape[0]
    match = _head_match(n_rows, rows, n_heads, t_pad)

    def spread(p):
        return jnp.where(match[:, :p.shape[1]], jnp.concatenate([p] * n_heads, axis=0), 0.0).astype(BF16)

    @pl.when(c == 0)
    def _():
        gates = g_ref[...]
        idx = lax.broadcasted_iota(jnp.int32, gates.shape, 0)
        sel = jnp.zeros(gates.shape, F32)
        for _ in range(MOBA_TOPK):
            mx = jnp.max(gates, axis=0, keepdims=True)
            first = jnp.min(jnp.where(gates == mx, idx, n_blocks), axis=0, keepdims=True)
            hit = idx == first
            sel = jnp.where(hit, 1.0, sel)
            gates = jnp.where(hit, -jnp.inf, gates)
        sel_scr[...] = sel

        own_w = kn_ref.shape[0]
        lo = lax.dot_general(q_ref[...], kn_ref[...].astype(BF16), _NT, preferred_element_type=F32)
        ro = _fold_heads(jnp.where(match[:, :own_w], lo, 0.0), n_heads, t_pad)
        t_key = lax.broadcasted_iota(jnp.int32, ro.shape, 1) // n_heads
        t_qry = lax.broadcasted_iota(jnp.int32, ro.shape, 0)
        ro = jnp.where(t_key <= t_qry, ro, NEG_INF)

        def max_body(b, mx):
            here = _lane_tiles(r_ref[b * pages_per_block], jnp.maximum)
            for pg in range(1, pages_per_block):
                here = jnp.maximum(here, _lane_tiles(r_ref[b * pages_per_block + pg], jnp.maximum))
            return jnp.maximum(mx, jnp.where(sel_scr[b] > 0.0, here, NEG_INF))

        mx = lax.fori_loop(0, n_blocks, max_body, ro)
        mx = _lane_class(mx, n_heads, jnp.maximum)
        width = r_ref.shape[2]
        mx_w = jnp.concatenate([mx] * (width // 128), axis=1)

        def exp_body(b, tot):
            keep = jnp.concatenate([sel_scr[b]] * (width // 128), axis=1) > 0.0
            for pg in range(pages_per_block):
                p = jnp.where(keep, jnp.exp(r_ref[b * pages_per_block + pg] - mx_w), 0.0)
                p_scr[b * pages_per_block + pg] = p
                tot = tot + _lane_tiles(p, jnp.add)
            return tot

        po = jnp.exp(ro - mx)
        tot = lax.fori_loop(0, n_blocks, exp_body, po)
        l_scr[...] = _lane_class(tot, n_heads, jnp.add)
        acc_scr[...] = jnp.dot(spread(po), vn_ref[...].astype(BF16), preferred_element_type=F32)

    acc = acc_scr[...]
    for pg in range(n_pages):
        v2 = v_refs[pg][...].reshape(rows, HEAD_DIM).astype(BF16)
        acc = acc + jnp.dot(spread(p_scr[c * n_pages + pg]), v2, preferred_element_type=F32)
    acc_scr[...] = acc

    @pl.when(c == pl.num_programs(1) - 1)
    def _():
        l = l_scr[...]
        den = jnp.concatenate([l[:, hh:hh + 1] for hh in range(n_heads)], axis=0)
        o_ref[...] = acc_scr[...] / den


def _sample_attention(q_all, k_new, v_new, cache_k, cache_v, page_table, *, layer, n_heads, t_pad):
    n_dec, n_pt = page_table.shape
    page = cache_k.shape[2]
    ppb = MOBA_BLOCK // page
    P = PAGES_PER_STEP
    assert MOBA_BLOCK % page == 0 and n_pt % P == 0 and P % ppb == 0
    n_steps = n_pt // P
    n_blocks = n_pt // ppb
    width = page * n_heads
    n_rows = n_heads * t_pad

    def page_spec(i):
        return pl.BlockSpec((None, None, page, n_heads, HEAD_DIM),
                            lambda b, c, pt: (layer, pt[b, c * P + i], 0, 0, 0))

    params = pltpu.CompilerParams(dimension_semantics=("arbitrary", "arbitrary"),
                                  vmem_limit_bytes=VMEM_LIMIT)
    common = dict(n_pages=P, n_heads=n_heads, t_pad=t_pad, pages_per_block=ppb)
    logits, gates = pl.pallas_call(
        functools.partial(_sa_logits_kernel, **common),
        grid_spec=pltpu.PrefetchScalarGridSpec(
            num_scalar_prefetch=1,
            grid=(n_dec, n_steps),
            in_specs=[pl.BlockSpec((None, n_rows, HEAD_DIM), lambda b, c, pt: (b, 0, 0))]
                     + [page_spec(i) for i in range(P)],
            out_specs=[
                pl.BlockSpec((None, P, t_pad, width), lambda b, c, pt: (b, c, 0, 0)),
                pl.BlockSpec((None, P // ppb, t_pad, 128), lambda b, c, pt: (b, c, 0, 0)),
            ],
        ),
        out_shape=[
            jax.ShapeDtypeStruct((n_dec, n_pt, t_pad, width), F32),
            jax.ShapeDtypeStruct((n_dec, n_blocks, t_pad, 128), F32),
        ],
        compiler_params=params,
        name="sample_logits",
    )(page_table, q_all, *([cache_k] * P))

    own = k_new.shape[1]
    return pl.pallas_call(
        functools.partial(_sa_values_kernel, **common),
        grid_spec=pltpu.PrefetchScalarGridSpec(
            num_scalar_prefetch=1,
            grid=(n_dec, n_steps),
            in_specs=[
                pl.BlockSpec((None, n_pt, t_pad, width), lambda b, c, pt: (b, 0, 0, 0)),
                pl.BlockSpec((None, n_blocks, t_pad, 128), lambda b, c, pt: (b, 0, 0, 0)),
                pl.BlockSpec((None, n_rows, HEAD_DIM), lambda b, c, pt: (b, 0, 0)),
                pl.BlockSpec((None, own, HEAD_DIM), lambda b, c, pt: (b, 0, 0)),
                pl.BlockSpec((None, own, HEAD_DIM), lambda b, c, pt: (b, 0, 0)),
            ] + [page_spec(i) for i in range(P)],
            out_specs=pl.BlockSpec((None, n_rows, HEAD_DIM), lambda b, c, pt: (b, 0, 0)),
            scratch_shapes=[
                pltpu.VMEM((n_pt, t_pad, width), F32),
                pltpu.VMEM((n_blocks, t_pad, 128), F32),
                pltpu.VMEM((t_pad, 128), F32),
                pltpu.VMEM((n_rows, HEAD_DIM), F32),
            ],
        ),
        out_shape=jax.ShapeDtypeStruct((n_dec, n_rows, HEAD_DIM), F32),
        compiler_params=params,
        name="sample_values",
    )(page_table, logits, gates, q_all, k_new, v_new, *([cache_v] * P))


def _rope_tables(pos):
    half = HEAD_DIM // 2
    inv = ROPE_THETA ** (-jnp.arange(half, dtype=F32) / half)
    ang = pos.astype(F32)[:, None] * inv[None, :]
    cos, sin = jnp.cos(ang), jnp.sin(ang)
    return jnp.concatenate([cos, cos], axis=1), jnp.concatenate([-sin, sin], axis=1)


def kernel(x_prompt, x_sample, cache_k, cache_v, page_table, state_conv, state_pool,
           norm_mix, norm_ffn, norm_final, w_in, conv_w, w_o, w_pool, pool_scale,
           w_gate, w_up, w_down):
    n_batch, n_seq, d = x_prompt.shape
    n_dec, n_new, _ = x_sample.shape
    depth = norm_mix.shape[0]
    seg = conv_w.shape[2]
    n_heads = seg // HEAD_DIM
    page = cache_k.shape[2]
    past_len = page_table.shape[1] * page
    assert n_dec == SUBLANES and n_new <= SUBLANES and past_len % MOBA_BLOCK == 0
    assert cache_k.shape[3] == n_heads and cache_k.shape[4] == HEAD_DIM
    q_scale = HEAD_DIM ** -0.5
    q_scale_log2 = q_scale * LOG2_E
    t_pad = SUBLANES
    own_keys = 128 // n_heads
    assert n_new <= own_keys

    w_in_b, w_o_b, w_pool_b = w_in.astype(BF16), w_o.astype(BF16), w_pool.astype(BF16)
    row = lambda a: a.reshape(1, -1)

    xp = x_prompt.reshape(n_batch * n_seq, d)
    xs = x_sample.transpose(1, 0, 2).reshape(n_new * n_dec, d)
    ms = n_new * n_dec
    bm = 512
    tps = n_seq // bm
    cos_p, sin_p = _rope_tables(jnp.arange(n_seq, dtype=jnp.int32))
    pos_s = past_len + jnp.repeat(jnp.arange(n_new, dtype=jnp.int32), n_dec)
    cos_s, sin_s = _rope_tables(pos_s)

    def tm(a):
        return a.transpose(1, 0, 2).reshape(1, a.shape[1] * n_dec, a.shape[2])

    def untm(a, rows):
        return a.reshape(rows, n_dec, a.shape[-1]).transpose(1, 0, 2)

    k_p, v_p, k_s, v_s, conv_p, conv_s, pool_p, pool_s = [], [], [], [], [], [], [], []
    for layer in range(depth):
        if layer % 2 == 0:
            e = layer // 2
            q, k, v, yc, cst = _inproj(
                xp, row(norm_mix[layer]), w_in_b, cos_p, sin_p, conv_w[e],
                jnp.zeros((n_batch, CONV_HIST, seg), F32),
                layer=e, bm=bm, tps=tps, stride=1, q_scale=q_scale_log2)
            attn = _moba_prompt(q, k, v, n_batch=n_batch, seq=n_seq, n_heads=n_heads)
            xp = _outproj(attn, yc, w_o_b, xp, layer=e, bm=bm, bn=d)
            k_p.append(k.reshape(n_batch, n_seq, n_heads, HEAD_DIM))
            v_p.append(v.reshape(n_batch, n_seq, n_heads, HEAD_DIM))
            conv_p.append(cst)
            q, k, v, yc, cst = _inproj(
                xs, row(norm_mix[layer]), w_in_b, cos_s, sin_s, conv_w[e], tm(state_conv[e]),
                layer=e, bm=ms, tps=1, stride=n_dec, q_scale=q_scale)
            heads = lambda a: a.reshape(n_new, n_dec, n_heads, HEAD_DIM)
            q_all = jnp.pad(heads(q).transpose(1, 2, 0, 3), ((0, 0), (0, 0), (0, t_pad - n_new), (0, 0)))
            q_all = q_all.reshape(n_dec, n_heads * t_pad, HEAD_DIM)
            k_new = heads(k).transpose(1, 0, 2, 3)
            v_new = heads(v).transpose(1, 0, 2, 3)
            own = lambda a: jnp.pad(a, ((0, 0), (0, own_keys - n_new), (0, 0), (0, 0))).reshape(
                n_dec, own_keys * n_heads, HEAD_DIM)
            o = _sample_attention(q_all, own(k_new), own(v_new), cache_k, cache_v, page_table,
                                  layer=e, n_heads=n_heads, t_pad=t_pad)
            o = o.reshape(n_dec, n_heads, t_pad, HEAD_DIM)[:, :, :n_new]
            attn = o.transpose(2, 0, 1, 3).reshape(ms, seg).astype(BF16)
            xs = _outproj(attn, yc, w_o_b, xs, layer=e, bm=ms, bn=d)
            k_s.append(k_new)
            v_s.append(v_new)
            conv_s.append(untm(cst[0], CONV_HIST))
        else:
            o_ = layer // 2
            xp, pst = _pool(xp, row(norm_mix[layer]), w_pool_b, row(pool_scale[o_]),
                            jnp.zeros((n_batch, POOL_HIST, d), F32),
                            layer=o_, bm=bm, tps=tps, stride=1, pos_base=0)
            pool_p.append(pst)
            xs, pst = _pool(xs, row(norm_mix[layer]), w_pool_b, row(pool_scale[o_]),
                            tm(state_pool[o_]), layer=o_, bm=ms, tps=1, stride=n_dec,
                            pos_base=past_len)
            pool_s.append(untm(pst[0], POOL_HIST))
        last = layer == depth - 1
        ffn = functools.partial(_ffn, nw=row(norm_ffn[layer]), wg=w_gate, wu=w_up, wd=w_down,
                                fw=row(norm_final), layer=layer, bf=256, final_norm=last)
        xp = ffn(xp, bm=2 * bm)
        xs = ffn(xs, bm=ms)

    y_prompt = xp.reshape(n_batch, n_seq, d)
    y_sample = xs.reshape(n_new, n_dec, d).transpose(1, 0, 2)
    return (y_prompt, y_sample, jnp.stack(k_p), jnp.stack(v_p), jnp.stack(k_s), jnp.stack(v_s),
            jnp.stack(conv_p), jnp.stack(conv_s), jnp.stack(pool_p), jnp.stack(pool_s))
```

```python
import functools

import jax
import jax.numpy as jnp
from jax import lax
from jax.experimental import pallas as pl
from jax.experimental.pallas import tpu as pltpu

F32 = jnp.float32
BF16 = jnp.bfloat16

HEAD_DIM = 128
MOBA_BLOCK = 256
MOBA_TOPK = 3
CONV_K = 3
CONV_HIST = CONV_K - 1
ROPE_THETA = 10000.0
POOL_WINDOWS = (2, 4, 8, 16)
POOL_HIST = max(POOL_WINDOWS) - 1
RMS_EPS = 1e-6
NEG_INF = -1e30
LOG2_E = 1.4426950408889634

SUBLANES = 8
MXU_DIM = 256
VMEM_LIMIT = 56 * 1024 * 1024
PAGES_PER_STEP = 32

_NT = (((1,), (1,)), ((), ()))


def _rmsnorm(x, g):
    ms = jnp.mean(x * x, axis=-1, keepdims=True)
    return x * lax.rsqrt(ms + RMS_EPS) * g


def _halo_base(rows):
    return -(-rows // SUBLANES) * SUBLANES


def _qkv_kernel(x_ref, nw_ref, w_ref, cos_ref, sin_ref, q_ref, k_ref, v_ref, *, n_heads, q_scale):
    seg = n_heads * HEAD_DIM
    h = _rmsnorm(x_ref[...], nw_ref[...]).astype(BF16)
    cos = cos_ref[...]
    sin = sin_ref[...]

    def rope(a, hh):
        part = a[:, hh * HEAD_DIM:(hh + 1) * HEAD_DIM]
        return part * cos + pltpu.roll(part, HEAD_DIM // 2, axis=1) * sin

    aq = jnp.dot(h, w_ref[:, 0:seg], preferred_element_type=F32)
    for hh in range(n_heads):
        q_ref[:, hh * HEAD_DIM:(hh + 1) * HEAD_DIM] = (rope(aq, hh) * q_scale).astype(BF16)
    ak = jnp.dot(h, w_ref[:, seg:2 * seg], preferred_element_type=F32)
    for hh in range(n_heads):
        k_ref[:, hh * HEAD_DIM:(hh + 1) * HEAD_DIM] = rope(ak, hh)
    v_ref[...] = jnp.dot(h, w_ref[:, 2 * seg:3 * seg], preferred_element_type=F32)


def _gconv_kernel(x_ref, nw_ref, w_ref, cw_ref, hist_ref, yc_ref, cst_ref, ue_scr,
                  *, bm, tps, stride):
    m = pl.program_id(0)
    seg = yc_ref.shape[1]
    hs = CONV_HIST * stride
    base = _halo_base(hs)
    h = _rmsnorm(x_ref[...], nw_ref[...]).astype(BF16)
    u = (jnp.dot(h, w_ref[:, seg:2 * seg], preferred_element_type=F32)
         * jnp.dot(h, w_ref[:, 2 * seg:3 * seg], preferred_element_type=F32))
    first = (m % tps) == 0

    @pl.when(first)
    def _():
        ue_scr[base - hs:base, :] = hist_ref[...]

    @pl.when(jnp.logical_not(first))
    def _():
        ue_scr[base - hs:base, :] = ue_scr[base + bm - hs:base + bm, :]

    ue_scr[base:base + bm, :] = u
    cw = cw_ref[...]
    conv = u * cw[CONV_K - 1:CONV_K, :]
    for j in range(CONV_K - 1):
        off = base - (CONV_K - 1 - j) * stride
        conv = conv + ue_scr[off:off + bm, :] * cw[j:j + 1, :]
    gate_b = jnp.dot(h, w_ref[:, 0:seg], preferred_element_type=F32)
    yc_ref[...] = (gate_b * conv).astype(BF16)
    cst_ref[...] = ue_scr[base + bm - hs:base + bm, :]


def _inproj(x, nw, w_in, cos, sin, conv_w, hist, *, layer, bm, tps, stride, q_scale):
    M, D = x.shape
    seg = conv_w.shape[1]
    assert w_in.shape[2] == 6 * seg and M % bm == 0 and (M // bm) % tps == 0
    n_seq = hist.shape[0]
    hs = CONV_HIST * stride
    assert hist.shape == (n_seq, hs, seg) and bm >= hs
    base = _halo_base(hs)
    params = pltpu.CompilerParams(dimension_semantics=("arbitrary",), vmem_limit_bytes=VMEM_LIMIT)
    rows = lambda width: pl.BlockSpec((bm, width), lambda m: (m, 0))
    half = lambda n: pl.BlockSpec((None, D, 3 * seg), lambda m: (layer, 0, n),
                                  pipeline_mode=pl.Buffered(1))
    norm = pl.BlockSpec((1, D), lambda m: (0, 0))
    table = pl.BlockSpec((bm, HEAD_DIM), lambda m: (m % tps, 0))
    q, k, v = pl.pallas_call(
        functools.partial(_qkv_kernel, n_heads=seg // HEAD_DIM, q_scale=q_scale),
        grid=(M // bm,),
        in_specs=[rows(D), norm, half(0), table, table],
        out_specs=[rows(seg), rows(seg), rows(seg)],
        out_shape=[
            jax.ShapeDtypeStruct((M, seg), BF16),
            jax.ShapeDtypeStruct((M, seg), F32),
            jax.ShapeDtypeStruct((M, seg), F32),
        ],
        compiler_params=params,
        name="inproj_qkv",
    )(x, nw, w_in, cos, sin)
    state = pl.BlockSpec((None, hs, seg), lambda m: (m // tps, 0, 0))
    yc, cst = pl.pallas_call(
        functools.partial(_gconv_kernel, bm=bm, tps=tps, stride=stride),
        grid=(M // bm,),
        in_specs=[rows(D), norm, half(1), pl.BlockSpec((CONV_K, seg), lambda m: (0, 0)), state],
        out_specs=[rows(seg), state],
        out_shape=[
            jax.ShapeDtypeStruct((M, seg), BF16),
            jax.ShapeDtypeStruct((n_seq, hs, seg), F32),
        ],
        scratch_shapes=[pltpu.VMEM((base + bm, seg), F32)],
        compiler_params=params,
        name="inproj_gconv",
    )(x, nw, w_in, conv_w, hist)
    return q, k, v, yc, cst


def _group_reduce(x, op):
    return op(x.reshape(x.shape[0] // SUBLANES, SUBLANES, x.shape[1]), axis=0)


def _moba_kernel(q_ref, k_ref, v_ref, o_ref, kb_scr, vt_scr, ks_scr, ksh_scr, ksl_scr,
                 qt_scr, s_scr, *, nq):
    i = pl.program_id(2)
    blk = MOBA_BLOCK

    @pl.when(i == 0)
    def _():
        for j in range(nq):
            kf = k_ref[j * blk:(j + 1) * blk, :]
            kb_scr[j] = kf.astype(BF16)
            vt_scr[j] = v_ref[j * blk:(j + 1) * blk, :].T.astype(BF16)
            ks_scr[j:j + 1, :] = jnp.sum(kf, axis=0, keepdims=True)
        ks = ks_scr[...]
        hi = ks.astype(BF16)
        ksh_scr[...] = hi
        ksl_scr[...] = (ks - hi.astype(F32)).astype(BF16)

    for pair in range(nq // 2):
        @pl.when(i == pair)
        def _(pair=pair):
            _moba_pair(pair, q_ref, o_ref, kb_scr, vt_scr, ksh_scr, ksl_scr, qt_scr, s_scr, nq=nq)


def _moba_pair(pair, q_ref, o_ref, kb_scr, vt_scr, ksh_scr, ksl_scr, qt_scr, s_scr, *, nq):
    blk = MOBA_BLOCK
    owns = (pair, nq - 1 - pair)
    blk_id = lax.broadcasted_iota(jnp.int32, (nq, blk), 0)
    sels = []
    for w, own in enumerate(owns):
        qt = q_ref[own * blk:(own + 1) * blk, :].astype(F32).T.astype(BF16)
        qt_scr[w] = qt
        gate = (jnp.dot(ksh_scr[...], qt, preferred_element_type=F32)
                + jnp.dot(ksl_scr[...], qt, preferred_element_type=F32))
        g = jnp.where(blk_id < own, gate, -jnp.inf)
        sel = jnp.zeros((nq, blk), jnp.bool_)
        for _ in range(min(MOBA_TOPK, own)):
            top = jnp.max(g, axis=0, keepdims=True)
            first = jnp.min(jnp.where(g == top, blk_id, nq), axis=0, keepdims=True)
            hit = blk_id == first
            sel = sel | hit
            g = jnp.where(hit, -jnp.inf, g)
        sels.append(jnp.where(sel, 1.0, 0.0))

    tiles = [(w, j) for w, own in enumerate(owns) for j in range(own)]
    n_gen = len(tiles)

    mx = [None, None]
    for t, (w, j) in enumerate(tiles):
        s = jnp.dot(kb_scr[j], qt_scr[w], preferred_element_type=F32)
        s = jnp.where(sels[w][j:j + 1, :] > 0.0, s, NEG_INF)
        s_scr[t] = s
        cm = _group_reduce(s, jnp.max)
        mx[w] = cm if mx[w] is None else jnp.maximum(mx[w], cm)
    key_id = lax.broadcasted_iota(jnp.int32, (blk, blk), 0)
    qry_id = lax.broadcasted_iota(jnp.int32, (blk, blk), 1)
    for w, own in enumerate(owns):
        s = jnp.dot(kb_scr[own], qt_scr[w], preferred_element_type=F32)
        s = jnp.where(key_id <= qry_id, s, NEG_INF)
        s_scr[n_gen + w] = s
        cm = _group_reduce(s, jnp.max)
        mx[w] = cm if mx[w] is None else jnp.maximum(mx[w], cm)
    mrow = [jnp.max(m, axis=0, keepdims=True) for m in mx]

    acc = [None, None]
    den = [None, None]
    order = list(enumerate(tiles)) + [(n_gen + w, (w, own)) for w, own in enumerate(owns)]
    for t, (w, j) in order:
        p = jnp.exp2(s_scr[t] - mrow[w])
        cs = _group_reduce(p, jnp.sum)
        den[w] = cs if den[w] is None else den[w] + cs
        r = jnp.dot(vt_scr[j], p.astype(BF16), preferred_element_type=F32)
        acc[w] = r if acc[w] is None else acc[w] + r
    for w, own in enumerate(owns):
        tot = jnp.sum(den[w], axis=0, keepdims=True)
        o_ref[own * blk:(own + 1) * blk, :] = (acc[w] / tot).T.astype(BF16)


def _moba_prompt(q, k, v, *, n_batch, seq, n_heads):
    blk = MOBA_BLOCK
    assert seq % (2 * blk) == 0
    nq = seq // blk
    width = n_heads * HEAD_DIM
    col = pl.BlockSpec((None, seq, HEAD_DIM), lambda b, h, i: (b, 0, h))
    out = pl.pallas_call(
        functools.partial(_moba_kernel, nq=nq),
        grid=(n_batch, n_heads, nq // 2),
        in_specs=[col, col, col],
        out_specs=col,
        out_shape=jax.ShapeDtypeStruct((n_batch, seq, width), BF16),
        scratch_shapes=[
            pltpu.VMEM((nq, blk, HEAD_DIM), BF16),
            pltpu.VMEM((nq, HEAD_DIM, blk), BF16),
            pltpu.VMEM((nq, HEAD_DIM), F32),
            pltpu.VMEM((nq, HEAD_DIM), BF16),
            pltpu.VMEM((nq, HEAD_DIM), BF16),
            pltpu.VMEM((2, HEAD_DIM, blk), BF16),
            pltpu.VMEM((nq + 1, blk, blk), F32),
        ],
        compiler_params=pltpu.CompilerParams(
            dimension_semantics=("arbitrary", "arbitrary", "arbitrary"),
            vmem_limit_bytes=VMEM_LIMIT),
        name="moba_prompt",
    )(q.reshape(n_batch, seq, width), k.reshape(n_batch, seq, width),
      v.reshape(n_batch, seq, width))
    return out.reshape(n_batch * seq, width)


def _outproj_kernel(a_ref, c_ref, wt_ref, wb_ref, x_ref, o_ref):
    o_ref[...] = (x_ref[...]
                  + jnp.dot(a_ref[...], wt_ref[...], preferred_element_type=F32)
                  + jnp.dot(c_ref[...], wb_ref[...], preferred_element_type=F32))


def _outproj(a, c, w_o, x, *, layer, bm, bn):
    M, D = x.shape
    half = a.shape[1]
    assert w_o.shape[1:] == (2 * half, D) and c.shape == a.shape and D % bn == 0 and M % bm == 0
    w_mode = pl.Buffered(1) if bn == D else None
    return pl.pallas_call(
        _outproj_kernel,
        grid=(M // bm, D // bn),
        in_specs=[
            pl.BlockSpec((bm, half), lambda m, n: (m, 0)),
            pl.BlockSpec((bm, half), lambda m, n: (m, 0)),
            pl.BlockSpec((None, half, bn), lambda m, n: (layer, 0, n), pipeline_mode=w_mode),
            pl.BlockSpec((None, half, bn), lambda m, n: (layer, 1, n), pipeline_mode=w_mode),
            pl.BlockSpec((bm, bn), lambda m, n: (m, n)),
        ],
        out_specs=pl.BlockSpec((bm, bn), lambda m, n: (m, n)),
        out_shape=jax.ShapeDtypeStruct((M, D), F32),
        compiler_params=pltpu.CompilerParams(
            dimension_semantics=("arbitrary", "arbitrary"), vmem_limit_bytes=VMEM_LIMIT),
        name="outproj",
    )(a, c, w_o, w_o, x)


def _ffn_kernel(x_ref, nw_ref, wg_ref, wu_ref, wd_ref, fw_ref, o_ref, h_scr, *, final_norm):
    f = pl.program_id(1)

    @pl.when(f == 0)
    def _():
        x = x_ref[...]
        h_scr[...] = _rmsnorm(x, nw_ref[...]).astype(BF16)
        o_ref[...] = x

    h = h_scr[...]
    g = jnp.dot(h, wg_ref[...], preferred_element_type=F32)
    u = jnp.dot(h, wu_ref[...], preferred_element_type=F32)
    a = (g * jax.nn.sigmoid(g) * u).astype(BF16)
    o_ref[...] += jnp.dot(a, wd_ref[...], preferred_element_type=F32)

    if final_norm:
        @pl.when(f == pl.num_programs(1) - 1)
        def _():
            o_ref[...] = _rmsnorm(o_ref[...], fw_ref[...])


def _ffn(x, nw, wg, wu, wd, fw, *, layer, bm, bf, final_norm):
    M, D = x.shape
    FF = wg.shape[2]
    assert FF % bf == 0 and M % bm == 0
    kern = functools.partial(_ffn_kernel, final_norm=final_norm)
    return pl.pallas_call(
        kern,
        grid=(M // bm, FF // bf),
        in_specs=[
            pl.BlockSpec((bm, D), lambda m, f: (m, 0)),
            pl.BlockSpec((1, D), lambda m, f: (0, 0)),
            pl.BlockSpec((None, D, bf), lambda m, f: (layer, 0, f)),
            pl.BlockSpec((None, D, bf), lambda m, f: (layer, 0, f)),
            pl.BlockSpec((None, bf, D), lambda m, f: (layer, f, 0)),
            pl.BlockSpec((1, D), lambda m, f: (0, 0)),
        ],
        out_specs=pl.BlockSpec((bm, D), lambda m, f: (m, 0)),
        out_shape=jax.ShapeDtypeStruct((M, D), F32),
        scratch_shapes=[pltpu.VMEM((bm, D), BF16)],
        compiler_params=pltpu.CompilerParams(
            dimension_semantics=("arbitrary", "arbitrary"), vmem_limit_bytes=VMEM_LIMIT),
        name="ffn",
    )(x, nw, wg, wu, wd, fw)


def _pool_kernel(x_ref, nw_ref, wp_ref, ps_ref, hist_ref, o_ref, pst_ref, ext_scr, band_scr,
                 *, bm, tps, stride, pos_base):
    m = pl.program_id(0)
    hs = POOL_HIST * stride
    base = _halo_base(hs)
    rb = band_scr.shape[1]
    nd = min(rb, base)
    assert bm % rb == 0
    assert nd == rb or (stride == 1 and nd % SUBLANES == 0 and nd >= max(POOL_WINDOWS))
    grp = wp_ref.shape[1]
    first = (m % tps) == 0

    if nd < rb:
        @pl.when(m == 0)
        def _():
            lag = (lax.broadcasted_iota(jnp.int32, (rb, rb), 0)
                   - lax.broadcasted_iota(jnp.int32, (rb, rb), 1))
            for gi, w in enumerate(POOL_WINDOWS):
                band_scr[gi] = jnp.where((lag >= 0) & (lag < w), 1.0, 0.0).astype(BF16)

    @pl.when(first)
    def _():
        ext_scr[base - hs:base, :] = hist_ref[...]

    @pl.when(jnp.logical_not(first))
    def _():
        ext_scr[base - hs:base, :] = ext_scr[base + bm - hs:base + bm, :]

    ext_scr[base:base + bm, :] = _rmsnorm(x_ref[...], nw_ref[...])
    for r0 in range(0, bm, rb):
        r = (m % tps) * bm + r0 + lax.broadcasted_iota(jnp.int32, (nd, 1), 0)
        pos = pos_base + (r if stride == 1 else lax.shift_right_logical(r, stride.bit_length() - 1))
        for gi, w in enumerate(POOL_WINDOWS):
            cols = slice(gi * grp, (gi + 1) * grp)
            hb = ext_scr[base + r0:base + r0 + rb, cols]
            tot = hb[:nd]
            for kk in range(1, w):
                off = base + r0 - kk * stride
                tot = tot + ext_scr[off:off + nd, cols]
            d = tot / jnp.minimum(pos + 1, w).astype(F32) - hb[:nd]
            if nd < rb:
                hi = hb.astype(BF16)
                lo = (hb - hi.astype(F32)).astype(BF16)
                band = band_scr[gi]
                full = (jnp.dot(band, hi, preferred_element_type=F32)
                        + jnp.dot(band, lo, preferred_element_type=F32))
                d = jnp.concatenate([d, full[nd:] * (1.0 / w) - hb[nd:]], axis=0)
            out = jnp.dot(d.astype(BF16), wp_ref[gi], preferred_element_type=F32)
            o_ref[r0:r0 + rb, cols] = x_ref[r0:r0 + rb, cols] + out * ps_ref[:, cols]
    pst_ref[...] = ext_scr[base + bm - hs:base + bm, :]


def _pool(x, nw, w_pool, pool_scale, hist, *, layer, bm, tps, stride, pos_base):
    M, D = x.shape
    n_seq = hist.shape[0]
    hs = POOL_HIST * stride
    assert stride & (stride - 1) == 0 and hist.shape == (n_seq, hs, D) and M % bm == 0
    base = _halo_base(hs)
    _, ng, grp, _ = w_pool.shape
    assert ng == len(POOL_WINDOWS) and ng * grp == D
    kern = functools.partial(_pool_kernel, bm=bm, tps=tps, stride=stride, pos_base=pos_base)
    return pl.pallas_call(
        kern,
        grid=(M // bm,),
        in_specs=[
            pl.BlockSpec((bm, D), lambda m: (m, 0)),
            pl.BlockSpec((1, D), lambda m: (0, 0)),
            pl.BlockSpec((None, ng, grp, grp), lambda m: (layer, 0, 0, 0)),
            pl.BlockSpec((1, D), lambda m: (0, 0)),
            pl.BlockSpec((None, hs, D), lambda m: (m // tps, 0, 0)),
        ],
        out_specs=[
            pl.BlockSpec((bm, D), lambda m: (m, 0)),
            pl.BlockSpec((None, hs, D), lambda m: (m // tps, 0, 0)),
        ],
        out_shape=[
            jax.ShapeDtypeStruct((M, D), F32),
            jax.ShapeDtypeStruct((n_seq, hs, D), F32),
        ],
        scratch_shapes=[pltpu.VMEM((base + bm, D), F32),
                        pltpu.VMEM((ng, min(bm, MXU_DIM), min(bm, MXU_DIM)), BF16)],
        compiler_params=pltpu.CompilerParams(
            dimension_semantics=("arbitrary",), vmem_limit_bytes=VMEM_LIMIT),
        name="pool",
    )(x, nw, w_pool, pool_scale, hist)


def _head_match(n_rows, n_cols, n_heads, t_pad):
    rh = lax.broadcasted_iota(jnp.int32, (n_rows, n_cols), 0) // t_pad
    ch = lax.broadcasted_iota(jnp.int32, (n_rows, n_cols), 1) % n_heads
    return rh == ch


def _fold_heads(x, n_heads, t_pad):
    out = x[0:t_pad]
    for hh in range(1, n_heads):
        out = out + x[hh * t_pad:(hh + 1) * t_pad]
    return out


def _lane_class(x, n_heads, op):
    sh = n_heads
    while sh < x.shape[-1]:
        x = op(x, pltpu.roll(x, sh, axis=x.ndim - 1))
        sh *= 2
    return x


def _lane_tiles(x, op):
    out = x[:, 0:128]
    for t in range(1, x.shape[1] // 128):
        out = op(out, x[:, t * 128:(t + 1) * 128])
    return out


def _sa_logits_kernel(pt_ref, q_ref, *refs, n_pages, n_heads, t_pad, pages_per_block):
    k_refs = refs[:n_pages]
    r_ref, g_ref = refs[n_pages:]
    qa = q_ref[...]
    rows = k_refs[0].shape[0] * n_heads
    match = _head_match(qa.shape[0], rows, n_heads, t_pad)
    for pg in range(n_pages):
        k2 = k_refs[pg][...].reshape(rows, HEAD_DIM).astype(BF16)
        l2 = lax.dot_general(qa, k2, _NT, preferred_element_type=F32)
        r_ref[pg] = _fold_heads(jnp.where(match, l2, 0.0), n_heads, t_pad)
    for bb in range(n_pages // pages_per_block):
        tot = _lane_tiles(r_ref[bb * pages_per_block], jnp.add)
        for pg in range(1, pages_per_block):
            tot = tot + _lane_tiles(r_ref[bb * pages_per_block + pg], jnp.add)
        g_ref[bb] = _lane_class(tot, n_heads, jnp.add)


def _sa_values_kernel(pt_ref, r_ref, g_ref, q_ref, kn_ref, vn_ref, *refs,
                      n_pages, n_heads, t_pad, pages_per_block):
    v_refs = refs[:n_pages]
    o_ref, p_scr, sel_scr, l_scr, acc_scr = refs[n_pages:]
    c = pl.program_id(1)
    n_blocks = g_ref.shape[0]
    rows = v_refs[0].shape[0] * n_heads
    n_rows = q_ref.shape[0]
    match = _head_match(n_rows, rows, n_heads, t_pad)

    def spread(p):
        return jnp.where(match[:, :p.shape[1]], jnp.concatenate([p] * n_heads, axis=0), 0.0).astype(BF16)

    @pl.when(c == 0)
    def _():
        gates = g_ref[...]
        idx = lax.broadcasted_iota(jnp.int32, gates.shape, 0)
        sel = jnp.zeros(gates.shape, F32)
        for _ in range(MOBA_TOPK):
            mx = jnp.max(gates, axis=0, keepdims=True)
            first = jnp.min(jnp.where(gates == mx, idx, n_blocks), axis=0, keepdims=True)
            hit = idx == first
            sel = jnp.where(hit, 1.0, sel)
            gates = jnp.where(hit, -jnp.inf, gates)
        sel_scr[...] = sel

        own_w = kn_ref.shape[0]
        lo = lax.dot_general(q_ref[...], kn_ref[...].astype(BF16), _NT, preferred_element_type=F32)
        ro = _fold_heads(jnp.where(match[:, :own_w], lo, 0.0), n_heads, t_pad)
        t_key = lax.broadcasted_iota(jnp.int32, ro.shape, 1) // n_heads
        t_qry = lax.broadcasted_iota(jnp.int32, ro.shape, 0)
        ro = jnp.where(t_key <= t_qry, ro, NEG_INF)

        def max_body(b, mx):
            here = _lane_tiles(r_ref[b * pages_per_block], jnp.maximum)
            for pg in range(1, pages_per_block):
                here = jnp.maximum(here, _lane_tiles(r_ref[b * pages_per_block + pg], jnp.maximum))
            return jnp.maximum(mx, jnp.where(sel_scr[b] > 0.0, here, NEG_INF))

        mx = lax.fori_loop(0, n_blocks, max_body, ro)
        mx = _lane_class(mx, n_heads, jnp.maximum)
        width = r_ref.shape[2]
        mx_w = jnp.concatenate([mx] * (width // 128), axis=1)

        def exp_body(b, tot):
            keep = jnp.concatenate([sel_scr[b]] * (width // 128), axis=1) > 0.0
            for pg in range(pages_per_block):
                p = jnp.where(keep, jnp.exp(r_ref[b * pages_per_block + pg] - mx_w), 0.0)
                p_scr[b * pages_per_block + pg] = p
                tot = tot + _lane_tiles(p, jnp.add)
            return tot

        po = jnp.exp(ro - mx)
        tot = lax.fori_loop(0, n_blocks, exp_body, po)
        l_scr[...] = _lane_class(tot, n_heads, jnp.add)
        acc_scr[...] = jnp.dot(spread(po), vn_ref[...].astype(BF16), preferred_element_type=F32)

    acc = acc_scr[...]
    for pg in range(n_pages):
        v2 = v_refs[pg][...].reshape(rows, HEAD_DIM).astype(BF16)
        acc = acc + jnp.dot(spread(p_scr[c * n_pages + pg]), v2, preferred_element_type=F32)
    acc_scr[...] = acc

    @pl.when(c == pl.num_programs(1) - 1)
    def _():
        l = l_scr[...]
        den = jnp.concatenate([l[:, hh:hh + 1] for hh in range(n_heads)], axis=0)
        o_ref[...] = acc_scr[...] / den


def _sample_attention(q_all, k_new, v_new, cache_k, cache_v, page_table, *, layer, n_heads, t_pad):
    n_dec, n_pt = page_table.shape
    page = cache_k.shape[2]
    ppb = MOBA_BLOCK // page
    P = PAGES_PER_STEP
    assert MOBA_BLOCK % page == 0 and n_pt % P == 0 and P % ppb == 0
    n_steps = n_pt // P
    n_blocks = n_pt // ppb
    width = page * n_heads
    n_rows = n_heads * t_pad

    def page_spec(i):
        return pl.BlockSpec((None, None, page, n_heads, HEAD_DIM),
                            lambda b, c, pt: (layer, pt[b, c * P + i], 0, 0, 0))

    params = pltpu.CompilerParams(dimension_semantics=("arbitrary", "arbitrary"),
                                  vmem_limit_bytes=VMEM_LIMIT)
    common = dict(n_pages=P, n_heads=n_heads, t_pad=t_pad, pages_per_block=ppb)
    logits, gates = pl.pallas_call(
        functools.partial(_sa_logits_kernel, **common),
        grid_spec=pltpu.PrefetchScalarGridSpec(
            num_scalar_prefetch=1,
            grid=(n_dec, n_steps),
            in_specs=[pl.BlockSpec((None, n_rows, HEAD_DIM), lambda b, c, pt: (b, 0, 0))]
                     + [page_spec(i) for i in range(P)],
            out_specs=[
                pl.BlockSpec((None, P, t_pad, width), lambda b, c, pt: (b, c, 0, 0)),
                pl.BlockSpec((None, P // ppb, t_pad, 128), lambda b, c, pt: (b, c, 0, 0)),
            ],
        ),
        out_shape=[
            jax.ShapeDtypeStruct((n_dec, n_pt, t_pad, width), F32),
            jax.ShapeDtypeStruct((n_dec, n_blocks, t_pad, 128), F32),
        ],
        compiler_params=params,
        name="sample_logits",
    )(page_table, q_all, *([cache_k] * P))

    own = k_new.shape[1]
    return pl.pallas_call(
        functools.partial(_sa_values_kernel, **common),
        grid_spec=pltpu.PrefetchScalarGridSpec(
            num_scalar_prefetch=1,
            grid=(n_dec, n_steps),
            in_specs=[
                pl.BlockSpec((None, n_pt, t_pad, width), lambda b, c, pt: (b, 0, 0, 0)),
                pl.BlockSpec((None, n_blocks, t_pad, 128), lambda b, c, pt: (b, 0, 0, 0)),
                pl.BlockSpec((None, n_rows, HEAD_DIM), lambda b, c, pt: (b, 0, 0)),
                pl.BlockSpec((None, own, HEAD_DIM), lambda b, c, pt: (b, 0, 0)),
                pl.BlockSpec((None, own, HEAD_DIM), lambda b, c, pt: (b, 0, 0)),
            ] + [page_spec(i) for i in range(P)],
            out_specs=pl.BlockSpec((None, n_rows, HEAD_DIM), lambda b, c, pt: (b, 0, 0)),
            scratch_shapes=[
                pltpu.VMEM((n_pt, t_pad, width), F32),
                pltpu.VMEM((n_blocks, t_pad, 128), F32),
                pltpu.VMEM((t_pad, 128), F32),
                pltpu.VMEM((n_rows, HEAD_DIM), F32),
            ],
        ),
        out_shape=jax.ShapeDtypeStruct((n_dec, n_rows, HEAD_DIM), F32),
        compiler_params=params,
        name="sample_values",
    )(page_table, logits, gates, q_all, k_new, v_new, *([cache_v] * P))


def _rope_tables(pos):
    half = HEAD_DIM // 2
    inv = ROPE_THETA ** (-jnp.arange(half, dtype=F32) / half)
    ang = pos.astype(F32)[:, None] * inv[None, :]
    cos, sin = jnp.cos(ang), jnp.sin(ang)
    return jnp.concatenate([cos, cos], axis=1), jnp.concatenate([-sin, sin], axis=1)


def kernel(x_prompt, x_sample, cache_k, cache_v, page_table, state_conv, state_pool,
           norm_mix, norm_ffn, norm_final, w_in, conv_w, w_o, w_pool, pool_scale,
           w_gate, w_up, w_down):
    n_batch, n_seq, d = x_prompt.shape
    n_dec, n_new, _ = x_sample.shape
    depth = norm_mix.shape[0]
    seg = conv_w.shape[2]
    n_heads = seg // HEAD_DIM
    page = cache_k.shape[2]
    past_len = page_table.shape[1] * page
    assert n_dec == SUBLANES and n_new <= SUBLANES and past_len % MOBA_BLOCK == 0
    assert cache_k.shape[3] == n_heads and cache_k.shape[4] == HEAD_DIM
    q_scale = HEAD_DIM ** -0.5
    q_scale_log2 = q_scale * LOG2_E
    t_pad = SUBLANES
    own_keys = 128 // n_heads
    assert n_new <= own_keys

    w_in_b, w_o_b, w_pool_b = w_in.astype(BF16), w_o.astype(BF16), w_pool.astype(BF16)
    w_gate_b, w_up_b, w_down_b = w_gate.astype(BF16), w_up.astype(BF16), w_down.astype(BF16)
    row = lambda a: a.reshape(1, -1)

    xp = x_prompt.reshape(n_batch * n_seq, d)
    xs = x_sample.transpose(1, 0, 2).reshape(n_new * n_dec, d)
    ms = n_new * n_dec
    bm = 512
    tps = n_seq // bm
    cos_p, sin_p = _rope_tables(jnp.arange(n_seq, dtype=jnp.int32))
    pos_s = past_len + jnp.repeat(jnp.arange(n_new, dtype=jnp.int32), n_dec)
    cos_s, sin_s = _rope_tables(pos_s)

    def tm(a):
        return a.transpose(1, 0, 2).reshape(1, a.shape[1] * n_dec, a.shape[2])

    def untm(a, rows):
        return a.reshape(rows, n_dec, a.shape[-1]).transpose(1, 0, 2)

    k_p, v_p, k_s, v_s, conv_p, conv_s, pool_p, pool_s = [], [], [], [], [], [], [], []
    for layer in range(depth):
        if layer % 2 == 0:
            e = layer // 2
            q, k, v, yc, cst = _inproj(
                xp, row(norm_mix[layer]), w_in_b, cos_p, sin_p, conv_w[e],
                jnp.zeros((n_batch, CONV_HIST, seg), F32),
                layer=e, bm=bm, tps=tps, stride=1, q_scale=q_scale_log2)
            attn = _moba_prompt(q, k, v, n_batch=n_batch, seq=n_seq, n_heads=n_heads)
            xp = _outproj(attn, yc, w_o_b, xp, layer=e, bm=bm, bn=d)
            k_p.append(k.reshape(n_batch, n_seq, n_heads, HEAD_DIM))
            v_p.append(v.reshape(n_batch, n_seq, n_heads, HEAD_DIM))
            conv_p.append(cst)
            q, k, v, yc, cst = _inproj(
                xs, row(norm_mix[layer]), w_in_b, cos_s, sin_s, conv_w[e], tm(state_conv[e]),
                layer=e, bm=ms, tps=1, stride=n_dec, q_scale=q_scale)
            heads = lambda a: a.reshape(n_new, n_dec, n_heads, HEAD_DIM)
            q_all = jnp.pad(heads(q).transpose(1, 2, 0, 3), ((0, 0), (0, 0), (0, t_pad - n_new), (0, 0)))
            q_all = q_all.reshape(n_dec, n_heads * t_pad, HEAD_DIM)
            k_new = heads(k).transpose(1, 0, 2, 3)
            v_new = heads(v).transpose(1, 0, 2, 3)
            own = lambda a: jnp.pad(a, ((0, 0), (0, own_keys - n_new), (0, 0), (0, 0))).reshape(
                n_dec, own_keys * n_heads, HEAD_DIM)
            o = _sample_attention(q_all, own(k_new), own(v_new), cache_k, cache_v, page_table,
                                  layer=e, n_heads=n_heads, t_pad=t_pad)
            o = o.reshape(n_dec, n_heads, t_pad, HEAD_DIM)[:, :, :n_new]
            attn = o.transpose(2, 0, 1, 3).reshape(ms, seg).astype(BF16)
            xs = _outproj(attn, yc, w_o_b, xs, layer=e, bm=ms, bn=d)
            k_s.append(k_new)
            v_s.append(v_new)
            conv_s.append(untm(cst[0], CONV_HIST))
        else:
            o_ = layer // 2
            xp, pst = _pool(xp, row(norm_mix[layer]), w_pool_b, row(pool_scale[o_]),
                            jnp.zeros((n_batch, POOL_HIST, d), F32),
                            layer=o_, bm=bm, tps=tps, stride=1, pos_base=0)
            pool_p.append(pst)
            xs, pst = _pool(xs, row(norm_mix[layer]), w_pool_b, row(pool_scale[o_]),
                            tm(state_pool[o_]), layer=o_, bm=ms, tps=1, stride=n_dec,
                            pos_base=past_len)
            pool_s.append(untm(pst[0], POOL_HIST))
        last = layer == depth - 1
        ffn = functools.partial(_ffn, nw=row(norm_ffn[layer]), wg=w_gate_b, wu=w_up_b, wd=w_down_b,
                                fw=row(norm_final), layer=layer, bf=512, final_norm=last)
        xp = ffn(xp, bm=2 * bm)
        xs = ffn(xs, bm=ms)

    y_prompt = xp.reshape(n_batch, n_seq, d)
    y_sample = xs.reshape(n_new, n_dec, d).transpose(1, 0, 2)
    return (y_prompt, y_sample, jnp.stack(k_p), jnp.stack(v_p), jnp.stack(k_s), jnp.stack(v_s),
            jnp.stack(conv_p), jnp.stack(conv_s), jnp.stack(pool_p), jnp.stack(pool_s))
```

```python
import functools

import jax
import jax.numpy as jnp
from jax import lax
from jax.experimental import pallas as pl
from jax.experimental.pallas import tpu as pltpu

F32 = jnp.float32
BF16 = jnp.bfloat16

HEAD_DIM = 128
MOBA_BLOCK = 256
MOBA_TOPK = 3
CONV_K = 3
CONV_HIST = CONV_K - 1
ROPE_THETA = 10000.0
POOL_WINDOWS = (2, 4, 8, 16)
POOL_HIST = max(POOL_WINDOWS) - 1
RMS_EPS = 1e-6
NEG_INF = -1e30
LOG2_E = 1.4426950408889634

SUBLANES = 8
MXU_DIM = 256
VMEM_LIMIT = 56 * 1024 * 1024
PAGES_PER_STEP = 32

_NT = (((1,), (1,)), ((), ()))


def _rmsnorm(x, g):
    ms = jnp.mean(x * x, axis=-1, keepdims=True)
    return x * lax.rsqrt(ms + RMS_EPS) * g


def _halo_base(rows):
    return -(-rows // SUBLANES) * SUBLANES


class _CastPlan:
    def __init__(self, stacked, layer, n_steps):
        _, rows, cols = stacked.shape
        n_col = cols // 128
        assert cols % 128 == 0 and rows % 16 == 0 and n_col <= n_steps
        self.parts = max(d for d in range(1, rows // 16 + 1)
                         if (rows // 16) % d == 0 and d * n_col <= n_steps)
        self.block = (rows // self.parts, 128)
        self.count = self.parts * n_col
        self.layer = layer
        self.shape = (rows, cols)

    def _index(self, step):
        c = jnp.minimum(step, self.count - 1)
        return c % self.parts, c // self.parts

    def specs(self, step_of, src_mode=None):
        def src(*g):
            return (self.layer,) + self._index(step_of(*g))
        def dst(*g):
            return self._index(step_of(*g))
        return (pl.BlockSpec((None,) + self.block, src, pipeline_mode=src_mode),
                pl.BlockSpec(self.block, dst))


def _cast_step(src_refs, dst_refs):
    for src, dst in zip(src_refs, dst_refs):
        dst[...] = src[...].astype(BF16)


def _qkv_kernel(x_ref, nw_ref, w_ref, cos_ref, sin_ref, q_ref, k_ref, v_ref, *, n_heads, q_scale):
    seg = n_heads * HEAD_DIM
    h = _rmsnorm(x_ref[...], nw_ref[...]).astype(BF16)
    cos = cos_ref[...]
    sin = sin_ref[...]

    def rope(a, hh):
        part = a[:, hh * HEAD_DIM:(hh + 1) * HEAD_DIM]
        return part * cos + pltpu.roll(part, HEAD_DIM // 2, axis=1) * sin

    aq = jnp.dot(h, w_ref[:, 0:seg], preferred_element_type=F32)
    for hh in range(n_heads):
        q_ref[:, hh * HEAD_DIM:(hh + 1) * HEAD_DIM] = (rope(aq, hh) * q_scale).astype(BF16)
    ak = jnp.dot(h, w_ref[:, seg:2 * seg], preferred_element_type=F32)
    for hh in range(n_heads):
        k_ref[:, hh * HEAD_DIM:(hh + 1) * HEAD_DIM] = rope(ak, hh)
    v_ref[...] = jnp.dot(h, w_ref[:, 2 * seg:3 * seg], preferred_element_type=F32)


def _gconv_kernel(x_ref, nw_ref, w_ref, cw_ref, hist_ref, yc_ref, cst_ref, ue_scr,
                  *, bm, tps, stride):
    m = pl.program_id(0)
    seg = yc_ref.shape[1]
    hs = CONV_HIST * stride
    base = _halo_base(hs)
    h = _rmsnorm(x_ref[...], nw_ref[...]).astype(BF16)
    u = (jnp.dot(h, w_ref[:, seg:2 * seg], preferred_element_type=F32)
         * jnp.dot(h, w_ref[:, 2 * seg:3 * seg], preferred_element_type=F32))
    first = (m % tps) == 0

    @pl.when(first)
    def _():
        ue_scr[base - hs:base, :] = hist_ref[...]

    @pl.when(jnp.logical_not(first))
    def _():
        ue_scr[base - hs:base, :] = ue_scr[base + bm - hs:base + bm, :]

    ue_scr[base:base + bm, :] = u
    cw = cw_ref[...]
    conv = u * cw[CONV_K - 1:CONV_K, :]
    for j in range(CONV_K - 1):
        off = base - (CONV_K - 1 - j) * stride
        conv = conv + ue_scr[off:off + bm, :] * cw[j:j + 1, :]
    gate_b = jnp.dot(h, w_ref[:, 0:seg], preferred_element_type=F32)
    yc_ref[...] = (gate_b * conv).astype(BF16)
    cst_ref[...] = ue_scr[base + bm - hs:base + bm, :]


def _inproj(x, nw, w_in, cos, sin, conv_w, hist, *, layer, bm, tps, stride, q_scale):
    M, D = x.shape
    seg = conv_w.shape[1]
    assert w_in.shape[2] == 6 * seg and M % bm == 0 and (M // bm) % tps == 0
    n_seq = hist.shape[0]
    hs = CONV_HIST * stride
    assert hist.shape == (n_seq, hs, seg) and bm >= hs
    base = _halo_base(hs)
    params = pltpu.CompilerParams(dimension_semantics=("arbitrary",), vmem_limit_bytes=VMEM_LIMIT)
    rows = lambda width: pl.BlockSpec((bm, width), lambda m: (m, 0))
    half = lambda n: pl.BlockSpec((None, D, 3 * seg), lambda m: (layer, 0, n),
                                  pipeline_mode=pl.Buffered(1))
    norm = pl.BlockSpec((1, D), lambda m: (0, 0))
    table = pl.BlockSpec((bm, HEAD_DIM), lambda m: (m % tps, 0))
    q, k, v = pl.pallas_call(
        functools.partial(_qkv_kernel, n_heads=seg // HEAD_DIM, q_scale=q_scale),
        grid=(M // bm,),
        in_specs=[rows(D), norm, half(0), table, table],
        out_specs=[rows(seg), rows(seg), rows(seg)],
        out_shape=[
            jax.ShapeDtypeStruct((M, seg), BF16),
            jax.ShapeDtypeStruct((M, seg), F32),
            jax.ShapeDtypeStruct((M, seg), F32),
        ],
        compiler_params=params,
        name="inproj_qkv",
    )(x, nw, w_in, cos, sin)
    state = pl.BlockSpec((None, hs, seg), lambda m: (m // tps, 0, 0))
    yc, cst = pl.pallas_call(
        functools.partial(_gconv_kernel, bm=bm, tps=tps, stride=stride),
        grid=(M // bm,),
        in_specs=[rows(D), norm, half(1), pl.BlockSpec((CONV_K, seg), lambda m: (0, 0)), state],
        out_specs=[rows(seg), state],
        out_shape=[
            jax.ShapeDtypeStruct((M, seg), BF16),
            jax.ShapeDtypeStruct((n_seq, hs, seg), F32),
        ],
        scratch_shapes=[pltpu.VMEM((base + bm, seg), F32)],
        compiler_params=params,
        name="inproj_gconv",
    )(x, nw, w_in, conv_w, hist)
    return q, k, v, yc, cst


def _group_reduce(x, op):
    return op(x.reshape(x.shape[0] // SUBLANES, SUBLANES, x.shape[1]), axis=0)


def _moba_kernel(*refs, nq, plans):
    n = len(plans)
    q_ref, k_ref, v_ref = refs[:3]
    o_ref = refs[3 + n]
    kb_scr, vt_scr, ks_scr, qt_scr, sel_scr, s_scr = refs[4 + 2 * n:]
    _cast_step(refs[3:3 + n], refs[4 + n:4 + 2 * n])
    _moba_body(q_ref, k_ref, v_ref, o_ref, kb_scr, vt_scr, ks_scr, qt_scr, sel_scr, s_scr, nq=nq)


def _moba_body(q_ref, k_ref, v_ref, o_ref, kb_scr, vt_scr, ks_scr, qt_scr, sel_scr, s_scr,
               *, nq):
    i = pl.program_id(2)
    blk = MOBA_BLOCK

    @pl.when(i == 0)
    def _():
        for j in range(nq):
            kf = k_ref[j * blk:(j + 1) * blk, :]
            kb_scr[j] = kf.astype(BF16)
            vt_scr[j] = v_ref[j * blk:(j + 1) * blk, :].T.astype(BF16)
            ks_scr[j:j + 1, :] = jnp.sum(kf, axis=0, keepdims=True)
        ks = ks_scr[...]
        ks_hi = ks.astype(BF16)
        ks_lo = (ks - ks_hi.astype(F32)).astype(BF16)
        blk_id = lax.broadcasted_iota(jnp.int32, (nq, blk), 0)
        for own in range(nq):
            qt = q_ref[own * blk:(own + 1) * blk, :].astype(F32).T.astype(BF16)
            qt_scr[own] = qt
            gate = (jnp.dot(ks_hi, qt, preferred_element_type=F32)
                    + jnp.dot(ks_lo, qt, preferred_element_type=F32))
            g = jnp.where(blk_id < own, gate, -jnp.inf)
            sel = jnp.zeros((nq, blk), jnp.bool_)
            for _ in range(min(MOBA_TOPK, own)):
                top = jnp.max(g, axis=0, keepdims=True)
                first = jnp.min(jnp.where(g == top, blk_id, nq), axis=0, keepdims=True)
                hit = blk_id == first
                sel = sel | hit
                g = jnp.where(hit, -jnp.inf, g)
            sel_scr[own] = jnp.where(sel, 1.0, 0.0)

    for pair in range(nq // 2):
        @pl.when(i == pair)
        def _(pair=pair):
            _moba_pair(pair, o_ref, kb_scr, vt_scr, qt_scr, sel_scr, s_scr, nq=nq)


def _moba_pair(pair, o_ref, kb_scr, vt_scr, qt_scr, sel_scr, s_scr, *, nq):
    blk = MOBA_BLOCK
    owns = (pair, nq - 1 - pair)

    tiles = [(w, j) for w, own in enumerate(owns) for j in range(own)]
    n_gen = len(tiles)

    mx = [None, None]
    for t, (w, j) in enumerate(tiles):
        s = jnp.dot(kb_scr[j], qt_scr[owns[w]], preferred_element_type=F32)
        s = jnp.where(sel_scr[owns[w], j:j + 1, :] > 0.0, s, NEG_INF)
        s_scr[t] = s
        cm = _group_reduce(s, jnp.max)
        mx[w] = cm if mx[w] is None else jnp.maximum(mx[w], cm)
    key_id = lax.broadcasted_iota(jnp.int32, (blk, blk), 0)
    qry_id = lax.broadcasted_iota(jnp.int32, (blk, blk), 1)
    for w, own in enumerate(owns):
        s = jnp.dot(kb_scr[own], qt_scr[own], preferred_element_type=F32)
        s = jnp.where(key_id <= qry_id, s, NEG_INF)
        s_scr[n_gen + w] = s
        cm = _group_reduce(s, jnp.max)
        mx[w] = cm if mx[w] is None else jnp.maximum(mx[w], cm)
    mrow = [jnp.max(m, axis=0, keepdims=True) for m in mx]

    acc = [None, None]
    den = [None, None]
    order = list(enumerate(tiles)) + [(n_gen + w, (w, own)) for w, own in enumerate(owns)]
    for t, (w, j) in order:
        p = jnp.exp2(s_scr[t] - mrow[w])
        cs = _group_reduce(p, jnp.sum)
        den[w] = cs if den[w] is None else den[w] + cs
        r = jnp.dot(vt_scr[j], p.astype(BF16), preferred_element_type=F32)
        acc[w] = r if acc[w] is None else acc[w] + r
    for w, own in enumerate(owns):
        tot = jnp.sum(den[w], axis=0, keepdims=True)
        o_ref[own * blk:(own + 1) * blk, :] = (acc[w] / tot).T.astype(BF16)


def _moba_prompt(q, k, v, *, n_batch, seq, n_heads, round_weights=(), round_layer=0):
    blk = MOBA_BLOCK
    assert seq % (2 * blk) == 0
    nq = seq // blk
    width = n_heads * HEAD_DIM
    grid = (n_batch, n_heads, nq // 2)
    plans = [_CastPlan(w, round_layer, grid[0] * grid[1] * grid[2]) for w in round_weights]
    cast_specs = [p.specs(lambda b, h, i: (b * grid[1] + h) * grid[2] + i) for p in plans]
    col = pl.BlockSpec((None, seq, HEAD_DIM), lambda b, h, i: (b, 0, h))
    out, *rounded = pl.pallas_call(
        functools.partial(_moba_kernel, nq=nq, plans=plans),
        grid=grid,
        in_specs=[col, col, col] + [s for s, _ in cast_specs],
        out_specs=[col] + [d for _, d in cast_specs],
        out_shape=[jax.ShapeDtypeStruct((n_batch, seq, width), BF16)]
                  + [jax.ShapeDtypeStruct(p.shape, BF16) for p in plans],
        scratch_shapes=[
            pltpu.VMEM((nq, blk, HEAD_DIM), BF16),
            pltpu.VMEM((nq, HEAD_DIM, blk), BF16),
            pltpu.VMEM((nq, HEAD_DIM), F32),
            pltpu.VMEM((nq, HEAD_DIM, blk), BF16),
            pltpu.VMEM((nq, nq, blk), F32),
            pltpu.VMEM((nq + 1, blk, blk), F32),
        ],
        compiler_params=pltpu.CompilerParams(
            dimension_semantics=("arbitrary", "arbitrary", "arbitrary"),
            vmem_limit_bytes=VMEM_LIMIT),
        name="moba_prompt",
    )(q.reshape(n_batch, seq, width), k.reshape(n_batch, seq, width),
      v.reshape(n_batch, seq, width), *round_weights)
    return out.reshape(n_batch * seq, width), rounded


def _outproj_kernel(a_ref, c_ref, wt_ref, wb_ref, x_ref, o_ref):
    o_ref[...] = (x_ref[...]
                  + jnp.dot(a_ref[...], wt_ref[...], preferred_element_type=F32)
                  + jnp.dot(c_ref[...], wb_ref[...], preferred_element_type=F32))


def _outproj(a, c, w_o, x, *, layer, bm, bn):
    M, D = x.shape
    half = a.shape[1]
    assert w_o.shape[1:] == (2 * half, D) and c.shape == a.shape and D % bn == 0 and M % bm == 0
    w_mode = pl.Buffered(1) if bn == D else None
    return pl.pallas_call(
        _outproj_kernel,
        grid=(M // bm, D // bn),
        in_specs=[
            pl.BlockSpec((bm, half), lambda m, n: (m, 0)),
            pl.BlockSpec((bm, half), lambda m, n: (m, 0)),
            pl.BlockSpec((None, half, bn), lambda m, n: (layer, 0, n), pipeline_mode=w_mode),
            pl.BlockSpec((None, half, bn), lambda m, n: (layer, 1, n), pipeline_mode=w_mode),
            pl.BlockSpec((bm, bn), lambda m, n: (m, n)),
        ],
        out_specs=pl.BlockSpec((bm, bn), lambda m, n: (m, n)),
        out_shape=jax.ShapeDtypeStruct((M, D), F32),
        compiler_params=pltpu.CompilerParams(
            dimension_semantics=("arbitrary", "arbitrary"), vmem_limit_bytes=VMEM_LIMIT),
        name="outproj",
    )(a, c, w_o, w_o, x)


def _ffn_kernel(*refs, final_norm, plans):
    n = len(plans)
    x_ref, nw_ref, wg_ref, wu_ref, wd_ref, fw_ref = refs[:6]
    o_ref = refs[6 + n]
    h_scr = refs[7 + 2 * n]
    f = pl.program_id(1)

    @pl.when(f == 0)
    def _():
        x = x_ref[...]
        h_scr[...] = _rmsnorm(x, nw_ref[...]).astype(BF16)
        o_ref[...] = x

    _cast_step(refs[6:6 + n], refs[7 + n:7 + 2 * n])
    h = h_scr[...]
    g = jnp.dot(h, wg_ref[...], preferred_element_type=F32)
    u = jnp.dot(h, wu_ref[...], preferred_element_type=F32)
    a = (g * jax.nn.sigmoid(g) * u).astype(BF16)
    o_ref[...] += jnp.dot(a, wd_ref[...], preferred_element_type=F32)

    if final_norm:
        @pl.when(f == pl.num_programs(1) - 1)
        def _():
            o_ref[...] = _rmsnorm(o_ref[...], fw_ref[...])


def _ffn(x, nw, wg, wu, wd, fw, *, bm, bf, final_norm, round_weights=(), round_layer=0):
    M, D = x.shape
    FF = wg.shape[1]
    assert FF % bf == 0 and M % bm == 0
    grid = (M // bm, FF // bf)
    plans = [_CastPlan(w, round_layer, grid[0] * grid[1]) for w in round_weights]
    cast_specs = [p.specs(lambda m, f: m * grid[1] + f, pl.Buffered(1)) for p in plans]
    out, *rounded = pl.pallas_call(
        functools.partial(_ffn_kernel, final_norm=final_norm, plans=plans),
        grid=grid,
        in_specs=[
            pl.BlockSpec((bm, D), lambda m, f: (m, 0)),
            pl.BlockSpec((1, D), lambda m, f: (0, 0)),
            pl.BlockSpec((D, bf), lambda m, f: (0, f)),
            pl.BlockSpec((D, bf), lambda m, f: (0, f)),
            pl.BlockSpec((bf, D), lambda m, f: (f, 0)),
            pl.BlockSpec((1, D), lambda m, f: (0, 0)),
        ] + [s for s, _ in cast_specs],
        out_specs=[pl.BlockSpec((bm, D), lambda m, f: (m, 0))] + [d for _, d in cast_specs],
        out_shape=[jax.ShapeDtypeStruct((M, D), F32)]
                  + [jax.ShapeDtypeStruct(p.shape, BF16) for p in plans],
        scratch_shapes=[pltpu.VMEM((bm, D), BF16)],
        compiler_params=pltpu.CompilerParams(
            dimension_semantics=("arbitrary", "arbitrary"), vmem_limit_bytes=VMEM_LIMIT),
        name="ffn",
    )(x, nw, wg, wu, wd, fw, *round_weights)
    return out, rounded


def _pool_kernel(x_ref, nw_ref, wp_ref, ps_ref, hist_ref, o_ref, pst_ref, ext_scr, band_scr,
                 *, bm, tps, stride, pos_base):
    m = pl.program_id(0)
    hs = POOL_HIST * stride
    base = _halo_base(hs)
    rb = band_scr.shape[1]
    nd = min(rb, base)
    assert bm % rb == 0
    assert nd == rb or (stride == 1 and nd % SUBLANES == 0 and nd >= max(POOL_WINDOWS))
    grp = wp_ref.shape[1]
    first = (m % tps) == 0

    if nd < rb:
        @pl.when(m == 0)
        def _():
            lag = (lax.broadcasted_iota(jnp.int32, (rb, rb), 0)
                   - lax.broadcasted_iota(jnp.int32, (rb, rb), 1))
            for gi, w in enumerate(POOL_WINDOWS):
                band_scr[gi] = jnp.where((lag >= 0) & (lag < w), 1.0, 0.0).astype(BF16)

    @pl.when(first)
    def _():
        ext_scr[base - hs:base, :] = hist_ref[...]

    @pl.when(jnp.logical_not(first))
    def _():
        ext_scr[base - hs:base, :] = ext_scr[base + bm - hs:base + bm, :]

    ext_scr[base:base + bm, :] = _rmsnorm(x_ref[...], nw_ref[...])
    for r0 in range(0, bm, rb):
        r = (m % tps) * bm + r0 + lax.broadcasted_iota(jnp.int32, (nd, 1), 0)
        pos = pos_base + (r if stride == 1 else lax.shift_right_logical(r, stride.bit_length() - 1))
        for gi, w in enumerate(POOL_WINDOWS):
            cols = slice(gi * grp, (gi + 1) * grp)
            hb = ext_scr[base + r0:base + r0 + rb, cols]
            tot = hb[:nd]
            for kk in range(1, w):
                off = base + r0 - kk * stride
                tot = tot + ext_scr[off:off + nd, cols]
            d = tot / jnp.minimum(pos + 1, w).astype(F32) - hb[:nd]
            if nd < rb:
                hi = hb.astype(BF16)
                lo = (hb - hi.astype(F32)).astype(BF16)
                band = band_scr[gi]
                full = (jnp.dot(band, hi, preferred_element_type=F32)
                        + jnp.dot(band, lo, preferred_element_type=F32))
                d = jnp.concatenate([d, full[nd:] * (1.0 / w) - hb[nd:]], axis=0)
            out = jnp.dot(d.astype(BF16), wp_ref[gi], preferred_element_type=F32)
            o_ref[r0:r0 + rb, cols] = x_ref[r0:r0 + rb, cols] + out * ps_ref[:, cols]
    pst_ref[...] = ext_scr[base + bm - hs:base + bm, :]


def _pool(x, nw, w_pool, pool_scale, hist, *, layer, bm, tps, stride, pos_base):
    M, D = x.shape
    n_seq = hist.shape[0]
    hs = POOL_HIST * stride
    assert stride & (stride - 1) == 0 and hist.shape == (n_seq, hs, D) and M % bm == 0
    base = _halo_base(hs)
    _, ng, grp, _ = w_pool.shape
    assert ng == len(POOL_WINDOWS) and ng * grp == D
    kern = functools.partial(_pool_kernel, bm=bm, tps=tps, stride=stride, pos_base=pos_base)
    return pl.pallas_call(
        kern,
        grid=(M // bm,),
        in_specs=[
            pl.BlockSpec((bm, D), lambda m: (m, 0)),
            pl.BlockSpec((1, D), lambda m: (0, 0)),
            pl.BlockSpec((None, ng, grp, grp), lambda m: (layer, 0, 0, 0)),
            pl.BlockSpec((1, D), lambda m: (0, 0)),
            pl.BlockSpec((None, hs, D), lambda m: (m // tps, 0, 0)),
        ],
        out_specs=[
            pl.BlockSpec((bm, D), lambda m: (m, 0)),
            pl.BlockSpec((None, hs, D), lambda m: (m // tps, 0, 0)),
        ],
        out_shape=[
            jax.ShapeDtypeStruct((M, D), F32),
            jax.ShapeDtypeStruct((n_seq, hs, D), F32),
        ],
        scratch_shapes=[pltpu.VMEM((base + bm, D), F32),
                        pltpu.VMEM((ng, min(bm, MXU_DIM), min(bm, MXU_DIM)), BF16)],
        compiler_params=pltpu.CompilerParams(
            dimension_semantics=("arbitrary",), vmem_limit_bytes=VMEM_LIMIT),
        name="pool",
    )(x, nw, w_pool, pool_scale, hist)


def _head_match(n_rows, n_cols, n_heads, t_pad):
    rh = lax.broadcasted_iota(jnp.int32, (n_rows, n_cols), 0) // t_pad
    ch = lax.broadcasted_iota(jnp.int32, (n_rows, n_cols), 1) % n_heads
    return rh == ch


def _fold_heads(x, n_heads, t_pad):
    out = x[0:t_pad]
    for hh in range(1, n_heads):
        out = out + x[hh * t_pad:(hh + 1) * t_pad]
    return out


def _lane_class(x, n_heads, op):
    sh = n_heads
    while sh < x.shape[-1]:
        x = op(x, pltpu.roll(x, sh, axis=x.ndim - 1))
        sh *= 2
    return x


def _lane_tiles(x, op):
    out = x[:, 0:128]
    for t in range(1, x.shape[1] // 128):
        out = op(out, x[:, t * 128:(t + 1) * 128])
    return out


def _sa_logits_kernel(pt_ref, q_ref, *refs, n_pages, n_heads, t_pad, pages_per_block):
    k_refs = refs[:n_pages]
    r_ref, g_ref = refs[n_pages:]
    qa = q_ref[...]
    rows = k_refs[0].shape[0] * n_heads
    match = _head_match(qa.shape[0], rows, n_heads, t_pad)
    for pg in range(n_pages):
        k2 = k_refs[pg][...].reshape(rows, HEAD_DIM).astype(BF16)
        l2 = lax.dot_general(qa, k2, _NT, preferred_element_type=F32)
        r_ref[pg] = _fold_heads(jnp.where(match, l2, 0.0), n_heads, t_pad)
    for bb in range(n_pages // pages_per_block):
        tot = _lane_tiles(r_ref[bb * pages_per_block], jnp.add)
        for pg in range(1, pages_per_block):
            tot = tot + _lane_tiles(r_ref[bb * pages_per_block + pg], jnp.add)
        g_ref[bb] = _lane_class(tot, n_heads, jnp.add)


def _sa_values_kernel(pt_ref, r_ref, g_ref, q_ref, kn_ref, vn_ref, *refs,
                      n_pages, n_heads, t_pad, pages_per_block):
    v_refs = refs[:n_pages]
    o_ref, p_scr, sel_scr, l_scr, acc_scr = refs[n_pages:]
    c = pl.program_id(1)
    n_blocks = g_ref.shape[0]
    rows = v_refs[0].shape[0] * n_heads
    n_rows = q_ref.shape[0]
    match = _head_match(n_rows, rows, n_heads, t_pad)

    def spread(p):
        return jnp.where(match[:, :p.shape[1]], jnp.concatenate([p] * n_heads, axis=0), 0.0).astype(BF16)

    @pl.when(c == 0)
    def _():
        gates = g_ref[...]
        idx = lax.broadcasted_iota(jnp.int32, gates.shape, 0)
        sel = jnp.zeros(gates.shape, F32)
        for _ in range(MOBA_TOPK):
            mx = jnp.max(gates, axis=0, keepdims=True)
            first = jnp.min(jnp.where(gates == mx, idx, n_blocks), axis=0, keepdims=True)
            hit = idx == first
            sel = jnp.where(hit, 1.0, sel)
            gates = jnp.where(hit, -jnp.inf, gates)
        sel_scr[...] = sel

        own_w = kn_ref.shape[0]
        lo = lax.dot_general(q_ref[...], kn_ref[...].astype(BF16), _NT, preferred_element_type=F32)
        ro = _fold_heads(jnp.where(match[:, :own_w], lo, 0.0), n_heads, t_pad)
        t_key = lax.broadcasted_iota(jnp.int32, ro.shape, 1) // n_heads
        t_qry = lax.broadcasted_iota(jnp.int32, ro.shape, 0)
        ro = jnp.where(t_key <= t_qry, ro, NEG_INF)

        def max_body(b, mx):
            here = _lane_tiles(r_ref[b * pages_per_block], jnp.maximum)
            for pg in range(1, pages_per_block):
                here = jnp.maximum(here, _lane_tiles(r_ref[b * pages_per_block + pg], jnp.maximum))
            return jnp.maximum(mx, jnp.where(sel_scr[b] > 0.0, here, NEG_INF))

        mx = lax.fori_loop(0, n_blocks, max_body, ro)
        mx = _lane_class(mx, n_heads, jnp.maximum)
        width = r_ref.shape[2]
        mx_w = jnp.concatenate([mx] * (width // 128), axis=1)

        def exp_body(b, tot):
            keep = jnp.concatenate([sel_scr[b]] * (width // 128), axis=1) > 0.0
            for pg in range(pages_per_block):
                p = jnp.where(keep, jnp.exp(r_ref[b * pages_per_block + pg] - mx_w), 0.0)
                p_scr[b * pages_per_block + pg] = p
                tot = tot + _lane_tiles(p, jnp.add)
            return tot

        po = jnp.exp(ro - mx)
        tot = lax.fori_loop(0, n_blocks, exp_body, po)
        l_scr[...] = _lane_class(tot, n_heads, jnp.add)
        acc_scr[...] = jnp.dot(spread(po), vn_ref[...].astype(BF16), preferred_element_type=F32)

    acc = acc_scr[...]
    for pg in range(n_pages):
        v2 = v_refs[pg][...].reshape(rows, HEAD_DIM).astype(BF16)
        acc = acc + jnp.dot(spread(p_scr[c * n_pages + pg]), v2, preferred_element_type=F32)
    acc_scr[...] = acc

    @pl.when(c == pl.num_programs(1) - 1)
    def _():
        l = l_scr[...]
        den = jnp.concatenate([l[:, hh:hh + 1] for hh in range(n_heads)], axis=0)
        o_ref[...] = acc_scr[...] / den


def _sample_attention(q_all, k_new, v_new, cache_k, cache_v, page_table, *, layer, n_heads, t_pad):
    n_dec, n_pt = page_table.shape
    page = cache_k.shape[2]
    ppb = MOBA_BLOCK // page
    P = PAGES_PER_STEP
    assert MOBA_BLOCK % page == 0 and n_pt % P == 0 and P % ppb == 0
    n_steps = n_pt // P
    n_blocks = n_pt // ppb
    width = page * n_heads
    n_rows = n_heads * t_pad

    def page_spec(i):
        return pl.BlockSpec((None, None, page, n_heads, HEAD_DIM),
                            lambda b, c, pt: (layer, pt[b, c * P + i], 0, 0, 0))

    params = pltpu.CompilerParams(dimension_semantics=("arbitrary", "arbitrary"),
                                  vmem_limit_bytes=VMEM_LIMIT)
    common = dict(n_pages=P, n_heads=n_heads, t_pad=t_pad, pages_per_block=ppb)
    logits, gates = pl.pallas_call(
        functools.partial(_sa_logits_kernel, **common),
        grid_spec=pltpu.PrefetchScalarGridSpec(
            num_scalar_prefetch=1,
            grid=(n_dec, n_steps),
            in_specs=[pl.BlockSpec((None, n_rows, HEAD_DIM), lambda b, c, pt: (b, 0, 0))]
                     + [page_spec(i) for i in range(P)],
            out_specs=[
                pl.BlockSpec((None, P, t_pad, width), lambda b, c, pt: (b, c, 0, 0)),
                pl.BlockSpec((None, P // ppb, t_pad, 128), lambda b, c, pt: (b, c, 0, 0)),
            ],
        ),
        out_shape=[
            jax.ShapeDtypeStruct((n_dec, n_pt, t_pad, width), F32),
            jax.ShapeDtypeStruct((n_dec, n_blocks, t_pad, 128), F32),
        ],
        compiler_params=params,
        name="sample_logits",
    )(page_table, q_all, *([cache_k] * P))

    own = k_new.shape[1]
    return pl.pallas_call(
        functools.partial(_sa_values_kernel, **common),
        grid_spec=pltpu.PrefetchScalarGridSpec(
            num_scalar_prefetch=1,
            grid=(n_dec, n_steps),
            in_specs=[
                pl.BlockSpec((None, n_pt, t_pad, width), lambda b, c, pt: (b, 0, 0, 0)),
                pl.BlockSpec((None, n_blocks, t_pad, 128), lambda b, c, pt: (b, 0, 0, 0)),
                pl.BlockSpec((None, n_rows, HEAD_DIM), lambda b, c, pt: (b, 0, 0)),
                pl.BlockSpec((None, own, HEAD_DIM), lambda b, c, pt: (b, 0, 0)),
                pl.BlockSpec((None, own, HEAD_DIM), lambda b, c, pt: (b, 0, 0)),
            ] + [page_spec(i) for i in range(P)],
            out_specs=pl.BlockSpec((None, n_rows, HEAD_DIM), lambda b, c, pt: (b, 0, 0)),
            scratch_shapes=[
                pltpu.VMEM((n_pt, t_pad, width), F32),
                pltpu.VMEM((n_blocks, t_pad, 128), F32),
                pltpu.VMEM((t_pad, 128), F32),
                pltpu.VMEM((n_rows, HEAD_DIM), F32),
            ],
        ),
        out_shape=jax.ShapeDtypeStruct((n_dec, n_rows, HEAD_DIM), F32),
        compiler_params=params,
        name="sample_values",
    )(page_table, logits, gates, q_all, k_new, v_new, *([cache_v] * P))


def _rope_tables(pos):
    half = HEAD_DIM // 2
    inv = ROPE_THETA ** (-jnp.arange(half, dtype=F32) / half)
    ang = pos.astype(F32)[:, None] * inv[None, :]
    cos, sin = jnp.cos(ang), jnp.sin(ang)
    return jnp.concatenate([cos, cos], axis=1), jnp.concatenate([-sin, sin], axis=1)


def kernel(x_prompt, x_sample, cache_k, cache_v, page_table, state_conv, state_pool,
           norm_mix, norm_ffn, norm_final, w_in, conv_w, w_o, w_pool, pool_scale,
           w_gate, w_up, w_down):
    n_batch, n_seq, d = x_prompt.shape
    n_dec, n_new, _ = x_sample.shape
    depth = norm_mix.shape[0]
    seg = conv_w.shape[2]
    n_heads = seg // HEAD_DIM
    page = cache_k.shape[2]
    past_len = page_table.shape[1] * page
    assert n_dec == SUBLANES and n_new <= SUBLANES and past_len % MOBA_BLOCK == 0
    assert cache_k.shape[3] == n_heads and cache_k.shape[4] == HEAD_DIM
    q_scale = HEAD_DIM ** -0.5
    q_scale_log2 = q_scale * LOG2_E
    t_pad = SUBLANES
    own_keys = 128 // n_heads
    assert n_new <= own_keys

    w_in_b, w_o_b, w_pool_b = w_in.astype(BF16), w_o.astype(BF16), w_pool.astype(BF16)
    ffn_f32 = (w_gate, w_up, w_down)
    ffn_w = None
    row = lambda a: a.reshape(1, -1)

    xp = x_prompt.reshape(n_batch * n_seq, d)
    xs = x_sample.transpose(1, 0, 2).reshape(n_new * n_dec, d)
    ms = n_new * n_dec
    bm = 512
    tps = n_seq // bm
    cos_p, sin_p = _rope_tables(jnp.arange(n_seq, dtype=jnp.int32))
    pos_s = past_len + jnp.repeat(jnp.arange(n_new, dtype=jnp.int32), n_dec)
    cos_s, sin_s = _rope_tables(pos_s)

    def tm(a):
        return a.transpose(1, 0, 2).reshape(1, a.shape[1] * n_dec, a.shape[2])

    def untm(a, rows):
        return a.reshape(rows, n_dec, a.shape[-1]).transpose(1, 0, 2)

    k_p, v_p, k_s, v_s, conv_p, conv_s, pool_p, pool_s = [], [], [], [], [], [], [], []
    for layer in range(depth):
        if layer % 2 == 0:
            e = layer // 2
            q, k, v, yc, cst = _inproj(
                xp, row(norm_mix[layer]), w_in_b, cos_p, sin_p, conv_w[e],
                jnp.zeros((n_batch, CONV_HIST, seg), F32),
                layer=e, bm=bm, tps=tps, stride=1, q_scale=q_scale_log2)
            attn, rounded = _moba_prompt(q, k, v, n_batch=n_batch, seq=n_seq, n_heads=n_heads,
                                         round_weights=ffn_f32 if layer == 0 else ())
            if layer == 0:
                ffn_w = rounded
            xp = _outproj(attn, yc, w_o_b, xp, layer=e, bm=bm, bn=d)
            k_p.append(k.reshape(n_batch, n_seq, n_heads, HEAD_DIM))
            v_p.append(v.reshape(n_batch, n_seq, n_heads, HEAD_DIM))
            conv_p.append(cst)
            q, k, v, yc, cst = _inproj(
                xs, row(norm_mix[layer]), w_in_b, cos_s, sin_s, conv_w[e], tm(state_conv[e]),
                layer=e, bm=ms, tps=1, stride=n_dec, q_scale=q_scale)
            heads = lambda a: a.reshape(n_new, n_dec, n_heads, HEAD_DIM)
            q_all = jnp.pad(heads(q).transpose(1, 2, 0, 3), ((0, 0), (0, 0), (0, t_pad - n_new), (0, 0)))
            q_all = q_all.reshape(n_dec, n_heads * t_pad, HEAD_DIM)
            k_new = heads(k).transpose(1, 0, 2, 3)
            v_new = heads(v).transpose(1, 0, 2, 3)
            own = lambda a: jnp.pad(a, ((0, 0), (0, own_keys - n_new), (0, 0), (0, 0))).reshape(
                n_dec, own_keys * n_heads, HEAD_DIM)
            o = _sample_attention(q_all, own(k_new), own(v_new), cache_k, cache_v, page_table,
                                  layer=e, n_heads=n_heads, t_pad=t_pad)
            o = o.reshape(n_dec, n_heads, t_pad, HEAD_DIM)[:, :, :n_new]
            attn = o.transpose(2, 0, 1, 3).reshape(ms, seg).astype(BF16)
            xs = _outproj(attn, yc, w_o_b, xs, layer=e, bm=ms, bn=d)
            k_s.append(k_new)
            v_s.append(v_new)
            conv_s.append(untm(cst[0], CONV_HIST))
        else:
            o_ = layer // 2
            xp, pst = _pool(xp, row(norm_mix[layer]), w_pool_b, row(pool_scale[o_]),
                            jnp.zeros((n_batch, POOL_HIST, d), F32),
                            layer=o_, bm=bm, tps=tps, stride=1, pos_base=0)
            pool_p.append(pst)
            xs, pst = _pool(xs, row(norm_mix[layer]), w_pool_b, row(pool_scale[o_]),
                            tm(state_pool[o_]), layer=o_, bm=ms, tps=1, stride=n_dec,
                            pos_base=past_len)
            pool_s.append(untm(pst[0], POOL_HIST))
        last = layer == depth - 1
        ffn = functools.partial(_ffn, nw=row(norm_ffn[layer]), wg=ffn_w[0], wu=ffn_w[1], wd=ffn_w[2],
                                fw=row(norm_final), final_norm=last)
        xp, ffn_next = ffn(xp, bm=2 * bm, bf=512 if last else 256,
                           round_weights=() if last else ffn_f32, round_layer=layer + 1)
        xs, _ = ffn(xs, bm=ms, bf=512)
        ffn_w = ffn_next

    y_prompt = xp.reshape(n_batch, n_seq, d)
    y_sample = xs.reshape(n_new, n_dec, d).transpose(1, 0, 2)
    return (y_prompt, y_sample, jnp.stack(k_p), jnp.stack(v_p), jnp.stack(k_s), jnp.stack(v_s),
            jnp.stack(conv_p), jnp.stack(conv_s), jnp.stack(pool_p), jnp.stack(pool_s))
```

```python
import functools

import jax
import jax.numpy as jnp
from jax import lax
from jax.experimental import pallas as pl
from jax.experimental.pallas import tpu as pltpu

F32 = jnp.float32
BF16 = jnp.bfloat16

HEAD_DIM = 128
MOBA_BLOCK = 256
MOBA_TOPK = 3
CONV_K = 3
CONV_HIST = CONV_K - 1
ROPE_THETA = 10000.0
POOL_WINDOWS = (2, 4, 8, 16)
POOL_HIST = max(POOL_WINDOWS) - 1
RMS_EPS = 1e-6
NEG_INF = -1e30
LOG2_E = 1.4426950408889634

SUBLANES = 8
MXU_DIM = 256
VMEM_LIMIT = 56 * 1024 * 1024
PAGES_PER_STEP = 32

_NT = (((1,), (1,)), ((), ()))


def _rmsnorm(x, g):
    ms = jnp.mean(x * x, axis=-1, keepdims=True)
    return x * lax.rsqrt(ms + RMS_EPS) * g


def _halo_base(rows):
    return -(-rows // SUBLANES) * SUBLANES


class _CastPlan:
    def __init__(self, stacked, layer, n_steps):
        _, rows, cols = stacked.shape
        bf16_rows = 16
        assert cols % 128 == 0 and rows % bf16_rows == 0
        self.count = max(d for d in range(1, min(n_steps, rows // bf16_rows) + 1)
                         if (rows // bf16_rows) % d == 0)
        self.block = (rows // self.count, cols)
        self.layer = layer
        self.shape = (rows, cols)

    def _index(self, step):
        return jnp.minimum(step, self.count - 1), 0

    def specs(self, step_of, src_mode=None):
        def src(*g):
            return (self.layer,) + self._index(step_of(*g))
        def dst(*g):
            return self._index(step_of(*g))
        return (pl.BlockSpec((None,) + self.block, src, pipeline_mode=src_mode),
                pl.BlockSpec(self.block, dst))


def _cast_step(src_refs, dst_refs):
    for src, dst in zip(src_refs, dst_refs):
        dst[...] = src[...].astype(BF16)


def _qkv_kernel(x_ref, nw_ref, w_ref, cos_ref, sin_ref, q_ref, k_ref, v_ref, *, n_heads, q_scale):
    seg = n_heads * HEAD_DIM
    h = _rmsnorm(x_ref[...], nw_ref[...]).astype(BF16)
    cos = cos_ref[...]
    sin = sin_ref[...]

    def rope(a, hh):
        part = a[:, hh * HEAD_DIM:(hh + 1) * HEAD_DIM]
        return part * cos + pltpu.roll(part, HEAD_DIM // 2, axis=1) * sin

    aq = jnp.dot(h, w_ref[:, 0:seg], preferred_element_type=F32)
    for hh in range(n_heads):
        q_ref[:, hh * HEAD_DIM:(hh + 1) * HEAD_DIM] = (rope(aq, hh) * q_scale).astype(BF16)
    ak = jnp.dot(h, w_ref[:, seg:2 * seg], preferred_element_type=F32)
    for hh in range(n_heads):
        k_ref[:, hh * HEAD_DIM:(hh + 1) * HEAD_DIM] = rope(ak, hh)
    v_ref[...] = jnp.dot(h, w_ref[:, 2 * seg:3 * seg], preferred_element_type=F32)


def _gconv_kernel(x_ref, nw_ref, w_ref, cw_ref, hist_ref, yc_ref, cst_ref, ue_scr,
                  *, bm, tps, stride):
    m = pl.program_id(0)
    seg = yc_ref.shape[1]
    hs = CONV_HIST * stride
    base = _halo_base(hs)
    h = _rmsnorm(x_ref[...], nw_ref[...]).astype(BF16)
    u = (jnp.dot(h, w_ref[:, seg:2 * seg], preferred_element_type=F32)
         * jnp.dot(h, w_ref[:, 2 * seg:3 * seg], preferred_element_type=F32))
    first = (m % tps) == 0

    @pl.when(first)
    def _():
        ue_scr[base - hs:base, :] = hist_ref[...]

    @pl.when(jnp.logical_not(first))
    def _():
        ue_scr[base - hs:base, :] = ue_scr[base + bm - hs:base + bm, :]

    ue_scr[base:base + bm, :] = u
    cw = cw_ref[...]
    conv = u * cw[CONV_K - 1:CONV_K, :]
    for j in range(CONV_K - 1):
        off = base - (CONV_K - 1 - j) * stride
        conv = conv + ue_scr[off:off + bm, :] * cw[j:j + 1, :]
    gate_b = jnp.dot(h, w_ref[:, 0:seg], preferred_element_type=F32)
    yc_ref[...] = (gate_b * conv).astype(BF16)
    cst_ref[...] = ue_scr[base + bm - hs:base + bm, :]


def _inproj(x, nw, w_in, cos, sin, conv_w, hist, *, layer, bm, tps, stride, q_scale):
    M, D = x.shape
    seg = conv_w.shape[1]
    assert w_in.shape[2] == 6 * seg and M % bm == 0 and (M // bm) % tps == 0
    n_seq = hist.shape[0]
    hs = CONV_HIST * stride
    assert hist.shape == (n_seq, hs, seg) and bm >= hs
    base = _halo_base(hs)
    params = pltpu.CompilerParams(dimension_semantics=("arbitrary",), vmem_limit_bytes=VMEM_LIMIT)
    rows = lambda width: pl.BlockSpec((bm, width), lambda m: (m, 0))
    half = lambda n: pl.BlockSpec((None, D, 3 * seg), lambda m: (layer, 0, n),
                                  pipeline_mode=pl.Buffered(1))
    norm = pl.BlockSpec((1, D), lambda m: (0, 0))
    table = pl.BlockSpec((bm, HEAD_DIM), lambda m: (m % tps, 0))
    q, k, v = pl.pallas_call(
        functools.partial(_qkv_kernel, n_heads=seg // HEAD_DIM, q_scale=q_scale),
        grid=(M // bm,),
        in_specs=[rows(D), norm, half(0), table, table],
        out_specs=[rows(seg), rows(seg), rows(seg)],
        out_shape=[
            jax.ShapeDtypeStruct((M, seg), BF16),
            jax.ShapeDtypeStruct((M, seg), F32),
            jax.ShapeDtypeStruct((M, seg), F32),
        ],
        compiler_params=params,
        name="inproj_qkv",
    )(x, nw, w_in, cos, sin)
    state = pl.BlockSpec((None, hs, seg), lambda m: (m // tps, 0, 0))
    yc, cst = pl.pallas_call(
        functools.partial(_gconv_kernel, bm=bm, tps=tps, stride=stride),
        grid=(M // bm,),
        in_specs=[rows(D), norm, half(1), pl.BlockSpec((CONV_K, seg), lambda m: (0, 0)), state],
        out_specs=[rows(seg), state],
        out_shape=[
            jax.ShapeDtypeStruct((M, seg), BF16),
            jax.ShapeDtypeStruct((n_seq, hs, seg), F32),
        ],
        scratch_shapes=[pltpu.VMEM((base + bm, seg), F32)],
        compiler_params=params,
        name="inproj_gconv",
    )(x, nw, w_in, conv_w, hist)
    return q, k, v, yc, cst


def _group_reduce(x, op):
    return op(x.reshape(x.shape[0] // SUBLANES, SUBLANES, x.shape[1]), axis=0)


def _moba_kernel(*refs, nq, plans):
    n = len(plans)
    q_ref, k_ref, v_ref = refs[:3]
    o_ref = refs[3 + n]
    kb_scr, vt_scr, ks_scr, qt_scr, sel_scr, s_scr = refs[4 + 2 * n:]
    _cast_step(refs[3:3 + n], refs[4 + n:4 + 2 * n])
    _moba_body(q_ref, k_ref, v_ref, o_ref, kb_scr, vt_scr, ks_scr, qt_scr, sel_scr, s_scr, nq=nq)


def _moba_body(q_ref, k_ref, v_ref, o_ref, kb_scr, vt_scr, ks_scr, qt_scr, sel_scr, s_scr,
               *, nq):
    i = pl.program_id(2)
    blk = MOBA_BLOCK

    @pl.when(i == 0)
    def _():
        for j in range(nq):
            kf = k_ref[j * blk:(j + 1) * blk, :]
            kb_scr[j] = kf.astype(BF16)
            vt_scr[j] = v_ref[j * blk:(j + 1) * blk, :].T.astype(BF16)
            ks_scr[j:j + 1, :] = jnp.sum(kf, axis=0, keepdims=True)
        ks = ks_scr[...]
        ks_hi = ks.astype(BF16)
        ks_lo = (ks - ks_hi.astype(F32)).astype(BF16)
        blk_id = lax.broadcasted_iota(jnp.int32, (nq, blk), 0)
        for own in range(nq):
            qt = q_ref[own * blk:(own + 1) * blk, :].astype(F32).T.astype(BF16)
            qt_scr[own] = qt
            gate = (jnp.dot(ks_hi, qt, preferred_element_type=F32)
                    + jnp.dot(ks_lo, qt, preferred_element_type=F32))
            g = jnp.where(blk_id < own, gate, -jnp.inf)
            sel = jnp.zeros((nq, blk), jnp.bool_)
            for _ in range(min(MOBA_TOPK, own)):
                top = jnp.max(g, axis=0, keepdims=True)
                first = jnp.min(jnp.where(g == top, blk_id, nq), axis=0, keepdims=True)
                hit = blk_id == first
                sel = sel | hit
                g = jnp.where(hit, -jnp.inf, g)
            sel_scr[own] = jnp.where(sel, 1.0, 0.0)

    for pair in range(nq // 2):
        @pl.when(i == pair)
        def _(pair=pair):
            _moba_pair(pair, o_ref, kb_scr, vt_scr, qt_scr, sel_scr, s_scr, nq=nq)


def _moba_pair(pair, o_ref, kb_scr, vt_scr, qt_scr, sel_scr, s_scr, *, nq):
    blk = MOBA_BLOCK
    owns = (pair, nq - 1 - pair)

    tiles = [(w, j) for w, own in enumerate(owns) for j in range(own)]
    n_gen = len(tiles)

    mx = [None, None]
    for t, (w, j) in enumerate(tiles):
        s = jnp.dot(kb_scr[j], qt_scr[owns[w]], preferred_element_type=F32)
        s = jnp.where(sel_scr[owns[w], j:j + 1, :] > 0.0, s, NEG_INF)
        s_scr[t] = s
        cm = _group_reduce(s, jnp.max)
        mx[w] = cm if mx[w] is None else jnp.maximum(mx[w], cm)
    key_id = lax.broadcasted_iota(jnp.int32, (blk, blk), 0)
    qry_id = lax.broadcasted_iota(jnp.int32, (blk, blk), 1)
    for w, own in enumerate(owns):
        s = jnp.dot(kb_scr[own], qt_scr[own], preferred_element_type=F32)
        s = jnp.where(key_id <= qry_id, s, NEG_INF)
        s_scr[n_gen + w] = s
        cm = _group_reduce(s, jnp.max)
        mx[w] = cm if mx[w] is None else jnp.maximum(mx[w], cm)
    mrow = [jnp.max(m, axis=0, keepdims=True) for m in mx]

    acc = [None, None]
    den = [None, None]
    order = list(enumerate(tiles)) + [(n_gen + w, (w, own)) for w, own in enumerate(owns)]
    for t, (w, j) in order:
        p = jnp.exp2(s_scr[t] - mrow[w])
        cs = _group_reduce(p, jnp.sum)
        den[w] = cs if den[w] is None else den[w] + cs
        r = jnp.dot(vt_scr[j], p.astype(BF16), preferred_element_type=F32)
        acc[w] = r if acc[w] is None else acc[w] + r
    for w, own in enumerate(owns):
        tot = jnp.sum(den[w], axis=0, keepdims=True)
        o_ref[own * blk:(own + 1) * blk, :] = (acc[w] / tot).T.astype(BF16)


def _moba_prompt(q, k, v, *, n_batch, seq, n_heads, round_weights=(), round_layer=0):
    blk = MOBA_BLOCK
    assert seq % (2 * blk) == 0
    nq = seq // blk
    width = n_heads * HEAD_DIM
    grid = (n_batch, n_heads, nq // 2)
    plans = [_CastPlan(w, round_layer, grid[0] * grid[1] * grid[2]) for w in round_weights]
    cast_specs = [p.specs(lambda b, h, i: (b * grid[1] + h) * grid[2] + i) for p in plans]
    col = pl.BlockSpec((None, seq, HEAD_DIM), lambda b, h, i: (b, 0, h))
    out, *rounded = pl.pallas_call(
        functools.partial(_moba_kernel, nq=nq, plans=plans),
        grid=grid,
        in_specs=[col, col, col] + [s for s, _ in cast_specs],
        out_specs=[col] + [d for _, d in cast_specs],
        out_shape=[jax.ShapeDtypeStruct((n_batch, seq, width), BF16)]
                  + [jax.ShapeDtypeStruct(p.shape, BF16) for p in plans],
        scratch_shapes=[
            pltpu.VMEM((nq, blk, HEAD_DIM), BF16),
            pltpu.VMEM((nq, HEAD_DIM, blk), BF16),
            pltpu.VMEM((nq, HEAD_DIM), F32),
            pltpu.VMEM((nq, HEAD_DIM, blk), BF16),
            pltpu.VMEM((nq, nq, blk), F32),
            pltpu.VMEM((nq + 1, blk, blk), F32),
        ],
        compiler_params=pltpu.CompilerParams(
            dimension_semantics=("arbitrary", "arbitrary", "arbitrary"),
            vmem_limit_bytes=VMEM_LIMIT),
        name="moba_prompt",
    )(q.reshape(n_batch, seq, width), k.reshape(n_batch, seq, width),
      v.reshape(n_batch, seq, width), *round_weights)
    return out.reshape(n_batch * seq, width), rounded


def _outproj_kernel(a_ref, c_ref, wt_ref, wb_ref, x_ref, o_ref):
    o_ref[...] = (x_ref[...]
                  + jnp.dot(a_ref[...], wt_ref[...], preferred_element_type=F32)
                  + jnp.dot(c_ref[...], wb_ref[...], preferred_element_type=F32))


def _outproj(a, c, w_o, x, *, layer, bm, bn):
    M, D = x.shape
    half = a.shape[1]
    assert w_o.shape[1:] == (2 * half, D) and c.shape == a.shape and D % bn == 0 and M % bm == 0
    w_mode = pl.Buffered(1) if bn == D else None
    return pl.pallas_call(
        _outproj_kernel,
        grid=(M // bm, D // bn),
        in_specs=[
            pl.BlockSpec((bm, half), lambda m, n: (m, 0)),
            pl.BlockSpec((bm, half), lambda m, n: (m, 0)),
            pl.BlockSpec((None, half, bn), lambda m, n: (layer, 0, n), pipeline_mode=w_mode),
            pl.BlockSpec((None, half, bn), lambda m, n: (layer, 1, n), pipeline_mode=w_mode),
            pl.BlockSpec((bm, bn), lambda m, n: (m, n)),
        ],
        out_specs=pl.BlockSpec((bm, bn), lambda m, n: (m, n)),
        out_shape=jax.ShapeDtypeStruct((M, D), F32),
        compiler_params=pltpu.CompilerParams(
            dimension_semantics=("arbitrary", "arbitrary"), vmem_limit_bytes=VMEM_LIMIT),
        name="outproj",
    )(a, c, w_o, w_o, x)


def _ffn_kernel(*refs, final_norm, plans):
    n = len(plans)
    x_ref, nw_ref, wg_ref, wu_ref, wd_ref, fw_ref = refs[:6]
    o_ref = refs[6 + n]
    h_scr = refs[7 + 2 * n]
    f = pl.program_id(1)

    @pl.when(f == 0)
    def _():
        x = x_ref[...]
        h_scr[...] = _rmsnorm(x, nw_ref[...]).astype(BF16)
        o_ref[...] = x

    _cast_step(refs[6:6 + n], refs[7 + n:7 + 2 * n])
    h = h_scr[...]
    g = jnp.dot(h, wg_ref[...], preferred_element_type=F32)
    u = jnp.dot(h, wu_ref[...], preferred_element_type=F32)
    a = (g * jax.nn.sigmoid(g) * u).astype(BF16)
    o_ref[...] += jnp.dot(a, wd_ref[...], preferred_element_type=F32)

    if final_norm:
        @pl.when(f == pl.num_programs(1) - 1)
        def _():
            o_ref[...] = _rmsnorm(o_ref[...], fw_ref[...])


def _ffn(x, nw, wg, wu, wd, fw, *, bm, bf, final_norm, round_weights=(), round_layer=0):
    M, D = x.shape
    FF = wg.shape[1]
    assert FF % bf == 0 and M % bm == 0
    grid = (M // bm, FF // bf)
    plans = [_CastPlan(w, round_layer, grid[0] * grid[1]) for w in round_weights]
    cast_specs = [p.specs(lambda m, f: m * grid[1] + f) for p in plans]
    out, *rounded = pl.pallas_call(
        functools.partial(_ffn_kernel, final_norm=final_norm, plans=plans),
        grid=grid,
        in_specs=[
            pl.BlockSpec((bm, D), lambda m, f: (m, 0)),
            pl.BlockSpec((1, D), lambda m, f: (0, 0)),
            pl.BlockSpec((D, bf), lambda m, f: (0, f)),
            pl.BlockSpec((D, bf), lambda m, f: (0, f)),
            pl.BlockSpec((bf, D), lambda m, f: (f, 0)),
            pl.BlockSpec((1, D), lambda m, f: (0, 0)),
        ] + [s for s, _ in cast_specs],
        out_specs=[pl.BlockSpec((bm, D), lambda m, f: (m, 0))] + [d for _, d in cast_specs],
        out_shape=[jax.ShapeDtypeStruct((M, D), F32)]
                  + [jax.ShapeDtypeStruct(p.shape, BF16) for p in plans],
        scratch_shapes=[pltpu.VMEM((bm, D), BF16)],
        compiler_params=pltpu.CompilerParams(
            dimension_semantics=("arbitrary", "arbitrary"), vmem_limit_bytes=VMEM_LIMIT),
        name="ffn",
    )(x, nw, wg, wu, wd, fw, *round_weights)
    return out, rounded


def _pool_kernel(x_ref, nw_ref, wp_ref, ps_ref, hist_ref, o_ref, pst_ref, ext_scr, band_scr,
                 *, bm, tps, stride, pos_base):
    m = pl.program_id(0)
    hs = POOL_HIST * stride
    base = _halo_base(hs)
    rb = band_scr.shape[1]
    nd = min(rb, base)
    assert bm % rb == 0
    assert nd == rb or (stride == 1 and nd % SUBLANES == 0 and nd >= max(POOL_WINDOWS))
    grp = wp_ref.shape[1]
    first = (m % tps) == 0

    if nd < rb:
        @pl.when(m == 0)
        def _():
            lag = (lax.broadcasted_iota(jnp.int32, (rb, rb), 0)
                   - lax.broadcasted_iota(jnp.int32, (rb, rb), 1))
            for gi, w in enumerate(POOL_WINDOWS):
                band_scr[gi] = jnp.where((lag >= 0) & (lag < w), 1.0, 0.0).astype(BF16)

    @pl.when(first)
    def _():
        ext_scr[base - hs:base, :] = hist_ref[...]

    @pl.when(jnp.logical_not(first))
    def _():
        ext_scr[base - hs:base, :] = ext_scr[base + bm - hs:base + bm, :]

    ext_scr[base:base + bm, :] = _rmsnorm(x_ref[...], nw_ref[...])
    for r0 in range(0, bm, rb):
        r = (m % tps) * bm + r0 + lax.broadcasted_iota(jnp.int32, (nd, 1), 0)
        pos = pos_base + (r if stride == 1 else lax.shift_right_logical(r, stride.bit_length() - 1))
        for gi, w in enumerate(POOL_WINDOWS):
            cols = slice(gi * grp, (gi + 1) * grp)
            hb = ext_scr[base + r0:base + r0 + rb, cols]
            tot = hb[:nd]
            for kk in range(1, w):
                off = base + r0 - kk * stride
                tot = tot + ext_scr[off:off + nd, cols]
            d = tot / jnp.minimum(pos + 1, w).astype(F32) - hb[:nd]
            if nd < rb:
                hi = hb.astype(BF16)
                lo = (hb - hi.astype(F32)).astype(BF16)
                band = band_scr[gi]
                full = (jnp.dot(band, hi, preferred_element_type=F32)
                        + jnp.dot(band, lo, preferred_element_type=F32))
                d = jnp.concatenate([d, full[nd:] * (1.0 / w) - hb[nd:]], axis=0)
            out = jnp.dot(d.astype(BF16), wp_ref[gi], preferred_element_type=F32)
            o_ref[r0:r0 + rb, cols] = x_ref[r0:r0 + rb, cols] + out * ps_ref[:, cols]
    pst_ref[...] = ext_scr[base + bm - hs:base + bm, :]


def _pool(x, nw, w_pool, pool_scale, hist, *, layer, bm, tps, stride, pos_base):
    M, D = x.shape
    n_seq = hist.shape[0]
    hs = POOL_HIST * stride
    assert stride & (stride - 1) == 0 and hist.shape == (n_seq, hs, D) and M % bm == 0
    base = _halo_base(hs)
    _, ng, grp, _ = w_pool.shape
    assert ng == len(POOL_WINDOWS) and ng * grp == D
    kern = functools.partial(_pool_kernel, bm=bm, tps=tps, stride=stride, pos_base=pos_base)
    return pl.pallas_call(
        kern,
        grid=(M // bm,),
        in_specs=[
            pl.BlockSpec((bm, D), lambda m: (m, 0)),
            pl.BlockSpec((1, D), lambda m: (0, 0)),
            pl.BlockSpec((None, ng, grp, grp), lambda m: (layer, 0, 0, 0)),
            pl.BlockSpec((1, D), lambda m: (0, 0)),
            pl.BlockSpec((None, hs, D), lambda m: (m // tps, 0, 0)),
        ],
        out_specs=[
            pl.BlockSpec((bm, D), lambda m: (m, 0)),
            pl.BlockSpec((None, hs, D), lambda m: (m // tps, 0, 0)),
        ],
        out_shape=[
            jax.ShapeDtypeStruct((M, D), F32),
            jax.ShapeDtypeStruct((n_seq, hs, D), F32),
        ],
        scratch_shapes=[pltpu.VMEM((base + bm, D), F32),
                        pltpu.VMEM((ng, min(bm, MXU_DIM), min(bm, MXU_DIM)), BF16)],
        compiler_params=pltpu.CompilerParams(
            dimension_semantics=("arbitrary",), vmem_limit_bytes=VMEM_LIMIT),
        name="pool",
    )(x, nw, w_pool, pool_scale, hist)


def _head_match(n_rows, n_cols, n_heads, t_pad):
    rh = lax.broadcasted_iota(jnp.int32, (n_rows, n_cols), 0) // t_pad
    ch = lax.broadcasted_iota(jnp.int32, (n_rows, n_cols), 1) % n_heads
    return rh == ch


def _fold_heads(x, n_heads, t_pad):
    out = x[0:t_pad]
    for hh in range(1, n_heads):
        out = out + x[hh * t_pad:(hh + 1) * t_pad]
    return out


def _lane_class(x, n_heads, op):
    sh = n_heads
    while sh < x.shape[-1]:
        x = op(x, pltpu.roll(x, sh, axis=x.ndim - 1))
        sh *= 2
    return x


def _lane_tiles(x, op):
    out = x[:, 0:128]
    for t in range(1, x.shape[1] // 128):
        out = op(out, x[:, t * 128:(t + 1) * 128])
    return out


def _sa_logits_kernel(pt_ref, q_ref, *refs, n_pages, n_heads, t_pad, pages_per_block):
    k_refs = refs[:n_pages]
    r_ref, g_ref = refs[n_pages:]
    qa = q_ref[...]
    rows = k_refs[0].shape[0] * n_heads
    match = _head_match(qa.shape[0], rows, n_heads, t_pad)
    for pg in range(n_pages):
        k2 = k_refs[pg][...].reshape(rows, HEAD_DIM).astype(BF16)
        l2 = lax.dot_general(qa, k2, _NT, preferred_element_type=F32)
        r_ref[pg] = _fold_heads(jnp.where(match, l2, 0.0), n_heads, t_pad)
    for bb in range(n_pages // pages_per_block):
        tot = _lane_tiles(r_ref[bb * pages_per_block], jnp.add)
        for pg in range(1, pages_per_block):
            tot = tot + _lane_tiles(r_ref[bb * pages_per_block + pg], jnp.add)
        g_ref[bb] = _lane_class(tot, n_heads, jnp.add)


def _sa_values_kernel(pt_ref, r_ref, g_ref, q_ref, kn_ref, vn_ref, *refs,
                      n_pages, n_heads, t_pad, pages_per_block):
    v_refs = refs[:n_pages]
    o_ref, p_scr, sel_scr, l_scr, acc_scr = refs[n_pages:]
    c = pl.program_id(1)
    n_blocks = g_ref.shape[0]
    rows = v_refs[0].shape[0] * n_heads
    n_rows = q_ref.shape[0]
    match = _head_match(n_rows, rows, n_heads, t_pad)

    def spread(p):
        return jnp.where(match[:, :p.shape[1]], jnp.concatenate([p] * n_heads, axis=0), 0.0).astype(BF16)

    @pl.when(c == 0)
    def _():
        gates = g_ref[...]
        idx = lax.broadcasted_iota(jnp.int32, gates.shape, 0)
        sel = jnp.zeros(gates.shape, F32)
        for _ in range(MOBA_TOPK):
            mx = jnp.max(gates, axis=0, keepdims=True)
            first = jnp.min(jnp.where(gates == mx, idx, n_blocks), axis=0, keepdims=True)
            hit = idx == first
            sel = jnp.where(hit, 1.0, sel)
            gates = jnp.where(hit, -jnp.inf, gates)
        sel_scr[...] = sel

        own_w = kn_ref.shape[0]
        lo = lax.dot_general(q_ref[...], kn_ref[...].astype(BF16), _NT, preferred_element_type=F32)
        ro = _fold_heads(jnp.where(match[:, :own_w], lo, 0.0), n_heads, t_pad)
        t_key = lax.broadcasted_iota(jnp.int32, ro.shape, 1) // n_heads
        t_qry = lax.broadcasted_iota(jnp.int32, ro.shape, 0)
        ro = jnp.where(t_key <= t_qry, ro, NEG_INF)

        def max_body(b, mx):
            here = _lane_tiles(r_ref[b * pages_per_block], jnp.maximum)
            for pg in range(1, pages_per_block):
                here = jnp.maximum(here, _lane_tiles(r_ref[b * pages_per_block + pg], jnp.maximum))
            return jnp.maximum(mx, jnp.where(sel_scr[b] > 0.0, here, NEG_INF))

        mx = lax.fori_loop(0, n_blocks, max_body, ro)
        mx = _lane_class(mx, n_heads, jnp.maximum)
        width = r_ref.shape[2]
        mx_w = jnp.concatenate([mx] * (width // 128), axis=1)

        def exp_body(b, tot):
            keep = jnp.concatenate([sel_scr[b]] * (width // 128), axis=1) > 0.0
            for pg in range(pages_per_block):
                p = jnp.where(keep, jnp.exp(r_ref[b * pages_per_block + pg] - mx_w), 0.0)
                p_scr[b * pages_per_block + pg] = p
                tot = tot + _lane_tiles(p, jnp.add)
            return tot

        po = jnp.exp(ro - mx)
        tot = lax.fori_loop(0, n_blocks, exp_body, po)
        l_scr[...] = _lane_class(tot, n_heads, jnp.add)
        acc_scr[...] = jnp.dot(spread(po), vn_ref[...].astype(BF16), preferred_element_type=F32)

    acc = acc_scr[...]
    for pg in range(n_pages):
        v2 = v_refs[pg][...].reshape(rows, HEAD_DIM).astype(BF16)
        acc = acc + jnp.dot(spread(p_scr[c * n_pages + pg]), v2, preferred_element_type=F32)
    acc_scr[...] = acc

    @pl.when(c == pl.num_programs(1) - 1)
    def _():
        l = l_scr[...]
        den = jnp.concatenate([l[:, hh:hh + 1] for hh in range(n_heads)], axis=0)
        o_ref[...] = acc_scr[...] / den


def _sample_attention(q_all, k_new, v_new, cache_k, cache_v, page_table, *, layer, n_heads, t_pad):
    n_dec, n_pt = page_table.shape
    page = cache_k.shape[2]
    ppb = MOBA_BLOCK // page
    P = PAGES_PER_STEP
    assert MOBA_BLOCK % page == 0 and n_pt % P == 0 and P % ppb == 0
    n_steps = n_pt // P
    n_blocks = n_pt // ppb
    width = page * n_heads
    n_rows = n_heads * t_pad

    def page_spec(i):
        return pl.BlockSpec((None, None, page, n_heads, HEAD_DIM),
                            lambda b, c, pt: (layer, pt[b, c * P + i], 0, 0, 0))

    params = pltpu.CompilerParams(dimension_semantics=("arbitrary", "arbitrary"),
                                  vmem_limit_bytes=VMEM_LIMIT)
    common = dict(n_pages=P, n_heads=n_heads, t_pad=t_pad, pages_per_block=ppb)
    logits, gates = pl.pallas_call(
        functools.partial(_sa_logits_kernel, **common),
        grid_spec=pltpu.PrefetchScalarGridSpec(
            num_scalar_prefetch=1,
            grid=(n_dec, n_steps),
            in_specs=[pl.BlockSpec((None, n_rows, HEAD_DIM), lambda b, c, pt: (b, 0, 0))]
                     + [page_spec(i) for i in range(P)],
            out_specs=[
                pl.BlockSpec((None, P, t_pad, width), lambda b, c, pt: (b, c, 0, 0)),
                pl.BlockSpec((None, P // ppb, t_pad, 128), lambda b, c, pt: (b, c, 0, 0)),
            ],
        ),
        out_shape=[
            jax.ShapeDtypeStruct((n_dec, n_pt, t_pad, width), F32),
            jax.ShapeDtypeStruct((n_dec, n_blocks, t_pad, 128), F32),
        ],
        compiler_params=params,
        name="sample_logits",
    )(page_table, q_all, *([cache_k] * P))

    own = k_new.shape[1]
    return pl.pallas_call(
        functools.partial(_sa_values_kernel, **common),
        grid_spec=pltpu.PrefetchScalarGridSpec(
            num_scalar_prefetch=1,
            grid=(n_dec, n_steps),
            in_specs=[
                pl.BlockSpec((None, n_pt, t_pad, width), lambda b, c, pt: (b, 0, 0, 0)),
                pl.BlockSpec((None, n_blocks, t_pad, 128), lambda b, c, pt: (b, 0, 0, 0)),
                pl.BlockSpec((None, n_rows, HEAD_DIM), lambda b, c, pt: (b, 0, 0)),
                pl.BlockSpec((None, own, HEAD_DIM), lambda b, c, pt: (b, 0, 0)),
                pl.BlockSpec((None, own, HEAD_DIM), lambda b, c, pt: (b, 0, 0)),
            ] + [page_spec(i) for i in range(P)],
            out_specs=pl.BlockSpec((None, n_rows, HEAD_DIM), lambda b, c, pt: (b, 0, 0)),
            scratch_shapes=[
                pltpu.VMEM((n_pt, t_pad, width), F32),
                pltpu.VMEM((n_blocks, t_pad, 128), F32),
                pltpu.VMEM((t_pad, 128), F32),
                pltpu.VMEM((n_rows, HEAD_DIM), F32),
            ],
        ),
        out_shape=jax.ShapeDtypeStruct((n_dec, n_rows, HEAD_DIM), F32),
        compiler_params=params,
        name="sample_values",
    )(page_table, logits, gates, q_all, k_new, v_new, *([cache_v] * P))


def _rope_tables(pos):
    half = HEAD_DIM // 2
    inv = ROPE_THETA ** (-jnp.arange(half, dtype=F32) / half)
    ang = pos.astype(F32)[:, None] * inv[None, :]
    cos, sin = jnp.cos(ang), jnp.sin(ang)
    return jnp.concatenate([cos, cos], axis=1), jnp.concatenate([-sin, sin], axis=1)


def kernel(x_prompt, x_sample, cache_k, cache_v, page_table, state_conv, state_pool,
           norm_mix, norm_ffn, norm_final, w_in, conv_w, w_o, w_pool, pool_scale,
           w_gate, w_up, w_down):
    n_batch, n_seq, d = x_prompt.shape
    n_dec, n_new, _ = x_sample.shape
    depth = norm_mix.shape[0]
    seg = conv_w.shape[2]
    n_heads = seg // HEAD_DIM
    page = cache_k.shape[2]
    past_len = page_table.shape[1] * page
    assert n_dec == SUBLANES and n_new <= SUBLANES and past_len % MOBA_BLOCK == 0
    assert cache_k.shape[3] == n_heads and cache_k.shape[4] == HEAD_DIM
    q_scale = HEAD_DIM ** -0.5
    q_scale_log2 = q_scale * LOG2_E
    t_pad = SUBLANES
    own_keys = 128 // n_heads
    assert n_new <= own_keys

    w_in_b, w_o_b, w_pool_b = w_in.astype(BF16), w_o.astype(BF16), w_pool.astype(BF16)
    ffn_f32 = (w_gate, w_up, w_down)
    ffn_w = None
    row = lambda a: a.reshape(1, -1)

    xp = x_prompt.reshape(n_batch * n_seq, d)
    xs = x_sample.transpose(1, 0, 2).reshape(n_new * n_dec, d)
    ms = n_new * n_dec
    bm = 512
    tps = n_seq // bm
    cos_p, sin_p = _rope_tables(jnp.arange(n_seq, dtype=jnp.int32))
    pos_s = past_len + jnp.repeat(jnp.arange(n_new, dtype=jnp.int32), n_dec)
    cos_s, sin_s = _rope_tables(pos_s)

    def tm(a):
        return a.transpose(1, 0, 2).reshape(1, a.shape[1] * n_dec, a.shape[2])

    def untm(a, rows):
        return a.reshape(rows, n_dec, a.shape[-1]).transpose(1, 0, 2)

    k_p, v_p, k_s, v_s, conv_p, conv_s, pool_p, pool_s = [], [], [], [], [], [], [], []
    for layer in range(depth):
        if layer % 2 == 0:
            e = layer // 2
            q, k, v, yc, cst = _inproj(
                xp, row(norm_mix[layer]), w_in_b, cos_p, sin_p, conv_w[e],
                jnp.zeros((n_batch, CONV_HIST, seg), F32),
                layer=e, bm=bm, tps=tps, stride=1, q_scale=q_scale_log2)
            attn, rounded = _moba_prompt(q, k, v, n_batch=n_batch, seq=n_seq, n_heads=n_heads,
                                         round_weights=ffn_f32 if layer == 0 else ())
            if layer == 0:
                ffn_w = rounded
            xp = _outproj(attn, yc, w_o_b, xp, layer=e, bm=bm, bn=d)
            k_p.append(k.reshape(n_batch, n_seq, n_heads, HEAD_DIM))
            v_p.append(v.reshape(n_batch, n_seq, n_heads, HEAD_DIM))
            conv_p.append(cst)
            q, k, v, yc, cst = _inproj(
                xs, row(norm_mix[layer]), w_in_b, cos_s, sin_s, conv_w[e], tm(state_conv[e]),
                layer=e, bm=ms, tps=1, stride=n_dec, q_scale=q_scale)
            heads = lambda a: a.reshape(n_new, n_dec, n_heads, HEAD_DIM)
            q_all = jnp.pad(heads(q).transpose(1, 2, 0, 3), ((0, 0), (0, 0), (0, t_pad - n_new), (0, 0)))
            q_all = q_all.reshape(n_dec, n_heads * t_pad, HEAD_DIM)
            k_new = heads(k).transpose(1, 0, 2, 3)
            v_new = heads(v).transpose(1, 0, 2, 3)
            own = lambda a: jnp.pad(a, ((0, 0), (0, own_keys - n_new), (0, 0), (0, 0))).reshape(
                n_dec, own_keys * n_heads, HEAD_DIM)
            o = _sample_attention(q_all, own(k_new), own(v_new), cache_k, cache_v, page_table,
                                  layer=e, n_heads=n_heads, t_pad=t_pad)
            o = o.reshape(n_dec, n_heads, t_pad, HEAD_DIM)[:, :, :n_new]
            attn = o.transpose(2, 0, 1, 3).reshape(ms, seg).astype(BF16)
            xs = _outproj(attn, yc, w_o_b, xs, layer=e, bm=ms, bn=d)
            k_s.append(k_new)
            v_s.append(v_new)
            conv_s.append(untm(cst[0], CONV_HIST))
        else:
            o_ = layer // 2
            xp, pst = _pool(xp, row(norm_mix[layer]), w_pool_b, row(pool_scale[o_]),
                            jnp.zeros((n_batch, POOL_HIST, d), F32),
                            layer=o_, bm=bm, tps=tps, stride=1, pos_base=0)
            pool_p.append(pst)
            xs, pst = _pool(xs, row(norm_mix[layer]), w_pool_b, row(pool_scale[o_]),
                            tm(state_pool[o_]), layer=o_, bm=ms, tps=1, stride=n_dec,
                            pos_base=past_len)
            pool_s.append(untm(pst[0], POOL_HIST))
        last = layer == depth - 1
        ffn = functools.partial(_ffn, nw=row(norm_ffn[layer]), wg=ffn_w[0], wu=ffn_w[1], wd=ffn_w[2],
                                fw=row(norm_final), final_norm=last)
        xp, ffn_next = ffn(xp, bm=2 * bm, bf=512 if last else 256,
                           round_weights=() if last else ffn_f32, round_layer=layer + 1)
        xs, _ = ffn(xs, bm=ms, bf=512)
        ffn_w = ffn_next

    y_prompt = xp.reshape(n_batch, n_seq, d)
    y_sample = xs.reshape(n_new, n_dec, d).transpose(1, 0, 2)
    return (y_prompt, y_sample, jnp.stack(k_p), jnp.stack(v_p), jnp.stack(k_s), jnp.stack(v_s),
            jnp.stack(conv_p), jnp.stack(conv_s), jnp.stack(pool_p), jnp.stack(pool_s))
```

```python
import functools

import jax
import jax.numpy as jnp
from jax import lax
from jax.experimental import pallas as pl
from jax.experimental.pallas import tpu as pltpu

F32 = jnp.float32
BF16 = jnp.bfloat16

HEAD_DIM = 128
MOBA_BLOCK = 256
MOBA_TOPK = 3
CONV_K = 3
CONV_HIST = CONV_K - 1
ROPE_THETA = 10000.0
POOL_WINDOWS = (2, 4, 8, 16)
POOL_HIST = max(POOL_WINDOWS) - 1
RMS_EPS = 1e-6
NEG_INF = -1e30
LOG2_E = 1.4426950408889634

SUBLANES = 8
MXU_DIM = 256
VMEM_LIMIT = 56 * 1024 * 1024
PAGES_PER_STEP = 32

_NT = (((1,), (1,)), ((), ()))


def _rmsnorm(x, g):
    ms = jnp.mean(x * x, axis=-1, keepdims=True)
    return x * lax.rsqrt(ms + RMS_EPS) * g


def _halo_base(rows):
    return -(-rows // SUBLANES) * SUBLANES


class _CastPlan:
    def __init__(self, stacked, layer, n_steps):
        _, rows, cols = stacked.shape
        bf16_rows = 16
        assert cols % 128 == 0 and rows % bf16_rows == 0
        self.count = max(d for d in range(1, min(n_steps, rows // bf16_rows) + 1)
                         if (rows // bf16_rows) % d == 0)
        self.block = (rows // self.count, cols)
        self.layer = layer
        self.shape = (rows, cols)

    def _index(self, step):
        return jnp.minimum(step, self.count - 1), 0

    def specs(self, step_of, src_mode=None):
        def src(*g):
            return (self.layer,) + self._index(step_of(*g))
        def dst(*g):
            return self._index(step_of(*g))
        return (pl.BlockSpec((None,) + self.block, src, pipeline_mode=src_mode),
                pl.BlockSpec(self.block, dst))


def _cast_step(src_refs, dst_refs):
    for src, dst in zip(src_refs, dst_refs):
        dst[...] = src[...].astype(BF16)


def _with_rounding(body, n_in, n_out, n_cast):
    def kernel(*refs, **kw):
        a, b, c = n_in + n_cast, n_in + n_cast + n_out, n_in + 2 * n_cast + n_out
        _cast_step(refs[n_in:a], refs[b:c])
        return body(*refs[:n_in], *refs[a:b], *refs[c:], **kw)
    return kernel


def _rounding_specs(weights, layer, n_steps, step_of):
    plans = [_CastPlan(w, layer, n_steps) for w in weights]
    specs = [p.specs(step_of) for p in plans]
    return ([s for s, _ in specs], [d for _, d in specs],
            [jax.ShapeDtypeStruct(p.shape, BF16) for p in plans])


def _qkv_kernel(x_ref, nw_ref, w_ref, cos_ref, sin_ref, q_ref, k_ref, v_ref, *, n_heads, q_scale):
    seg = n_heads * HEAD_DIM
    h = _rmsnorm(x_ref[...], nw_ref[...]).astype(BF16)
    cos = cos_ref[...]
    sin = sin_ref[...]

    def rope(a, hh):
        part = a[:, hh * HEAD_DIM:(hh + 1) * HEAD_DIM]
        return part * cos + pltpu.roll(part, HEAD_DIM // 2, axis=1) * sin

    aq = jnp.dot(h, w_ref[:, 0:seg], preferred_element_type=F32)
    for hh in range(n_heads):
        q_ref[:, hh * HEAD_DIM:(hh + 1) * HEAD_DIM] = (rope(aq, hh) * q_scale).astype(BF16)
    ak = jnp.dot(h, w_ref[:, seg:2 * seg], preferred_element_type=F32)
    for hh in range(n_heads):
        k_ref[:, hh * HEAD_DIM:(hh + 1) * HEAD_DIM] = rope(ak, hh)
    v_ref[...] = jnp.dot(h, w_ref[:, 2 * seg:3 * seg], preferred_element_type=F32)


def _gconv_kernel(x_ref, nw_ref, w_ref, cw_ref, hist_ref, yc_ref, cst_ref, ue_scr,
                  *, bm, tps, stride):
    m = pl.program_id(0)
    seg = yc_ref.shape[1]
    hs = CONV_HIST * stride
    base = _halo_base(hs)
    h = _rmsnorm(x_ref[...], nw_ref[...]).astype(BF16)
    u = (jnp.dot(h, w_ref[:, seg:2 * seg], preferred_element_type=F32)
         * jnp.dot(h, w_ref[:, 2 * seg:3 * seg], preferred_element_type=F32))
    first = (m % tps) == 0

    @pl.when(first)
    def _():
        ue_scr[base - hs:base, :] = hist_ref[...]

    @pl.when(jnp.logical_not(first))
    def _():
        ue_scr[base - hs:base, :] = ue_scr[base + bm - hs:base + bm, :]

    ue_scr[base:base + bm, :] = u
    cw = cw_ref[...]
    conv = u * cw[CONV_K - 1:CONV_K, :]
    for j in range(CONV_K - 1):
        off = base - (CONV_K - 1 - j) * stride
        conv = conv + ue_scr[off:off + bm, :] * cw[j:j + 1, :]
    gate_b = jnp.dot(h, w_ref[:, 0:seg], preferred_element_type=F32)
    yc_ref[...] = (gate_b * conv).astype(BF16)
    cst_ref[...] = ue_scr[base + bm - hs:base + bm, :]


def _inproj(x, nw, w_in, cos, sin, conv_w, hist, *, layer, bm, tps, stride, q_scale,
            round_qkv=(), round_gconv=(), round_layer=0):
    M, D = x.shape
    seg = conv_w.shape[1]
    assert w_in.shape[2] == 6 * seg and M % bm == 0 and (M // bm) % tps == 0
    n_seq = hist.shape[0]
    hs = CONV_HIST * stride
    assert hist.shape == (n_seq, hs, seg) and bm >= hs
    base = _halo_base(hs)
    params = pltpu.CompilerParams(dimension_semantics=("arbitrary",), vmem_limit_bytes=VMEM_LIMIT)
    rows = lambda width: pl.BlockSpec((bm, width), lambda m: (m, 0))
    half = lambda n: pl.BlockSpec((None, D, 3 * seg), lambda m: (layer, 0, n),
                                  pipeline_mode=pl.Buffered(1))
    norm = pl.BlockSpec((1, D), lambda m: (0, 0))
    table = pl.BlockSpec((bm, HEAD_DIM), lambda m: (m % tps, 0))
    n_steps = M // bm
    src, dst, shp = _rounding_specs(round_qkv, round_layer, n_steps, lambda m: m)
    q, k, v, *rounded_a = pl.pallas_call(
        _with_rounding(functools.partial(_qkv_kernel, n_heads=seg // HEAD_DIM, q_scale=q_scale),
                       5, 3, len(round_qkv)),
        grid=(n_steps,),
        in_specs=[rows(D), norm, half(0), table, table] + src,
        out_specs=[rows(seg), rows(seg), rows(seg)] + dst,
        out_shape=[
            jax.ShapeDtypeStruct((M, seg), BF16),
            jax.ShapeDtypeStruct((M, seg), F32),
            jax.ShapeDtypeStruct((M, seg), F32),
        ] + shp,
        compiler_params=params,
        name="inproj_qkv",
    )(x, nw, w_in, cos, sin, *round_qkv)
    state = pl.BlockSpec((None, hs, seg), lambda m: (m // tps, 0, 0))
    src, dst, shp = _rounding_specs(round_gconv, round_layer, n_steps, lambda m: m)
    yc, cst, *rounded_b = pl.pallas_call(
        _with_rounding(functools.partial(_gconv_kernel, bm=bm, tps=tps, stride=stride),
                       5, 2, len(round_gconv)),
        grid=(n_steps,),
        in_specs=[rows(D), norm, half(1), pl.BlockSpec((CONV_K, seg), lambda m: (0, 0)), state]
                 + src,
        out_specs=[rows(seg), state] + dst,
        out_shape=[
            jax.ShapeDtypeStruct((M, seg), BF16),
            jax.ShapeDtypeStruct((n_seq, hs, seg), F32),
        ] + shp,
        scratch_shapes=[pltpu.VMEM((base + bm, seg), F32)],
        compiler_params=params,
        name="inproj_gconv",
    )(x, nw, w_in, conv_w, hist, *round_gconv)
    return q, k, v, yc, cst, rounded_a + rounded_b


def _group_reduce(x, op):
    return op(x.reshape(x.shape[0] // SUBLANES, SUBLANES, x.shape[1]), axis=0)


def _moba_kernel(*refs, nq, plans):
    n = len(plans)
    q_ref, k_ref, v_ref = refs[:3]
    o_ref = refs[3 + n]
    kb_scr, vt_scr, ks_scr, qt_scr, sel_scr, s_scr = refs[4 + 2 * n:]
    _cast_step(refs[3:3 + n], refs[4 + n:4 + 2 * n])
    _moba_body(q_ref, k_ref, v_ref, o_ref, kb_scr, vt_scr, ks_scr, qt_scr, sel_scr, s_scr, nq=nq)


def _moba_body(q_ref, k_ref, v_ref, o_ref, kb_scr, vt_scr, ks_scr, qt_scr, sel_scr, s_scr,
               *, nq):
    i = pl.program_id(2)
    blk = MOBA_BLOCK

    @pl.when(i == 0)
    def _():
        for j in range(nq):
            kf = k_ref[j * blk:(j + 1) * blk, :]
            kb_scr[j] = kf.astype(BF16)
            vt_scr[j] = v_ref[j * blk:(j + 1) * blk, :].T.astype(BF16)
            ks_scr[j:j + 1, :] = jnp.sum(kf, axis=0, keepdims=True)
        ks = ks_scr[...]
        ks_hi = ks.astype(BF16)
        ks_lo = (ks - ks_hi.astype(F32)).astype(BF16)
        blk_id = lax.broadcasted_iota(jnp.int32, (nq, blk), 0)
        for own in range(nq):
            qt = q_ref[own * blk:(own + 1) * blk, :].astype(F32).T.astype(BF16)
            qt_scr[own] = qt
            gate = (jnp.dot(ks_hi, qt, preferred_element_type=F32)
                    + jnp.dot(ks_lo, qt, preferred_element_type=F32))
            g = jnp.where(blk_id < own, gate, -jnp.inf)
            sel = jnp.zeros((nq, blk), jnp.bool_)
            for _ in range(min(MOBA_TOPK, own)):
                top = jnp.max(g, axis=0, keepdims=True)
                first = jnp.min(jnp.where(g == top, blk_id, nq), axis=0, keepdims=True)
                hit = blk_id == first
                sel = sel | hit
                g = jnp.where(hit, -jnp.inf, g)
            sel_scr[own] = jnp.where(sel, 1.0, 0.0)

    for pair in range(nq // 2):
        @pl.when(i == pair)
        def _(pair=pair):
            _moba_pair(pair, o_ref, kb_scr, vt_scr, qt_scr, sel_scr, s_scr, nq=nq)


def _moba_pair(pair, o_ref, kb_scr, vt_scr, qt_scr, sel_scr, s_scr, *, nq):
    blk = MOBA_BLOCK
    owns = (pair, nq - 1 - pair)

    tiles = [(w, j) for w, own in enumerate(owns) for j in range(own)]
    n_gen = len(tiles)

    mx = [None, None]
    for t, (w, j) in enumerate(tiles):
        s = jnp.dot(kb_scr[j], qt_scr[owns[w]], preferred_element_type=F32)
        s = jnp.where(sel_scr[owns[w], j:j + 1, :] > 0.0, s, NEG_INF)
        s_scr[t] = s
        cm = _group_reduce(s, jnp.max)
        mx[w] = cm if mx[w] is None else jnp.maximum(mx[w], cm)
    key_id = lax.broadcasted_iota(jnp.int32, (blk, blk), 0)
    qry_id = lax.broadcasted_iota(jnp.int32, (blk, blk), 1)
    for w, own in enumerate(owns):
        s = jnp.dot(kb_scr[own], qt_scr[own], preferred_element_type=F32)
        s = jnp.where(key_id <= qry_id, s, NEG_INF)
        s_scr[n_gen + w] = s
        cm = _group_reduce(s, jnp.max)
        mx[w] = cm if mx[w] is None else jnp.maximum(mx[w], cm)
    mrow = [jnp.max(m, axis=0, keepdims=True) for m in mx]

    acc = [None, None]
    den = [None, None]
    order = list(enumerate(tiles)) + [(n_gen + w, (w, own)) for w, own in enumerate(owns)]
    for t, (w, j) in order:
        p = jnp.exp2(s_scr[t] - mrow[w])
        cs = _group_reduce(p, jnp.sum)
        den[w] = cs if den[w] is None else den[w] + cs
        r = jnp.dot(vt_scr[j], p.astype(BF16), preferred_element_type=F32)
        acc[w] = r if acc[w] is None else acc[w] + r
    for w, own in enumerate(owns):
        tot = jnp.sum(den[w], axis=0, keepdims=True)
        o_ref[own * blk:(own + 1) * blk, :] = (acc[w] / tot).T.astype(BF16)


def _moba_prompt(q, k, v, *, n_batch, seq, n_heads, round_weights=(), round_layer=0):
    blk = MOBA_BLOCK
    assert seq % (2 * blk) == 0
    nq = seq // blk
    width = n_heads * HEAD_DIM
    grid = (n_batch, n_heads, nq // 2)
    plans = [_CastPlan(w, round_layer, grid[0] * grid[1] * grid[2]) for w in round_weights]
    cast_specs = [p.specs(lambda b, h, i: (b * grid[1] + h) * grid[2] + i) for p in plans]
    col = pl.BlockSpec((None, seq, HEAD_DIM), lambda b, h, i: (b, 0, h))
    out, *rounded = pl.pallas_call(
        functools.partial(_moba_kernel, nq=nq, plans=plans),
        grid=grid,
        in_specs=[col, col, col] + [s for s, _ in cast_specs],
        out_specs=[col] + [d for _, d in cast_specs],
        out_shape=[jax.ShapeDtypeStruct((n_batch, seq, width), BF16)]
                  + [jax.ShapeDtypeStruct(p.shape, BF16) for p in plans],
        scratch_shapes=[
            pltpu.VMEM((nq, blk, HEAD_DIM), BF16),
            pltpu.VMEM((nq, HEAD_DIM, blk), BF16),
            pltpu.VMEM((nq, HEAD_DIM), F32),
            pltpu.VMEM((nq, HEAD_DIM, blk), BF16),
            pltpu.VMEM((nq, nq, blk), F32),
            pltpu.VMEM((nq + 1, blk, blk), F32),
        ],
        compiler_params=pltpu.CompilerParams(
            dimension_semantics=("arbitrary", "arbitrary", "arbitrary"),
            vmem_limit_bytes=VMEM_LIMIT),
        name="moba_prompt",
    )(q.reshape(n_batch, seq, width), k.reshape(n_batch, seq, width),
      v.reshape(n_batch, seq, width), *round_weights)
    return out.reshape(n_batch * seq, width), rounded


def _outproj_kernel(a_ref, c_ref, wt_ref, wb_ref, x_ref, o_ref):
    o_ref[...] = (x_ref[...]
                  + jnp.dot(a_ref[...], wt_ref[...], preferred_element_type=F32)
                  + jnp.dot(c_ref[...], wb_ref[...], preferred_element_type=F32))


def _outproj(a, c, w_o, x, *, layer, bm, bn, round_weights=(), round_layer=0):
    M, D = x.shape
    half = a.shape[1]
    assert w_o.shape[1:] == (2 * half, D) and c.shape == a.shape and D % bn == 0 and M % bm == 0
    w_mode = pl.Buffered(1) if bn == D else None
    grid = (M // bm, D // bn)
    src, dst, shp = _rounding_specs(round_weights, round_layer, grid[0] * grid[1],
                                    lambda m, n: m * grid[1] + n)
    out, *rounded = pl.pallas_call(
        _with_rounding(_outproj_kernel, 5, 1, len(round_weights)),
        grid=grid,
        in_specs=[
            pl.BlockSpec((bm, half), lambda m, n: (m, 0)),
            pl.BlockSpec((bm, half), lambda m, n: (m, 0)),
            pl.BlockSpec((None, half, bn), lambda m, n: (layer, 0, n), pipeline_mode=w_mode),
            pl.BlockSpec((None, half, bn), lambda m, n: (layer, 1, n), pipeline_mode=w_mode),
            pl.BlockSpec((bm, bn), lambda m, n: (m, n)),
        ] + src,
        out_specs=[pl.BlockSpec((bm, bn), lambda m, n: (m, n))] + dst,
        out_shape=[jax.ShapeDtypeStruct((M, D), F32)] + shp,
        compiler_params=pltpu.CompilerParams(
            dimension_semantics=("arbitrary", "arbitrary"), vmem_limit_bytes=VMEM_LIMIT),
        name="outproj",
    )(a, c, w_o, w_o, x, *round_weights)
    return out, rounded


def _ffn_kernel(*refs, final_norm, plans):
    n = len(plans)
    x_ref, nw_ref, wg_ref, wu_ref, wd_ref, fw_ref = refs[:6]
    o_ref = refs[6 + n]
    h_scr = refs[7 + 2 * n]
    f = pl.program_id(1)

    @pl.when(f == 0)
    def _():
        x = x_ref[...]
        h_scr[...] = _rmsnorm(x, nw_ref[...]).astype(BF16)
        o_ref[...] = x

    _cast_step(refs[6:6 + n], refs[7 + n:7 + 2 * n])
    h = h_scr[...]
    g = jnp.dot(h, wg_ref[...], preferred_element_type=F32)
    u = jnp.dot(h, wu_ref[...], preferred_element_type=F32)
    a = (g * jax.nn.sigmoid(g) * u).astype(BF16)
    o_ref[...] += jnp.dot(a, wd_ref[...], preferred_element_type=F32)

    if final_norm:
        @pl.when(f == pl.num_programs(1) - 1)
        def _():
            o_ref[...] = _rmsnorm(o_ref[...], fw_ref[...])


def _ffn(x, nw, wg, wu, wd, fw, *, bm, bf, final_norm, round_weights=(), round_layer=0):
    M, D = x.shape
    FF = wg.shape[1]
    assert FF % bf == 0 and M % bm == 0
    grid = (M // bm, FF // bf)
    plans = [_CastPlan(w, round_layer, grid[0] * grid[1]) for w in round_weights]
    cast_specs = [p.specs(lambda m, f: m * grid[1] + f) for p in plans]
    out, *rounded = pl.pallas_call(
        functools.partial(_ffn_kernel, final_norm=final_norm, plans=plans),
        grid=grid,
        in_specs=[
            pl.BlockSpec((bm, D), lambda m, f: (m, 0)),
            pl.BlockSpec((1, D), lambda m, f: (0, 0)),
            pl.BlockSpec((D, bf), lambda m, f: (0, f)),
            pl.BlockSpec((D, bf), lambda m, f: (0, f)),
            pl.BlockSpec((bf, D), lambda m, f: (f, 0)),
            pl.BlockSpec((1, D), lambda m, f: (0, 0)),
        ] + [s for s, _ in cast_specs],
        out_specs=[pl.BlockSpec((bm, D), lambda m, f: (m, 0))] + [d for _, d in cast_specs],
        out_shape=[jax.ShapeDtypeStruct((M, D), F32)]
                  + [jax.ShapeDtypeStruct(p.shape, BF16) for p in plans],
        scratch_shapes=[pltpu.VMEM((bm, D), BF16)],
        compiler_params=pltpu.CompilerParams(
            dimension_semantics=("arbitrary", "arbitrary"), vmem_limit_bytes=VMEM_LIMIT),
        name="ffn",
    )(x, nw, wg, wu, wd, fw, *round_weights)
    return out, rounded


def _pool_kernel(x_ref, nw_ref, wp_ref, ps_ref, hist_ref, o_ref, pst_ref, ext_scr, band_scr,
                 *, bm, tps, stride, pos_base):
    m = pl.program_id(0)
    hs = POOL_HIST * stride
    base = _halo_base(hs)
    rb = band_scr.shape[1]
    nd = min(rb, base)
    assert bm % rb == 0
    assert nd == rb or (stride == 1 and nd % SUBLANES == 0 and nd >= max(POOL_WINDOWS))
    grp = wp_ref.shape[1]
    first = (m % tps) == 0

    if nd < rb:
        @pl.when(m == 0)
        def _():
            lag = (lax.broadcasted_iota(jnp.int32, (rb, rb), 0)
                   - lax.broadcasted_iota(jnp.int32, (rb, rb), 1))
            for gi, w in enumerate(POOL_WINDOWS):
                band_scr[gi] = jnp.where((lag >= 0) & (lag < w), 1.0, 0.0).astype(BF16)

    @pl.when(first)
    def _():
        ext_scr[base - hs:base, :] = hist_ref[...]

    @pl.when(jnp.logical_not(first))
    def _():
        ext_scr[base - hs:base, :] = ext_scr[base + bm - hs:base + bm, :]

    ext_scr[base:base + bm, :] = _rmsnorm(x_ref[...], nw_ref[...])
    for r0 in range(0, bm, rb):
        r = (m % tps) * bm + r0 + lax.broadcasted_iota(jnp.int32, (nd, 1), 0)
        pos = pos_base + (r if stride == 1 else lax.shift_right_logical(r, stride.bit_length() - 1))
        for gi, w in enumerate(POOL_WINDOWS):
            cols = slice(gi * grp, (gi + 1) * grp)
            hb = ext_scr[base + r0:base + r0 + rb, cols]
            tot = hb[:nd]
            for kk in range(1, w):
                off = base + r0 - kk * stride
                tot = tot + ext_scr[off:off + nd, cols]
            d = tot / jnp.minimum(pos + 1, w).astype(F32) - hb[:nd]
            if nd < rb:
                hi = hb.astype(BF16)
                lo = (hb - hi.astype(F32)).astype(BF16)
                band = band_scr[gi]
                full = (jnp.dot(band, hi, preferred_element_type=F32)
                        + jnp.dot(band, lo, preferred_element_type=F32))
                d = jnp.concatenate([d, full[nd:] * (1.0 / w) - hb[nd:]], axis=0)
            out = jnp.dot(d.astype(BF16), wp_ref[gi], preferred_element_type=F32)
            o_ref[r0:r0 + rb, cols] = x_ref[r0:r0 + rb, cols] + out * ps_ref[:, cols]
    pst_ref[...] = ext_scr[base + bm - hs:base + bm, :]


def _pool(x, nw, w_pool, pool_scale, hist, *, layer, bm, tps, stride, pos_base):
    M, D = x.shape
    n_seq = hist.shape[0]
    hs = POOL_HIST * stride
    assert stride & (stride - 1) == 0 and hist.shape == (n_seq, hs, D) and M % bm == 0
    base = _halo_base(hs)
    _, ng, grp, _ = w_pool.shape
    assert ng == len(POOL_WINDOWS) and ng * grp == D
    kern = functools.partial(_pool_kernel, bm=bm, tps=tps, stride=stride, pos_base=pos_base)
    return pl.pallas_call(
        kern,
        grid=(M // bm,),
        in_specs=[
            pl.BlockSpec((bm, D), lambda m: (m, 0)),
            pl.BlockSpec((1, D), lambda m: (0, 0)),
            pl.BlockSpec((None, ng, grp, grp), lambda m: (layer, 0, 0, 0)),
            pl.BlockSpec((1, D), lambda m: (0, 0)),
            pl.BlockSpec((None, hs, D), lambda m: (m // tps, 0, 0)),
        ],
        out_specs=[
            pl.BlockSpec((bm, D), lambda m: (m, 0)),
            pl.BlockSpec((None, hs, D), lambda m: (m // tps, 0, 0)),
        ],
        out_shape=[
            jax.ShapeDtypeStruct((M, D), F32),
            jax.ShapeDtypeStruct((n_seq, hs, D), F32),
        ],
        scratch_shapes=[pltpu.VMEM((base + bm, D), F32),
                        pltpu.VMEM((ng, min(bm, MXU_DIM), min(bm, MXU_DIM)), BF16)],
        compiler_params=pltpu.CompilerParams(
            dimension_semantics=("arbitrary",), vmem_limit_bytes=VMEM_LIMIT),
        name="pool",
    )(x, nw, w_pool, pool_scale, hist)


def _head_match(n_rows, n_cols, n_heads, t_pad):
    rh = lax.broadcasted_iota(jnp.int32, (n_rows, n_cols), 0) // t_pad
    ch = lax.broadcasted_iota(jnp.int32, (n_rows, n_cols), 1) % n_heads
    return rh == ch


def _fold_heads(x, n_heads, t_pad):
    out = x[0:t_pad]
    for hh in range(1, n_heads):
        out = out + x[hh * t_pad:(hh + 1) * t_pad]
    return out


def _lane_class(x, n_heads, op):
    sh = n_heads
    while sh < x.shape[-1]:
        x = op(x, pltpu.roll(x, sh, axis=x.ndim - 1))
        sh *= 2
    return x


def _lane_tiles(x, op):
    out = x[:, 0:128]
    for t in range(1, x.shape[1] // 128):
        out = op(out, x[:, t * 128:(t + 1) * 128])
    return out


def _sa_logits_kernel(pt_ref, q_ref, *refs, n_pages, n_heads, t_pad, pages_per_block):
    k_refs = refs[:n_pages]
    r_ref, g_ref = refs[n_pages:]
    qa = q_ref[...]
    rows = k_refs[0].shape[0] * n_heads
    match = _head_match(qa.shape[0], rows, n_heads, t_pad)
    for pg in range(n_pages):
        k2 = k_refs[pg][...].reshape(rows, HEAD_DIM).astype(BF16)
        l2 = lax.dot_general(qa, k2, _NT, preferred_element_type=F32)
        r_ref[pg] = _fold_heads(jnp.where(match, l2, 0.0), n_heads, t_pad)
    for bb in range(n_pages // pages_per_block):
        tot = _lane_tiles(r_ref[bb * pages_per_block], jnp.add)
        for pg in range(1, pages_per_block):
            tot = tot + _lane_tiles(r_ref[bb * pages_per_block + pg], jnp.add)
        g_ref[bb] = _lane_class(tot, n_heads, jnp.add)


def _sa_values_kernel(pt_ref, r_ref, g_ref, q_ref, kn_ref, vn_ref, *refs,
                      n_pages, n_heads, t_pad, pages_per_block):
    v_refs = refs[:n_pages]
    o_ref, p_scr, sel_scr, l_scr, acc_scr = refs[n_pages:]
    c = pl.program_id(1)
    n_blocks = g_ref.shape[0]
    rows = v_refs[0].shape[0] * n_heads
    n_rows = q_ref.shape[0]
    match = _head_match(n_rows, rows, n_heads, t_pad)

    def spread(p):
        return jnp.where(match[:, :p.shape[1]], jnp.concatenate([p] * n_heads, axis=0), 0.0).astype(BF16)

    @pl.when(c == 0)
    def _():
        gates = g_ref[...]
        idx = lax.broadcasted_iota(jnp.int32, gates.shape, 0)
        sel = jnp.zeros(gates.shape, F32)
        for _ in range(MOBA_TOPK):
            mx = jnp.max(gates, axis=0, keepdims=True)
            first = jnp.min(jnp.where(gates == mx, idx, n_blocks), axis=0, keepdims=True)
            hit = idx == first
            sel = jnp.where(hit, 1.0, sel)
            gates = jnp.where(hit, -jnp.inf, gates)
        sel_scr[...] = sel

        own_w = kn_ref.shape[0]
        lo = lax.dot_general(q_ref[...], kn_ref[...].astype(BF16), _NT, preferred_element_type=F32)
        ro = _fold_heads(jnp.where(match[:, :own_w], lo, 0.0), n_heads, t_pad)
        t_key = lax.broadcasted_iota(jnp.int32, ro.shape, 1) // n_heads
        t_qry = lax.broadcasted_iota(jnp.int32, ro.shape, 0)
        ro = jnp.where(t_key <= t_qry, ro, NEG_INF)

        def max_body(b, mx):
            here = _lane_tiles(r_ref[b * pages_per_block], jnp.maximum)
            for pg in range(1, pages_per_block):
                here = jnp.maximum(here, _lane_tiles(r_ref[b * pages_per_block + pg], jnp.maximum))
            return jnp.maximum(mx, jnp.where(sel_scr[b] > 0.0, here, NEG_INF))

        mx = lax.fori_loop(0, n_blocks, max_body, ro)
        mx = _lane_class(mx, n_heads, jnp.maximum)
        width = r_ref.shape[2]
        mx_w = jnp.concatenate([mx] * (width // 128), axis=1)

        def exp_body(b, tot):
            keep = jnp.concatenate([sel_scr[b]] * (width // 128), axis=1) > 0.0
            for pg in range(pages_per_block):
                p = jnp.where(keep, jnp.exp(r_ref[b * pages_per_block + pg] - mx_w), 0.0)
                p_scr[b * pages_per_block + pg] = p
                tot = tot + _lane_tiles(p, jnp.add)
            return tot

        po = jnp.exp(ro - mx)
        tot = lax.fori_loop(0, n_blocks, exp_body, po)
        l_scr[...] = _lane_class(tot, n_heads, jnp.add)
        acc_scr[...] = jnp.dot(spread(po), vn_ref[...].astype(BF16), preferred_element_type=F32)

    acc = acc_scr[...]
    for pg in range(n_pages):
        v2 = v_refs[pg][...].reshape(rows, HEAD_DIM).astype(BF16)
        acc = acc + jnp.dot(spread(p_scr[c * n_pages + pg]), v2, preferred_element_type=F32)
    acc_scr[...] = acc

    @pl.when(c == pl.num_programs(1) - 1)
    def _():
        l = l_scr[...]
        den = jnp.concatenate([l[:, hh:hh + 1] for hh in range(n_heads)], axis=0)
        o_ref[...] = acc_scr[...] / den


def _sample_attention(q_all, k_new, v_new, cache_k, cache_v, page_table, *, layer, n_heads, t_pad):
    n_dec, n_pt = page_table.shape
    page = cache_k.shape[2]
    ppb = MOBA_BLOCK // page
    P = PAGES_PER_STEP
    assert MOBA_BLOCK % page == 0 and n_pt % P == 0 and P % ppb == 0
    n_steps = n_pt // P
    n_blocks = n_pt // ppb
    width = page * n_heads
    n_rows = n_heads * t_pad

    def page_spec(i):
        return pl.BlockSpec((None, None, page, n_heads, HEAD_DIM),
                            lambda b, c, pt: (layer, pt[b, c * P + i], 0, 0, 0))

    params = pltpu.CompilerParams(dimension_semantics=("arbitrary", "arbitrary"),
                                  vmem_limit_bytes=VMEM_LIMIT)
    common = dict(n_pages=P, n_heads=n_heads, t_pad=t_pad, pages_per_block=ppb)
    logits, gates = pl.pallas_call(
        functools.partial(_sa_logits_kernel, **common),
        grid_spec=pltpu.PrefetchScalarGridSpec(
            num_scalar_prefetch=1,
            grid=(n_dec, n_steps),
            in_specs=[pl.BlockSpec((None, n_rows, HEAD_DIM), lambda b, c, pt: (b, 0, 0))]
                     + [page_spec(i) for i in range(P)],
            out_specs=[
                pl.BlockSpec((None, P, t_pad, width), lambda b, c, pt: (b, c, 0, 0)),
                pl.BlockSpec((None, P // ppb, t_pad, 128), lambda b, c, pt: (b, c, 0, 0)),
            ],
        ),
        out_shape=[
            jax.ShapeDtypeStruct((n_dec, n_pt, t_pad, width), F32),
            jax.ShapeDtypeStruct((n_dec, n_blocks, t_pad, 128), F32),
        ],
        compiler_params=params,
        name="sample_logits",
    )(page_table, q_all, *([cache_k] * P))

    own = k_new.shape[1]
    return pl.pallas_call(
        functools.partial(_sa_values_kernel, **common),
        grid_spec=pltpu.PrefetchScalarGridSpec(
            num_scalar_prefetch=1,
            grid=(n_dec, n_steps),
            in_specs=[
                pl.BlockSpec((None, n_pt, t_pad, width), lambda b, c, pt: (b, 0, 0, 0)),
                pl.BlockSpec((None, n_blocks, t_pad, 128), lambda b, c, pt: (b, 0, 0, 0)),
                pl.BlockSpec((None, n_rows, HEAD_DIM), lambda b, c, pt: (b, 0, 0)),
                pl.BlockSpec((None, own, HEAD_DIM), lambda b, c, pt: (b, 0, 0)),
                pl.BlockSpec((None, own, HEAD_DIM), lambda b, c, pt: (b, 0, 0)),
            ] + [page_spec(i) for i in range(P)],
            out_specs=pl.BlockSpec((None, n_rows, HEAD_DIM), lambda b, c, pt: (b, 0, 0)),
            scratch_shapes=[
                pltpu.VMEM((n_pt, t_pad, width), F32),
                pltpu.VMEM((n_blocks, t_pad, 128), F32),
                pltpu.VMEM((t_pad, 128), F32),
                pltpu.VMEM((n_rows, HEAD_DIM), F32),
            ],
        ),
        out_shape=jax.ShapeDtypeStruct((n_dec, n_rows, HEAD_DIM), F32),
        compiler_params=params,
        name="sample_values",
    )(page_table, logits, gates, q_all, k_new, v_new, *([cache_v] * P))


def _rope_tables(pos):
    half = HEAD_DIM // 2
    inv = ROPE_THETA ** (-jnp.arange(half, dtype=F32) / half)
    ang = pos.astype(F32)[:, None] * inv[None, :]
    cos, sin = jnp.cos(ang), jnp.sin(ang)
    return jnp.concatenate([cos, cos], axis=1), jnp.concatenate([-sin, sin], axis=1)


def kernel(x_prompt, x_sample, cache_k, cache_v, page_table, state_conv, state_pool,
           norm_mix, norm_ffn, norm_final, w_in, conv_w, w_o, w_pool, pool_scale,
           w_gate, w_up, w_down):
    n_batch, n_seq, d = x_prompt.shape
    n_dec, n_new, _ = x_sample.shape
    depth = norm_mix.shape[0]
    seg = conv_w.shape[2]
    n_heads = seg // HEAD_DIM
    page = cache_k.shape[2]
    past_len = page_table.shape[1] * page
    assert n_dec == SUBLANES and n_new <= SUBLANES and past_len % MOBA_BLOCK == 0
    assert cache_k.shape[3] == n_heads and cache_k.shape[4] == HEAD_DIM
    q_scale = HEAD_DIM ** -0.5
    q_scale_log2 = q_scale * LOG2_E
    t_pad = SUBLANES
    own_keys = 128 // n_heads
    assert n_new <= own_keys

    w_in_b, w_o_b, w_pool_b = w_in.astype(BF16), w_o.astype(BF16), w_pool.astype(BF16)
    ffn_f32 = (w_gate, w_up, w_down)
    ffn_bf16 = {}
    row = lambda a: a.reshape(1, -1)

    xp = x_prompt.reshape(n_batch * n_seq, d)
    xs = x_sample.transpose(1, 0, 2).reshape(n_new * n_dec, d)
    ms = n_new * n_dec
    bm = 512
    tps = n_seq // bm
    cos_p, sin_p = _rope_tables(jnp.arange(n_seq, dtype=jnp.int32))
    pos_s = past_len + jnp.repeat(jnp.arange(n_new, dtype=jnp.int32), n_dec)
    cos_s, sin_s = _rope_tables(pos_s)

    def tm(a):
        return a.transpose(1, 0, 2).reshape(1, a.shape[1] * n_dec, a.shape[2])

    def untm(a, rows):
        return a.reshape(rows, n_dec, a.shape[-1]).transpose(1, 0, 2)

    k_p, v_p, k_s, v_s, conv_p, conv_s, pool_p, pool_s = [], [], [], [], [], [], [], []
    for layer in range(depth):
        if layer % 2 == 0:
            e = layer // 2
            ahead = layer + 1 < depth
            q, k, v, yc, cst, r_in = _inproj(
                xp, row(norm_mix[layer]), w_in_b, cos_p, sin_p, conv_w[e],
                jnp.zeros((n_batch, CONV_HIST, seg), F32),
                layer=e, bm=bm, tps=tps, stride=1, q_scale=q_scale_log2,
                round_qkv=(w_gate,) if ahead else (), round_gconv=(w_up,) if ahead else (),
                round_layer=layer + 1)
            attn, ffn_bf16[layer] = _moba_prompt(
                q, k, v, n_batch=n_batch, seq=n_seq, n_heads=n_heads,
                round_weights=ffn_f32, round_layer=layer)
            xp, r_out = _outproj(attn, yc, w_o_b, xp, layer=e, bm=bm, bn=d,
                                 round_weights=(w_down,) if ahead else (), round_layer=layer + 1)
            if ahead:
                ffn_bf16[layer + 1] = r_in + r_out
            k_p.append(k.reshape(n_batch, n_seq, n_heads, HEAD_DIM))
            v_p.append(v.reshape(n_batch, n_seq, n_heads, HEAD_DIM))
            conv_p.append(cst)
            q, k, v, yc, cst, _ = _inproj(
                xs, row(norm_mix[layer]), w_in_b, cos_s, sin_s, conv_w[e], tm(state_conv[e]),
                layer=e, bm=ms, tps=1, stride=n_dec, q_scale=q_scale)
            heads = lambda a: a.reshape(n_new, n_dec, n_heads, HEAD_DIM)
            q_all = jnp.pad(heads(q).transpose(1, 2, 0, 3), ((0, 0), (0, 0), (0, t_pad - n_new), (0, 0)))
            q_all = q_all.reshape(n_dec, n_heads * t_pad, HEAD_DIM)
            k_new = heads(k).transpose(1, 0, 2, 3)
            v_new = heads(v).transpose(1, 0, 2, 3)
            own = lambda a: jnp.pad(a, ((0, 0), (0, own_keys - n_new), (0, 0), (0, 0))).reshape(
                n_dec, own_keys * n_heads, HEAD_DIM)
            o = _sample_attention(q_all, own(k_new), own(v_new), cache_k, cache_v, page_table,
                                  layer=e, n_heads=n_heads, t_pad=t_pad)
            o = o.reshape(n_dec, n_heads, t_pad, HEAD_DIM)[:, :, :n_new]
            attn = o.transpose(2, 0, 1, 3).reshape(ms, seg).astype(BF16)
            xs, _ = _outproj(attn, yc, w_o_b, xs, layer=e, bm=ms, bn=d)
            k_s.append(k_new)
            v_s.append(v_new)
            conv_s.append(untm(cst[0], CONV_HIST))
        else:
            o_ = layer // 2
            xp, pst = _pool(xp, row(norm_mix[layer]), w_pool_b, row(pool_scale[o_]),
                            jnp.zeros((n_batch, POOL_HIST, d), F32),
                            layer=o_, bm=bm, tps=tps, stride=1, pos_base=0)
            pool_p.append(pst)
            xs, pst = _pool(xs, row(norm_mix[layer]), w_pool_b, row(pool_scale[o_]),
                            tm(state_pool[o_]), layer=o_, bm=ms, tps=1, stride=n_dec,
                            pos_base=past_len)
            pool_s.append(untm(pst[0], POOL_HIST))
        last = layer == depth - 1
        wg, wu, wd = ffn_bf16[layer]
        ffn = functools.partial(_ffn, nw=row(norm_ffn[layer]), wg=wg, wu=wu, wd=wd,
                                fw=row(norm_final), bf=512, final_norm=last)
        xp, _ = ffn(xp, bm=2 * bm)
        xs, _ = ffn(xs, bm=ms)

    y_prompt = xp.reshape(n_batch, n_seq, d)
    y_sample = xs.reshape(n_new, n_dec, d).transpose(1, 0, 2)
    return (y_prompt, y_sample, jnp.stack(k_p), jnp.stack(v_p), jnp.stack(k_s), jnp.stack(v_s),
            jnp.stack(conv_p), jnp.stack(conv_s), jnp.stack(pool_p), jnp.stack(pool_s))
```

```python
import functools

import jax
import jax.numpy as jnp
from jax import lax
from jax.experimental import pallas as pl
from jax.experimental.pallas import tpu as pltpu

F32 = jnp.float32
BF16 = jnp.bfloat16

HEAD_DIM = 128
MOBA_BLOCK = 256
MOBA_TOPK = 3
CONV_K = 3
CONV_HIST = CONV_K - 1
ROPE_THETA = 10000.0
POOL_WINDOWS = (2, 4, 8, 16)
POOL_HIST = max(POOL_WINDOWS) - 1
RMS_EPS = 1e-6
NEG_INF = -1e30
LOG2_E = 1.4426950408889634

SUBLANES = 8
MXU_DIM = 256
VMEM_LIMIT = 56 * 1024 * 1024
PAGES_PER_STEP = 32

_NT = (((1,), (1,)), ((), ()))


def _rmsnorm(x, g):
    ms = jnp.mean(x * x, axis=-1, keepdims=True)
    return x * lax.rsqrt(ms + RMS_EPS) * g


def _halo_base(rows):
    return -(-rows // SUBLANES) * SUBLANES


class _CastPlan:
    def __init__(self, stacked, layer, n_steps):
        _, rows, cols = stacked.shape
        bf16_rows = 16
        assert cols % 128 == 0 and rows % bf16_rows == 0
        self.count = max(d for d in range(1, min(n_steps, rows // bf16_rows) + 1)
                         if (rows // bf16_rows) % d == 0)
        self.block = (rows // self.count, cols)
        self.layer = layer
        self.shape = (rows, cols)

    def _index(self, step):
        return jnp.minimum(step, self.count - 1), 0

    def specs(self, step_of, src_mode=None):
        def src(*g):
            return (self.layer,) + self._index(step_of(*g))
        def dst(*g):
            return self._index(step_of(*g))
        return (pl.BlockSpec((None,) + self.block, src, pipeline_mode=src_mode),
                pl.BlockSpec(self.block, dst))


def _cast_step(src_refs, dst_refs):
    for src, dst in zip(src_refs, dst_refs):
        dst[...] = src[...].astype(BF16)


def _with_rounding(body, n_in, n_out, n_cast):
    def kernel(*refs, **kw):
        a, b, c = n_in + n_cast, n_in + n_cast + n_out, n_in + 2 * n_cast + n_out
        _cast_step(refs[n_in:a], refs[b:c])
        return body(*refs[:n_in], *refs[a:b], *refs[c:], **kw)
    return kernel


def _rounding_specs(weights, layer, n_steps, step_of):
    plans = [_CastPlan(w, layer, n_steps) for w in weights]
    specs = [p.specs(step_of) for p in plans]
    return ([s for s, _ in specs], [d for _, d in specs],
            [jax.ShapeDtypeStruct(p.shape, BF16) for p in plans])


def _qkv_kernel(x_ref, nw_ref, w_ref, cos_ref, sin_ref, q_ref, k_ref, v_ref, *, n_heads, q_scale):
    seg = n_heads * HEAD_DIM
    h = _rmsnorm(x_ref[...], nw_ref[...]).astype(BF16)
    cos = cos_ref[...]
    sin = sin_ref[...]

    def rope(a, hh):
        part = a[:, hh * HEAD_DIM:(hh + 1) * HEAD_DIM]
        return part * cos + pltpu.roll(part, HEAD_DIM // 2, axis=1) * sin

    aq = jnp.dot(h, w_ref[:, 0:seg], preferred_element_type=F32)
    for hh in range(n_heads):
        q_ref[:, hh * HEAD_DIM:(hh + 1) * HEAD_DIM] = (rope(aq, hh) * q_scale).astype(BF16)
    ak = jnp.dot(h, w_ref[:, seg:2 * seg], preferred_element_type=F32)
    for hh in range(n_heads):
        k_ref[:, hh * HEAD_DIM:(hh + 1) * HEAD_DIM] = rope(ak, hh)
    v_ref[...] = jnp.dot(h, w_ref[:, 2 * seg:3 * seg], preferred_element_type=F32)


def _gconv_kernel(x_ref, nw_ref, w_ref, cw_ref, hist_ref, yc_ref, cst_ref, ue_scr,
                  *, bm, tps, stride):
    m = pl.program_id(0)
    seg = yc_ref.shape[1]
    hs = CONV_HIST * stride
    base = _halo_base(hs)
    h = _rmsnorm(x_ref[...], nw_ref[...]).astype(BF16)
    u = (jnp.dot(h, w_ref[:, seg:2 * seg], preferred_element_type=F32)
         * jnp.dot(h, w_ref[:, 2 * seg:3 * seg], preferred_element_type=F32))
    first = (m % tps) == 0

    @pl.when(first)
    def _():
        ue_scr[base - hs:base, :] = hist_ref[...]

    @pl.when(jnp.logical_not(first))
    def _():
        ue_scr[base - hs:base, :] = ue_scr[base + bm - hs:base + bm, :]

    ue_scr[base:base + bm, :] = u
    cw = cw_ref[...]
    conv = u * cw[CONV_K - 1:CONV_K, :]
    for j in range(CONV_K - 1):
        off = base - (CONV_K - 1 - j) * stride
        conv = conv + ue_scr[off:off + bm, :] * cw[j:j + 1, :]
    gate_b = jnp.dot(h, w_ref[:, 0:seg], preferred_element_type=F32)
    yc_ref[...] = (gate_b * conv).astype(BF16)
    cst_ref[...] = ue_scr[base + bm - hs:base + bm, :]


def _inproj(x, nw, w_in, cos, sin, conv_w, hist, *, layer, bm, tps, stride, q_scale,
            round_qkv=(), round_gconv=(), round_layer=0):
    M, D = x.shape
    seg = conv_w.shape[1]
    assert w_in.shape[2] == 6 * seg and M % bm == 0 and (M // bm) % tps == 0
    n_seq = hist.shape[0]
    hs = CONV_HIST * stride
    assert hist.shape == (n_seq, hs, seg) and bm >= hs
    base = _halo_base(hs)
    params = pltpu.CompilerParams(dimension_semantics=("arbitrary",), vmem_limit_bytes=VMEM_LIMIT)
    rows = lambda width: pl.BlockSpec((bm, width), lambda m: (m, 0))
    half = lambda n: pl.BlockSpec((None, D, 3 * seg), lambda m: (layer, 0, n),
                                  pipeline_mode=pl.Buffered(1))
    norm = pl.BlockSpec((1, D), lambda m: (0, 0))
    table = pl.BlockSpec((bm, HEAD_DIM), lambda m: (m % tps, 0))
    n_steps = M // bm
    src, dst, shp = _rounding_specs(round_qkv, round_layer, n_steps, lambda m: m)
    q, k, v, *rounded_a = pl.pallas_call(
        _with_rounding(functools.partial(_qkv_kernel, n_heads=seg // HEAD_DIM, q_scale=q_scale),
                       5, 3, len(round_qkv)),
        grid=(n_steps,),
        in_specs=[rows(D), norm, half(0), table, table] + src,
        out_specs=[rows(seg), rows(seg), rows(seg)] + dst,
        out_shape=[
            jax.ShapeDtypeStruct((M, seg), BF16),
            jax.ShapeDtypeStruct((M, seg), F32),
            jax.ShapeDtypeStruct((M, seg), F32),
        ] + shp,
        compiler_params=params,
        name="inproj_qkv",
    )(x, nw, w_in, cos, sin, *round_qkv)
    state = pl.BlockSpec((None, hs, seg), lambda m: (m // tps, 0, 0))
    src, dst, shp = _rounding_specs(round_gconv, round_layer, n_steps, lambda m: m)
    yc, cst, *rounded_b = pl.pallas_call(
        _with_rounding(functools.partial(_gconv_kernel, bm=bm, tps=tps, stride=stride),
                       5, 2, len(round_gconv)),
        grid=(n_steps,),
        in_specs=[rows(D), norm, half(1), pl.BlockSpec((CONV_K, seg), lambda m: (0, 0)), state]
                 + src,
        out_specs=[rows(seg), state] + dst,
        out_shape=[
            jax.ShapeDtypeStruct((M, seg), BF16),
            jax.ShapeDtypeStruct((n_seq, hs, seg), F32),
        ] + shp,
        scratch_shapes=[pltpu.VMEM((base + bm, seg), F32)],
        compiler_params=params,
        name="inproj_gconv",
    )(x, nw, w_in, conv_w, hist, *round_gconv)
    return q, k, v, yc, cst, rounded_a + rounded_b


def _group_reduce(x, op):
    return op(x.reshape(x.shape[0] // SUBLANES, SUBLANES, x.shape[1]), axis=0)


def _moba_kernel(*refs, nq, plans):
    n = len(plans)
    q_ref, k_ref, v_ref = refs[:3]
    o_ref = refs[3 + n]
    kb_scr, vt_scr, ks_scr, qt_scr, sel_scr, s_scr = refs[4 + 2 * n:]
    _cast_step(refs[3:3 + n], refs[4 + n:4 + 2 * n])
    _moba_body(q_ref, k_ref, v_ref, o_ref, kb_scr, vt_scr, ks_scr, qt_scr, sel_scr, s_scr, nq=nq)


def _moba_body(q_ref, k_ref, v_ref, o_ref, kb_scr, vt_scr, ks_scr, qt_scr, sel_scr, s_scr,
               *, nq):
    i = pl.program_id(2)
    blk = MOBA_BLOCK

    @pl.when(i == 0)
    def _():
        for j in range(nq):
            kf = k_ref[j * blk:(j + 1) * blk, :]
            kb_scr[j] = kf.astype(BF16)
            vt_scr[j] = v_ref[j * blk:(j + 1) * blk, :].T.astype(BF16)
            ks_scr[j:j + 1, :] = jnp.sum(kf, axis=0, keepdims=True)
        ks = ks_scr[...]
        ks_hi = ks.astype(BF16)
        ks_lo = (ks - ks_hi.astype(F32)).astype(BF16)
        blk_id = lax.broadcasted_iota(jnp.int32, (nq, blk), 0)
        for own in range(nq):
            qt = q_ref[own * blk:(own + 1) * blk, :].astype(F32).T.astype(BF16)
            qt_scr[own] = qt
            gate = (jnp.dot(ks_hi, qt, preferred_element_type=F32)
                    + jnp.dot(ks_lo, qt, preferred_element_type=F32))
            g = jnp.where(blk_id < own, gate, -jnp.inf)
            sel = jnp.zeros((nq, blk), jnp.bool_)
            for _ in range(min(MOBA_TOPK, own)):
                top = jnp.max(g, axis=0, keepdims=True)
                first = jnp.min(jnp.where(g == top, blk_id, nq), axis=0, keepdims=True)
                hit = blk_id == first
                sel = sel | hit
                g = jnp.where(hit, -jnp.inf, g)
            sel_scr[own] = jnp.where(sel, 1.0, 0.0)

    for pair in range(nq // 2):
        @pl.when(i == pair)
        def _(pair=pair):
            _moba_pair(pair, o_ref, kb_scr, vt_scr, qt_scr, sel_scr, s_scr, nq=nq)


def _moba_pair(pair, o_ref, kb_scr, vt_scr, qt_scr, sel_scr, s_scr, *, nq):
    blk = MOBA_BLOCK
    owns = (pair, nq - 1 - pair)

    tiles = [(w, j) for w, own in enumerate(owns) for j in range(own)]
    n_gen = len(tiles)

    mx = [None, None]
    for t, (w, j) in enumerate(tiles):
        s = jnp.dot(kb_scr[j], qt_scr[owns[w]], preferred_element_type=F32)
        s = jnp.where(sel_scr[owns[w], j:j + 1, :] > 0.0, s, NEG_INF)
        s_scr[t] = s
        cm = _group_reduce(s, jnp.max)
        mx[w] = cm if mx[w] is None else jnp.maximum(mx[w], cm)
    key_id = lax.broadcasted_iota(jnp.int32, (blk, blk), 0)
    qry_id = lax.broadcasted_iota(jnp.int32, (blk, blk), 1)
    for w, own in enumerate(owns):
        s = jnp.dot(kb_scr[own], qt_scr[own], preferred_element_type=F32)
        s = jnp.where(key_id <= qry_id, s, NEG_INF)
        s_scr[n_gen + w] = s
        cm = _group_reduce(s, jnp.max)
        mx[w] = cm if mx[w] is None else jnp.maximum(mx[w], cm)
    mrow = [jnp.max(m, axis=0, keepdims=True) for m in mx]

    acc = [None, None]
    den = [None, None]
    order = list(enumerate(tiles)) + [(n_gen + w, (w, own)) for w, own in enumerate(owns)]
    for t, (w, j) in order:
        p = jnp.exp2(s_scr[t] - mrow[w])
        cs = _group_reduce(p, jnp.sum)
        den[w] = cs if den[w] is None else den[w] + cs
        r = jnp.dot(vt_scr[j], p.astype(BF16), preferred_element_type=F32)
        acc[w] = r if acc[w] is None else acc[w] + r
    for w, own in enumerate(owns):
        tot = jnp.sum(den[w], axis=0, keepdims=True)
        o_ref[own * blk:(own + 1) * blk, :] = (acc[w] / tot).T.astype(BF16)


def _moba_prompt(q, k, v, *, n_batch, seq, n_heads, round_weights=(), round_layer=0):
    blk = MOBA_BLOCK
    assert seq % (2 * blk) == 0
    nq = seq // blk
    width = n_heads * HEAD_DIM
    grid = (n_batch, n_heads, nq // 2)
    plans = [_CastPlan(w, round_layer, grid[0] * grid[1] * grid[2]) for w in round_weights]
    cast_specs = [p.specs(lambda b, h, i: (b * grid[1] + h) * grid[2] + i) for p in plans]
    col = pl.BlockSpec((None, seq, HEAD_DIM), lambda b, h, i: (b, 0, h))
    out, *rounded = pl.pallas_call(
        functools.partial(_moba_kernel, nq=nq, plans=plans),
        grid=grid,
        in_specs=[col, col, col] + [s for s, _ in cast_specs],
        out_specs=[col] + [d for _, d in cast_specs],
        out_shape=[jax.ShapeDtypeStruct((n_batch, seq, width), BF16)]
                  + [jax.ShapeDtypeStruct(p.shape, BF16) for p in plans],
        scratch_shapes=[
            pltpu.VMEM((nq, blk, HEAD_DIM), BF16),
            pltpu.VMEM((nq, HEAD_DIM, blk), BF16),
            pltpu.VMEM((nq, HEAD_DIM), F32),
            pltpu.VMEM((nq, HEAD_DIM, blk), BF16),
            pltpu.VMEM((nq, nq, blk), F32),
            pltpu.VMEM((nq + 1, blk, blk), F32),
        ],
        compiler_params=pltpu.CompilerParams(
            dimension_semantics=("arbitrary", "arbitrary", "arbitrary"),
            vmem_limit_bytes=VMEM_LIMIT),
        name="moba_prompt",
    )(q.reshape(n_batch, seq, width), k.reshape(n_batch, seq, width),
      v.reshape(n_batch, seq, width), *round_weights)
    return out.reshape(n_batch * seq, width), rounded


def _outproj_kernel(a_ref, c_ref, wt_ref, wb_ref, x_ref, o_ref):
    o_ref[...] = (x_ref[...]
                  + jnp.dot(a_ref[...], wt_ref[...], preferred_element_type=F32)
                  + jnp.dot(c_ref[...], wb_ref[...], preferred_element_type=F32))


def _outproj(a, c, w_o, x, *, layer, bm, bn, round_weights=(), round_layer=0):
    M, D = x.shape
    half = a.shape[1]
    assert w_o.shape[1:] == (2 * half, D) and c.shape == a.shape and D % bn == 0 and M % bm == 0
    w_mode = pl.Buffered(1) if bn == D else None
    grid = (M // bm, D // bn)
    src, dst, shp = _rounding_specs(round_weights, round_layer, grid[0] * grid[1],
                                    lambda m, n: m * grid[1] + n)
    out, *rounded = pl.pallas_call(
        _with_rounding(_outproj_kernel, 5, 1, len(round_weights)),
        grid=grid,
        in_specs=[
            pl.BlockSpec((bm, half), lambda m, n: (m, 0)),
            pl.BlockSpec((bm, half), lambda m, n: (m, 0)),
            pl.BlockSpec((None, half, bn), lambda m, n: (layer, 0, n), pipeline_mode=w_mode),
            pl.BlockSpec((None, half, bn), lambda m, n: (layer, 1, n), pipeline_mode=w_mode),
            pl.BlockSpec((bm, bn), lambda m, n: (m, n)),
        ] + src,
        out_specs=[pl.BlockSpec((bm, bn), lambda m, n: (m, n))] + dst,
        out_shape=[jax.ShapeDtypeStruct((M, D), F32)] + shp,
        compiler_params=pltpu.CompilerParams(
            dimension_semantics=("arbitrary", "arbitrary"), vmem_limit_bytes=VMEM_LIMIT),
        name="outproj",
    )(a, c, w_o, w_o, x, *round_weights)
    return out, rounded


def _ffn_kernel(*refs, final_norm, plans):
    n = len(plans)
    x_ref, nw_ref, wg_ref, wu_ref, wd_ref, fw_ref = refs[:6]
    o_ref = refs[6 + n]
    h_scr = refs[7 + 2 * n]
    f = pl.program_id(1)

    @pl.when(f == 0)
    def _():
        x = x_ref[...]
        h_scr[...] = _rmsnorm(x, nw_ref[...]).astype(BF16)
        o_ref[...] = x

    _cast_step(refs[6:6 + n], refs[7 + n:7 + 2 * n])
    h = h_scr[...]
    g = jnp.dot(h, wg_ref[...], preferred_element_type=F32)
    u = jnp.dot(h, wu_ref[...], preferred_element_type=F32)
    a = (g * jax.nn.sigmoid(g) * u).astype(BF16)
    o_ref[...] += jnp.dot(a, wd_ref[...], preferred_element_type=F32)

    if final_norm:
        @pl.when(f == pl.num_programs(1) - 1)
        def _():
            o_ref[...] = _rmsnorm(o_ref[...], fw_ref[...])


def _ffn(x, nw, wg, wu, wd, fw, *, bm, bf, final_norm, round_weights=(), round_layer=0):
    M, D = x.shape
    FF = wg.shape[1]
    assert FF % bf == 0 and M % bm == 0
    grid = (M // bm, FF // bf)
    plans = [_CastPlan(w, round_layer, grid[0] * grid[1]) for w in round_weights]
    cast_specs = [p.specs(lambda m, f: m * grid[1] + f) for p in plans]
    out, *rounded = pl.pallas_call(
        functools.partial(_ffn_kernel, final_norm=final_norm, plans=plans),
        grid=grid,
        in_specs=[
            pl.BlockSpec((bm, D), lambda m, f: (m, 0)),
            pl.BlockSpec((1, D), lambda m, f: (0, 0)),
            pl.BlockSpec((D, bf), lambda m, f: (0, f)),
            pl.BlockSpec((D, bf), lambda m, f: (0, f)),
            pl.BlockSpec((bf, D), lambda m, f: (f, 0)),
            pl.BlockSpec((1, D), lambda m, f: (0, 0)),
        ] + [s for s, _ in cast_specs],
        out_specs=[pl.BlockSpec((bm, D), lambda m, f: (m, 0))] + [d for _, d in cast_specs],
        out_shape=[jax.ShapeDtypeStruct((M, D), F32)]
                  + [jax.ShapeDtypeStruct(p.shape, BF16) for p in plans],
        scratch_shapes=[pltpu.VMEM((bm, D), BF16)],
        compiler_params=pltpu.CompilerParams(
            dimension_semantics=("arbitrary", "arbitrary"), vmem_limit_bytes=VMEM_LIMIT),
        name="ffn",
    )(x, nw, wg, wu, wd, fw, *round_weights)
    return out, rounded


def _pool_kernel(x_ref, nw_ref, wp_ref, ps_ref, hist_ref, o_ref, pst_ref, ext_scr, band_scr,
                 *, bm, tps, stride, pos_base):
    m = pl.program_id(0)
    hs = POOL_HIST * stride
    base = _halo_base(hs)
    rb = band_scr.shape[1]
    nd = min(rb, base)
    assert bm % rb == 0
    assert nd == rb or (stride == 1 and nd % SUBLANES == 0 and nd >= max(POOL_WINDOWS))
    grp = wp_ref.shape[1]
    first = (m % tps) == 0

    if nd < rb:
        @pl.when(m == 0)
        def _():
            lag = (lax.broadcasted_iota(jnp.int32, (rb, rb), 0)
                   - lax.broadcasted_iota(jnp.int32, (rb, rb), 1))
            for gi, w in enumerate(POOL_WINDOWS):
                band_scr[gi] = jnp.where((lag >= 0) & (lag < w), 1.0, 0.0).astype(BF16)

    @pl.when(first)
    def _():
        ext_scr[base - hs:base, :] = hist_ref[...]

    @pl.when(jnp.logical_not(first))
    def _():
        ext_scr[base - hs:base, :] = ext_scr[base + bm - hs:base + bm, :]

    ext_scr[base:base + bm, :] = _rmsnorm(x_ref[...], nw_ref[...])
    for r0 in range(0, bm, rb):
        r = (m % tps) * bm + r0 + lax.broadcasted_iota(jnp.int32, (nd, 1), 0)
        pos = pos_base + (r if stride == 1 else lax.shift_right_logical(r, stride.bit_length() - 1))
        for gi, w in enumerate(POOL_WINDOWS):
            cols = slice(gi * grp, (gi + 1) * grp)
            hb = ext_scr[base + r0:base + r0 + rb, cols]
            tot = hb[:nd]
            for kk in range(1, w):
                off = base + r0 - kk * stride
                tot = tot + ext_scr[off:off + nd, cols]
            d = tot / jnp.minimum(pos + 1, w).astype(F32) - hb[:nd]
            if nd < rb:
                hi = hb.astype(BF16)
                lo = (hb - hi.astype(F32)).astype(BF16)
                band = band_scr[gi]
                full = (jnp.dot(band, hi, preferred_element_type=F32)
                        + jnp.dot(band, lo, preferred_element_type=F32))
                d = jnp.concatenate([d, full[nd:] * (1.0 / w) - hb[nd:]], axis=0)
            out = jnp.dot(d.astype(BF16), wp_ref[gi], preferred_element_type=F32)
            o_ref[r0:r0 + rb, cols] = x_ref[r0:r0 + rb, cols] + out * ps_ref[:, cols]
    pst_ref[...] = ext_scr[base + bm - hs:base + bm, :]


def _pool(x, nw, w_pool, pool_scale, hist, *, layer, bm, tps, stride, pos_base):
    M, D = x.shape
    n_seq = hist.shape[0]
    hs = POOL_HIST * stride
    assert stride & (stride - 1) == 0 and hist.shape == (n_seq, hs, D) and M % bm == 0
    base = _halo_base(hs)
    _, ng, grp, _ = w_pool.shape
    assert ng == len(POOL_WINDOWS) and ng * grp == D
    kern = functools.partial(_pool_kernel, bm=bm, tps=tps, stride=stride, pos_base=pos_base)
    return pl.pallas_call(
        kern,
        grid=(M // bm,),
        in_specs=[
            pl.BlockSpec((bm, D), lambda m: (m, 0)),
            pl.BlockSpec((1, D), lambda m: (0, 0)),
            pl.BlockSpec((None, ng, grp, grp), lambda m: (layer, 0, 0, 0)),
            pl.BlockSpec((1, D), lambda m: (0, 0)),
            pl.BlockSpec((None, hs, D), lambda m: (m // tps, 0, 0)),
        ],
        out_specs=[
            pl.BlockSpec((bm, D), lambda m: (m, 0)),
            pl.BlockSpec((None, hs, D), lambda m: (m // tps, 0, 0)),
        ],
        out_shape=[
            jax.ShapeDtypeStruct((M, D), F32),
            jax.ShapeDtypeStruct((n_seq, hs, D), F32),
        ],
        scratch_shapes=[pltpu.VMEM((base + bm, D), F32),
                        pltpu.VMEM((ng, min(bm, MXU_DIM), min(bm, MXU_DIM)), BF16)],
        compiler_params=pltpu.CompilerParams(
            dimension_semantics=("arbitrary",), vmem_limit_bytes=VMEM_LIMIT),
        name="pool",
    )(x, nw, w_pool, pool_scale, hist)


def _head_match(n_rows, n_cols, n_heads, t_pad):
    rh = lax.broadcasted_iota(jnp.int32, (n_rows, n_cols), 0) // t_pad
    ch = lax.broadcasted_iota(jnp.int32, (n_rows, n_cols), 1) % n_heads
    return rh == ch


def _fold_heads(x, n_heads, t_pad):
    out = x[0:t_pad]
    for hh in range(1, n_heads):
        out = out + x[hh * t_pad:(hh + 1) * t_pad]
    return out


def _lane_class(x, n_heads, op):
    sh = n_heads
    while sh < x.shape[-1]:
        x = op(x, pltpu.roll(x, sh, axis=x.ndim - 1))
        sh *= 2
    return x


def _lane_tiles(x, op):
    out = x[:, 0:128]
    for t in range(1, x.shape[1] // 128):
        out = op(out, x[:, t * 128:(t + 1) * 128])
    return out


def _sa_logits_kernel(pt_ref, q_ref, *refs, n_pages, n_heads, t_pad, pages_per_block):
    k_refs = refs[:n_pages]
    r_ref, g_ref = refs[n_pages:]
    qa = q_ref[...]
    rows = k_refs[0].shape[0] * n_heads
    match = _head_match(qa.shape[0], rows, n_heads, t_pad)
    for pg in range(n_pages):
        k2 = k_refs[pg][...].reshape(rows, HEAD_DIM).astype(BF16)
        l2 = lax.dot_general(qa, k2, _NT, preferred_element_type=F32)
        r_ref[pg] = _fold_heads(jnp.where(match, l2, 0.0), n_heads, t_pad)
    for bb in range(n_pages // pages_per_block):
        tot = _lane_tiles(r_ref[bb * pages_per_block], jnp.add)
        for pg in range(1, pages_per_block):
            tot = tot + _lane_tiles(r_ref[bb * pages_per_block + pg], jnp.add)
        g_ref[bb] = _lane_class(tot, n_heads, jnp.add)


def _sa_values_kernel(pt_ref, r_ref, g_ref, q_ref, kn_ref, vn_ref, *refs,
                      n_pages, n_heads, t_pad, pages_per_block):
    v_refs = refs[:n_pages]
    o_ref, p_scr, sel_scr, l_scr, acc_scr = refs[n_pages:]
    c = pl.program_id(1)
    n_blocks = g_ref.shape[0]
    rows = v_refs[0].shape[0] * n_heads
    n_rows = q_ref.shape[0]
    match = _head_match(n_rows, rows, n_heads, t_pad)

    def spread(p):
        return jnp.where(match[:, :p.shape[1]], jnp.concatenate([p] * n_heads, axis=0), 0.0).astype(BF16)

    @pl.when(c == 0)
    def _():
        gates = g_ref[...]
        idx = lax.broadcasted_iota(jnp.int32, gates.shape, 0)
        sel = jnp.zeros(gates.shape, F32)
        for _ in range(MOBA_TOPK):
            mx = jnp.max(gates, axis=0, keepdims=True)
            first = jnp.min(jnp.where(gates == mx, idx, n_blocks), axis=0, keepdims=True)
            hit = idx == first
            sel = jnp.where(hit, 1.0, sel)
            gates = jnp.where(hit, -jnp.inf, gates)
        sel_scr[...] = sel

        own_w = kn_ref.shape[0]
        lo = lax.dot_general(q_ref[...], kn_ref[...].astype(BF16), _NT, preferred_element_type=F32)
        ro = _fold_heads(jnp.where(match[:, :own_w], lo, 0.0), n_heads, t_pad)
        t_key = lax.broadcasted_iota(jnp.int32, ro.shape, 1) // n_heads
        t_qry = lax.broadcasted_iota(jnp.int32, ro.shape, 0)
        ro = jnp.where(t_key <= t_qry, ro, NEG_INF)

        def max_body(b, mx):
            here = _lane_tiles(r_ref[b * pages_per_block], jnp.maximum)
            for pg in range(1, pages_per_block):
                here = jnp.maximum(here, _lane_tiles(r_ref[b * pages_per_block + pg], jnp.maximum))
            return jnp.maximum(mx, jnp.where(sel_scr[b] > 0.0, here, NEG_INF))

        mx = lax.fori_loop(0, n_blocks, max_body, ro)
        mx = _lane_class(mx, n_heads, jnp.maximum)
        width = r_ref.shape[2]
        mx_w = jnp.concatenate([mx] * (width // 128), axis=1)

        def exp_body(b, tot):
            keep = jnp.concatenate([sel_scr[b]] * (width // 128), axis=1) > 0.0
            for pg in range(pages_per_block):
                p = jnp.where(keep, jnp.exp(r_ref[b * pages_per_block + pg] - mx_w), 0.0)
                p_scr[b * pages_per_block + pg] = p
                tot = tot + _lane_tiles(p, jnp.add)
            return tot

        po = jnp.exp(ro - mx)
        tot = lax.fori_loop(0, n_blocks, exp_body, po)
        l_scr[...] = _lane_class(tot, n_heads, jnp.add)
        acc_scr[...] = jnp.dot(spread(po), vn_ref[...].astype(BF16), preferred_element_type=F32)

    acc = acc_scr[...]
    for pg in range(n_pages):
        v2 = v_refs[pg][...].reshape(rows, HEAD_DIM).astype(BF16)
        acc = acc + jnp.dot(spread(p_scr[c * n_pages + pg]), v2, preferred_element_type=F32)
    acc_scr[...] = acc

    @pl.when(c == pl.num_programs(1) - 1)
    def _():
        l = l_scr[...]
        den = jnp.concatenate([l[:, hh:hh + 1] for hh in range(n_heads)], axis=0)
        o_ref[...] = acc_scr[...] / den


def _sample_attention(q_all, k_new, v_new, cache_k, cache_v, page_table, *, layer, n_heads, t_pad):
    n_dec, n_pt = page_table.shape
    page = cache_k.shape[2]
    ppb = MOBA_BLOCK // page
    P = PAGES_PER_STEP
    assert MOBA_BLOCK % page == 0 and n_pt % P == 0 and P % ppb == 0
    n_steps = n_pt // P
    n_blocks = n_pt // ppb
    width = page * n_heads
    n_rows = n_heads * t_pad

    def page_spec(i):
        return pl.BlockSpec((None, None, page, n_heads, HEAD_DIM),
                            lambda b, c, pt: (layer, pt[b, c * P + i], 0, 0, 0))

    params = pltpu.CompilerParams(dimension_semantics=("arbitrary", "arbitrary"),
                                  vmem_limit_bytes=VMEM_LIMIT)
    common = dict(n_pages=P, n_heads=n_heads, t_pad=t_pad, pages_per_block=ppb)
    logits, gates = pl.pallas_call(
        functools.partial(_sa_logits_kernel, **common),
        grid_spec=pltpu.PrefetchScalarGridSpec(
            num_scalar_prefetch=1,
            grid=(n_dec, n_steps),
            in_specs=[pl.BlockSpec((None, n_rows, HEAD_DIM), lambda b, c, pt: (b, 0, 0))]
                     + [page_spec(i) for i in range(P)],
            out_specs=[
                pl.BlockSpec((None, P, t_pad, width), lambda b, c, pt: (b, c, 0, 0)),
                pl.BlockSpec((None, P // ppb, t_pad, 128), lambda b, c, pt: (b, c, 0, 0)),
            ],
        ),
        out_shape=[
            jax.ShapeDtypeStruct((n_dec, n_pt, t_pad, width), F32),
            jax.ShapeDtypeStruct((n_dec, n_blocks, t_pad, 128), F32),
        ],
        compiler_params=params,
        name="sample_logits",
    )(page_table, q_all, *([cache_k] * P))

    own = k_new.shape[1]
    return pl.pallas_call(
        functools.partial(_sa_values_kernel, **common),
        grid_spec=pltpu.PrefetchScalarGridSpec(
            num_scalar_prefetch=1,
            grid=(n_dec, n_steps),
            in_specs=[
                pl.BlockSpec((None, n_pt, t_pad, width), lambda b, c, pt: (b, 0, 0, 0)),
                pl.BlockSpec((None, n_blocks, t_pad, 128), lambda b, c, pt: (b, 0, 0, 0)),
                pl.BlockSpec((None, n_rows, HEAD_DIM), lambda b, c, pt: (b, 0, 0)),
                pl.BlockSpec((None, own, HEAD_DIM), lambda b, c, pt: (b, 0, 0)),
                pl.BlockSpec((None, own, HEAD_DIM), lambda b, c, pt: (b, 0, 0)),
            ] + [page_spec(i) for i in range(P)],
            out_specs=pl.BlockSpec((None, n_rows, HEAD_DIM), lambda b, c, pt: (b, 0, 0)),
            scratch_shapes=[
                pltpu.VMEM((n_pt, t_pad, width), F32),
                pltpu.VMEM((n_blocks, t_pad, 128), F32),
                pltpu.VMEM((t_pad, 128), F32),
                pltpu.VMEM((n_rows, HEAD_DIM), F32),
            ],
        ),
        out_shape=jax.ShapeDtypeStruct((n_dec, n_rows, HEAD_DIM), F32),
        compiler_params=params,
        name="sample_values",
    )(page_table, logits, gates, q_all, k_new, v_new, *([cache_v] * P))


def _rope_tables(pos):
    half = HEAD_DIM // 2
    inv = ROPE_THETA ** (-jnp.arange(half, dtype=F32) / half)
    ang = pos.astype(F32)[:, None] * inv[None, :]
    cos, sin = jnp.cos(ang), jnp.sin(ang)
    return jnp.concatenate([cos, cos], axis=1), jnp.concatenate([-sin, sin], axis=1)


def kernel(x_prompt, x_sample, cache_k, cache_v, page_table, state_conv, state_pool,
           norm_mix, norm_ffn, norm_final, w_in, conv_w, w_o, w_pool, pool_scale,
           w_gate, w_up, w_down):
    n_batch, n_seq, d = x_prompt.shape
    n_dec, n_new, _ = x_sample.shape
    depth = norm_mix.shape[0]
    seg = conv_w.shape[2]
    n_heads = seg // HEAD_DIM
    page = cache_k.shape[2]
    past_len = page_table.shape[1] * page
    assert n_dec == SUBLANES and n_new <= SUBLANES and past_len % MOBA_BLOCK == 0
    assert cache_k.shape[3] == n_heads and cache_k.shape[4] == HEAD_DIM
    q_scale = HEAD_DIM ** -0.5
    q_scale_log2 = q_scale * LOG2_E
    t_pad = SUBLANES
    own_keys = 128 // n_heads
    assert n_new <= own_keys

    w_in_b, w_o_b, w_pool_b = w_in.astype(BF16), w_o.astype(BF16), w_pool.astype(BF16)
    ffn_f32 = (w_gate, w_up, w_down)
    ffn_bf16 = {}
    row = lambda a: a.reshape(1, -1)

    xp = x_prompt.reshape(n_batch * n_seq, d)
    xs = x_sample.transpose(1, 0, 2).reshape(n_new * n_dec, d)
    ms = n_new * n_dec
    bm = 512
    tps = n_seq // bm
    cos_p, sin_p = _rope_tables(jnp.arange(n_seq, dtype=jnp.int32))
    pos_s = past_len + jnp.repeat(jnp.arange(n_new, dtype=jnp.int32), n_dec)
    cos_s, sin_s = _rope_tables(pos_s)

    def tm(a):
        return a.transpose(1, 0, 2).reshape(1, a.shape[1] * n_dec, a.shape[2])

    def untm(a, rows):
        return a.reshape(rows, n_dec, a.shape[-1]).transpose(1, 0, 2)

    k_p, v_p, k_s, v_s, conv_p, conv_s, pool_p, pool_s = [], [], [], [], [], [], [], []
    for layer in range(depth):
        if layer % 2 == 0:
            e = layer // 2
            ahead = layer + 1 < depth
            q, k, v, yc, cst, r_in = _inproj(
                xp, row(norm_mix[layer]), w_in_b, cos_p, sin_p, conv_w[e],
                jnp.zeros((n_batch, CONV_HIST, seg), F32),
                layer=e, bm=bm, tps=tps, stride=1, q_scale=q_scale_log2,
                round_qkv=(w_gate,) if ahead else (), round_gconv=(w_up,) if ahead else (),
                round_layer=layer + 1)
            attn, ffn_bf16[layer] = _moba_prompt(
                q, k, v, n_batch=n_batch, seq=n_seq, n_heads=n_heads,
                round_weights=ffn_f32, round_layer=layer)
            xp, r_out = _outproj(attn, yc, w_o_b, xp, layer=e, bm=bm, bn=d,
                                 round_weights=(w_down,) if ahead else (), round_layer=layer + 1)
            if ahead:
                ffn_bf16[layer + 1] = r_in + r_out
            xs, attn = lax.optimization_barrier((xs, attn))
            k_p.append(k.reshape(n_batch, n_seq, n_heads, HEAD_DIM))
            v_p.append(v.reshape(n_batch, n_seq, n_heads, HEAD_DIM))
            conv_p.append(cst)
            q, k, v, yc, cst, _ = _inproj(
                xs, row(norm_mix[layer]), w_in_b, cos_s, sin_s, conv_w[e], tm(state_conv[e]),
                layer=e, bm=ms, tps=1, stride=n_dec, q_scale=q_scale)
            heads = lambda a: a.reshape(n_new, n_dec, n_heads, HEAD_DIM)
            q_all = jnp.pad(heads(q).transpose(1, 2, 0, 3), ((0, 0), (0, 0), (0, t_pad - n_new), (0, 0)))
            q_all = q_all.reshape(n_dec, n_heads * t_pad, HEAD_DIM)
            k_new = heads(k).transpose(1, 0, 2, 3)
            v_new = heads(v).transpose(1, 0, 2, 3)
            own = lambda a: jnp.pad(a, ((0, 0), (0, own_keys - n_new), (0, 0), (0, 0))).reshape(
                n_dec, own_keys * n_heads, HEAD_DIM)
            o = _sample_attention(q_all, own(k_new), own(v_new), cache_k, cache_v, page_table,
                                  layer=e, n_heads=n_heads, t_pad=t_pad)
            o = o.reshape(n_dec, n_heads, t_pad, HEAD_DIM)[:, :, :n_new]
            attn = o.transpose(2, 0, 1, 3).reshape(ms, seg).astype(BF16)
            xs, _ = _outproj(attn, yc, w_o_b, xs, layer=e, bm=ms, bn=d)
            k_s.append(k_new)
            v_s.append(v_new)
            conv_s.append(untm(cst[0], CONV_HIST))
        else:
            o_ = layer // 2
            xp, pst = _pool(xp, row(norm_mix[layer]), w_pool_b, row(pool_scale[o_]),
                            jnp.zeros((n_batch, POOL_HIST, d), F32),
                            layer=o_, bm=bm, tps=tps, stride=1, pos_base=0)
            pool_p.append(pst)
            xs, pst = _pool(xs, row(norm_mix[layer]), w_pool_b, row(pool_scale[o_]),
                            tm(state_pool[o_]), layer=o_, bm=ms, tps=1, stride=n_dec,
                            pos_base=past_len)
            pool_s.append(untm(pst[0], POOL_HIST))
        last = layer == depth - 1
        wg, wu, wd = ffn_bf16[layer]
        ffn = functools.partial(_ffn, nw=row(norm_ffn[layer]), wg=wg, wu=wu, wd=wd,
                                fw=row(norm_final), bf=512, final_norm=last)
        xp, _ = ffn(xp, bm=2 * bm)
        xs, _ = ffn(xs, bm=ms)

    y_prompt = xp.reshape(n_batch, n_seq, d)
    y_sample = xs.reshape(n_new, n_dec, d).transpose(1, 0, 2)
    return (y_prompt, y_sample, jnp.stack(k_p), jnp.stack(v_p), jnp.stack(k_s), jnp.stack(v_s),
            jnp.stack(conv_p), jnp.stack(conv_s), jnp.stack(pool_p), jnp.stack(pool_s))
```

```python
import functools

import jax
import jax.numpy as jnp
from jax import lax
from jax.experimental import pallas as pl
from jax.experimental.pallas import tpu as pltpu

F32 = jnp.float32
BF16 = jnp.bfloat16

HEAD_DIM = 128
MOBA_BLOCK = 256
MOBA_TOPK = 3
CONV_K = 3
CONV_HIST = CONV_K - 1
ROPE_THETA = 10000.0
POOL_WINDOWS = (2, 4, 8, 16)
POOL_HIST = max(POOL_WINDOWS) - 1
RMS_EPS = 1e-6
NEG_INF = -1e30
LOG2_E = 1.4426950408889634

SUBLANES = 8
LANES = 128
BF16_ROWS = 16
MXU_DIM = 256
ROW_TILE = 512
FFN_ROW_TILE = 1024
FFN_FF_TILE = 512
VMEM_LIMIT = 56 * 1024 * 1024
PAGES_PER_STEP = 32

_NT = (((1,), (1,)), ((), ()))


def _rmsnorm(x, g):
    ms = jnp.mean(x * x, axis=-1, keepdims=True)
    return x * lax.rsqrt(ms + RMS_EPS) * g


def _halo_base(rows):
    return -(-rows // SUBLANES) * SUBLANES


class _CastPlan:
    def __init__(self, stacked, layer, n_steps):
        _, rows, cols = stacked.shape
        assert cols % LANES == 0 and rows % BF16_ROWS == 0
        self.count = max(d for d in range(1, min(n_steps, rows // BF16_ROWS) + 1)
                         if (rows // BF16_ROWS) % d == 0)
        self.block = (rows // self.count, cols)
        self.layer = layer
        self.shape = (rows, cols)

    def _index(self, step):
        return jnp.minimum(step, self.count - 1), 0

    def specs(self, step_of):
        def src(*g):
            return (self.layer,) + self._index(step_of(*g))
        def dst(*g):
            return self._index(step_of(*g))
        return pl.BlockSpec((None,) + self.block, src), pl.BlockSpec(self.block, dst)


def _cast_step(src_refs, dst_refs):
    for src, dst in zip(src_refs, dst_refs):
        dst[...] = src[...].astype(BF16)


def _with_rounding(body, n_in, n_out, n_cast):
    def kernel(*refs, **kw):
        a, b, c = n_in + n_cast, n_in + n_cast + n_out, n_in + 2 * n_cast + n_out
        _cast_step(refs[n_in:a], refs[b:c])
        return body(*refs[:n_in], *refs[a:b], *refs[c:], **kw)
    return kernel


def _rounding_specs(todo, n_steps, step_of):
    plans = [_CastPlan(w, layer, n_steps) for w, layer in todo]
    specs = [p.specs(step_of) for p in plans]
    return ([s for s, _ in specs], [d for _, d in specs],
            [jax.ShapeDtypeStruct(p.shape, BF16) for p in plans])


def _qkv_kernel(x_ref, nw_ref, w_ref, cos_ref, sin_ref, q_ref, k_ref, v_ref, *, n_heads, q_scale):
    seg = n_heads * HEAD_DIM
    h = _rmsnorm(x_ref[...], nw_ref[...]).astype(BF16)
    cos = cos_ref[...]
    sin = sin_ref[...]

    def rope(a, hh):
        part = a[:, hh * HEAD_DIM:(hh + 1) * HEAD_DIM]
        return part * cos + pltpu.roll(part, HEAD_DIM // 2, axis=1) * sin

    aq = jnp.dot(h, w_ref[:, 0:seg], preferred_element_type=F32)
    for hh in range(n_heads):
        q_ref[:, hh * HEAD_DIM:(hh + 1) * HEAD_DIM] = (rope(aq, hh) * q_scale).astype(BF16)
    ak = jnp.dot(h, w_ref[:, seg:2 * seg], preferred_element_type=F32)
    for hh in range(n_heads):
        k_ref[:, hh * HEAD_DIM:(hh + 1) * HEAD_DIM] = rope(ak, hh)
    v_ref[...] = jnp.dot(h, w_ref[:, 2 * seg:3 * seg], preferred_element_type=F32)


def _gconv_kernel(x_ref, nw_ref, w_ref, cw_ref, hist_ref, yc_ref, cst_ref, ue_scr,
                  *, bm, tps, stride):
    m = pl.program_id(0)
    seg = yc_ref.shape[1]
    hs = CONV_HIST * stride
    base = _halo_base(hs)
    h = _rmsnorm(x_ref[...], nw_ref[...]).astype(BF16)
    u = (jnp.dot(h, w_ref[:, seg:2 * seg], preferred_element_type=F32)
         * jnp.dot(h, w_ref[:, 2 * seg:3 * seg], preferred_element_type=F32))
    first = (m % tps) == 0

    @pl.when(first)
    def _():
        ue_scr[base - hs:base, :] = hist_ref[...]

    @pl.when(jnp.logical_not(first))
    def _():
        ue_scr[base - hs:base, :] = ue_scr[base + bm - hs:base + bm, :]

    ue_scr[base:base + bm, :] = u
    cw = cw_ref[...]
    conv = u * cw[CONV_K - 1:CONV_K, :]
    for j in range(CONV_K - 1):
        off = base - (CONV_K - 1 - j) * stride
        conv = conv + ue_scr[off:off + bm, :] * cw[j:j + 1, :]
    gate_b = jnp.dot(h, w_ref[:, 0:seg], preferred_element_type=F32)
    yc_ref[...] = (gate_b * conv).astype(BF16)
    cst_ref[...] = ue_scr[base + bm - hs:base + bm, :]


def _inproj(x, nw, w_in, cos, sin, conv_w, hist, *, layer, bm, tps, stride, q_scale,
            round_qkv=(), round_gconv=()):
    M, D = x.shape
    seg = conv_w.shape[1]
    assert w_in.shape[2] == 6 * seg and M % bm == 0 and (M // bm) % tps == 0
    n_seq = hist.shape[0]
    hs = CONV_HIST * stride
    assert hist.shape == (n_seq, hs, seg) and bm >= hs
    base = _halo_base(hs)
    params = pltpu.CompilerParams(dimension_semantics=("arbitrary",), vmem_limit_bytes=VMEM_LIMIT)
    rows = lambda width: pl.BlockSpec((bm, width), lambda m: (m, 0))
    half = lambda n: pl.BlockSpec((None, D, 3 * seg), lambda m: (layer, 0, n),
                                  pipeline_mode=pl.Buffered(1))
    norm = pl.BlockSpec((1, D), lambda m: (0, 0))
    table = pl.BlockSpec((bm, HEAD_DIM), lambda m: (m % tps, 0))
    n_steps = M // bm
    src, dst, shp = _rounding_specs(round_qkv, n_steps, lambda m: m)
    q, k, v, *rounded_a = pl.pallas_call(
        _with_rounding(functools.partial(_qkv_kernel, n_heads=seg // HEAD_DIM, q_scale=q_scale),
                       5, 3, len(round_qkv)),
        grid=(n_steps,),
        in_specs=[rows(D), norm, half(0), table, table] + src,
        out_specs=[rows(seg), rows(seg), rows(seg)] + dst,
        out_shape=[
            jax.ShapeDtypeStruct((M, seg), BF16),
            jax.ShapeDtypeStruct((M, seg), F32),
            jax.ShapeDtypeStruct((M, seg), F32),
        ] + shp,
        compiler_params=params,
        name="inproj_qkv",
    )(x, nw, w_in, cos, sin, *[w for w, _ in round_qkv])
    state = pl.BlockSpec((None, hs, seg), lambda m: (m // tps, 0, 0))
    src, dst, shp = _rounding_specs(round_gconv, n_steps, lambda m: m)
    yc, cst, *rounded_b = pl.pallas_call(
        _with_rounding(functools.partial(_gconv_kernel, bm=bm, tps=tps, stride=stride),
                       5, 2, len(round_gconv)),
        grid=(n_steps,),
        in_specs=[rows(D), norm, half(1), pl.BlockSpec((CONV_K, seg), lambda m: (0, 0)), state]
                 + src,
        out_specs=[rows(seg), state] + dst,
        out_shape=[
            jax.ShapeDtypeStruct((M, seg), BF16),
            jax.ShapeDtypeStruct((n_seq, hs, seg), F32),
        ] + shp,
        scratch_shapes=[pltpu.VMEM((base + bm, seg), F32)],
        compiler_params=params,
        name="inproj_gconv",
    )(x, nw, w_in, conv_w, hist, *[w for w, _ in round_gconv])
    return q, k, v, yc, cst, rounded_a + rounded_b


def _group_reduce(x, op):
    return op(x.reshape(x.shape[0] // SUBLANES, SUBLANES, x.shape[1]), axis=0)


def _moba_kernel(q_ref, k_ref, v_ref, o_ref, kb_scr, vt_scr, ks_scr, qt_scr, sel_scr, s_scr,
               *, nq):
    i = pl.program_id(2)
    blk = MOBA_BLOCK

    @pl.when(i == 0)
    def _():
        for j in range(nq):
            kf = k_ref[j * blk:(j + 1) * blk, :]
            kb_scr[j] = kf.astype(BF16)
            vt_scr[j] = v_ref[j * blk:(j + 1) * blk, :].T.astype(BF16)
            ks_scr[j:j + 1, :] = jnp.sum(kf, axis=0, keepdims=True)
        ks = ks_scr[...]
        ks_hi = ks.astype(BF16)
        ks_lo = (ks - ks_hi.astype(F32)).astype(BF16)
        blk_id = lax.broadcasted_iota(jnp.int32, (nq, blk), 0)
        for own in range(nq):
            qt = q_ref[own * blk:(own + 1) * blk, :].astype(F32).T.astype(BF16)
            qt_scr[own] = qt
            gate = (jnp.dot(ks_hi, qt, preferred_element_type=F32)
                    + jnp.dot(ks_lo, qt, preferred_element_type=F32))
            g = jnp.where(blk_id < own, gate, -jnp.inf)
            sel = jnp.zeros((nq, blk), jnp.bool_)
            for _ in range(min(MOBA_TOPK, own)):
                top = jnp.max(g, axis=0, keepdims=True)
                first = jnp.min(jnp.where(g == top, blk_id, nq), axis=0, keepdims=True)
                hit = blk_id == first
                sel = sel | hit
                g = jnp.where(hit, -jnp.inf, g)
            sel_scr[own] = jnp.where(sel, 1.0, 0.0)

    for pair in range(nq // 2):
        @pl.when(i == pair)
        def _(pair=pair):
            _moba_pair(pair, o_ref, kb_scr, vt_scr, qt_scr, sel_scr, s_scr, nq=nq)


def _moba_pair(pair, o_ref, kb_scr, vt_scr, qt_scr, sel_scr, s_scr, *, nq):
    blk = MOBA_BLOCK
    owns = (pair, nq - 1 - pair)

    tiles = [(w, j) for w, own in enumerate(owns) for j in range(own)]
    n_gen = len(tiles)

    mx = [None, None]
    for t, (w, j) in enumerate(tiles):
        s = jnp.dot(kb_scr[j], qt_scr[owns[w]], preferred_element_type=F32)
        s = jnp.where(sel_scr[owns[w], j:j + 1, :] > 0.0, s, NEG_INF)
        s_scr[t] = s
        cm = _group_reduce(s, jnp.max)
        mx[w] = cm if mx[w] is None else jnp.maximum(mx[w], cm)
    key_id = lax.broadcasted_iota(jnp.int32, (blk, blk), 0)
    qry_id = lax.broadcasted_iota(jnp.int32, (blk, blk), 1)
    for w, own in enumerate(owns):
        s = jnp.dot(kb_scr[own], qt_scr[own], preferred_element_type=F32)
        s = jnp.where(key_id <= qry_id, s, NEG_INF)
        s_scr[n_gen + w] = s
        cm = _group_reduce(s, jnp.max)
        mx[w] = cm if mx[w] is None else jnp.maximum(mx[w], cm)
    mrow = [jnp.max(m, axis=0, keepdims=True) for m in mx]

    acc = [None, None]
    den = [None, None]
    order = list(enumerate(tiles)) + [(n_gen + w, (w, own)) for w, own in enumerate(owns)]
    for t, (w, j) in order:
        p = jnp.exp2(s_scr[t] - mrow[w])
        cs = _group_reduce(p, jnp.sum)
        den[w] = cs if den[w] is None else den[w] + cs
        r = jnp.dot(vt_scr[j], p.astype(BF16), preferred_element_type=F32)
        acc[w] = r if acc[w] is None else acc[w] + r
    for w, own in enumerate(owns):
        tot = jnp.sum(den[w], axis=0, keepdims=True)
        o_ref[own * blk:(own + 1) * blk, :] = (acc[w] / tot).T.astype(BF16)


def _moba_prompt(q, k, v, *, n_batch, seq, n_heads, round_weights=()):
    blk = MOBA_BLOCK
    assert seq % (2 * blk) == 0
    nq = seq // blk
    width = n_heads * HEAD_DIM
    grid = (n_batch, n_heads, nq // 2)
    src, dst, shp = _rounding_specs(round_weights, grid[0] * grid[1] * grid[2],
                                    lambda b, h, i: (b * grid[1] + h) * grid[2] + i)
    col = pl.BlockSpec((None, seq, HEAD_DIM), lambda b, h, i: (b, 0, h))
    out, *rounded = pl.pallas_call(
        _with_rounding(functools.partial(_moba_kernel, nq=nq), 3, 1, len(round_weights)),
        grid=grid,
        in_specs=[col, col, col] + src,
        out_specs=[col] + dst,
        out_shape=[jax.ShapeDtypeStruct((n_batch, seq, width), BF16)] + shp,
        scratch_shapes=[
            pltpu.VMEM((nq, blk, HEAD_DIM), BF16),
            pltpu.VMEM((nq, HEAD_DIM, blk), BF16),
            pltpu.VMEM((nq, HEAD_DIM), F32),
            pltpu.VMEM((nq, HEAD_DIM, blk), BF16),
            pltpu.VMEM((nq, nq, blk), F32),
            pltpu.VMEM((nq + 1, blk, blk), F32),
        ],
        compiler_params=pltpu.CompilerParams(
            dimension_semantics=("arbitrary", "arbitrary", "arbitrary"),
            vmem_limit_bytes=VMEM_LIMIT),
        name="moba_prompt",
    )(q.reshape(n_batch, seq, width), k.reshape(n_batch, seq, width),
      v.reshape(n_batch, seq, width), *[w for w, _ in round_weights])
    return out.reshape(n_batch * seq, width), rounded


def _outproj_kernel(a_ref, c_ref, wt_ref, wb_ref, x_ref, o_ref):
    o_ref[...] = (x_ref[...]
                  + jnp.dot(a_ref[...], wt_ref[...], preferred_element_type=F32)
                  + jnp.dot(c_ref[...], wb_ref[...], preferred_element_type=F32))


def _outproj(a, c, w_o, x, *, bm, bn, round_weights=()):
    M, D = x.shape
    half = a.shape[1]
    assert w_o.shape == (2 * half, D) and c.shape == a.shape and D % bn == 0 and M % bm == 0
    w_mode = pl.Buffered(1) if bn == D else None
    grid = (M // bm, D // bn)
    src, dst, shp = _rounding_specs(round_weights, grid[0] * grid[1],
                                    lambda m, n: m * grid[1] + n)
    out, *rounded = pl.pallas_call(
        _with_rounding(_outproj_kernel, 5, 1, len(round_weights)),
        grid=grid,
        in_specs=[
            pl.BlockSpec((bm, half), lambda m, n: (m, 0)),
            pl.BlockSpec((bm, half), lambda m, n: (m, 0)),
            pl.BlockSpec((half, bn), lambda m, n: (0, n), pipeline_mode=w_mode),
            pl.BlockSpec((half, bn), lambda m, n: (1, n), pipeline_mode=w_mode),
            pl.BlockSpec((bm, bn), lambda m, n: (m, n)),
        ] + src,
        out_specs=[pl.BlockSpec((bm, bn), lambda m, n: (m, n))] + dst,
        out_shape=[jax.ShapeDtypeStruct((M, D), F32)] + shp,
        compiler_params=pltpu.CompilerParams(
            dimension_semantics=("arbitrary", "arbitrary"), vmem_limit_bytes=VMEM_LIMIT),
        name="outproj",
    )(a, c, w_o, w_o, x, *[w for w, _ in round_weights])
    return out, rounded


def _ffn_kernel(x_ref, nw_ref, wg_ref, wu_ref, wd_ref, fw_ref, o_ref, h_scr, *, final_norm):
    f = pl.program_id(1)

    @pl.when(f == 0)
    def _():
        x = x_ref[...]
        h_scr[...] = _rmsnorm(x, nw_ref[...]).astype(BF16)
        o_ref[...] = x

    h = h_scr[...]
    g = jnp.dot(h, wg_ref[...], preferred_element_type=F32)
    u = jnp.dot(h, wu_ref[...], preferred_element_type=F32)
    a = (g * jax.nn.sigmoid(g) * u).astype(BF16)
    o_ref[...] += jnp.dot(a, wd_ref[...], preferred_element_type=F32)

    if final_norm:
        @pl.when(f == pl.num_programs(1) - 1)
        def _():
            o_ref[...] = _rmsnorm(o_ref[...], fw_ref[...])


def _ffn(x, nw, wg, wu, wd, fw, *, bm, bf, final_norm):
    M, D = x.shape
    FF = wg.shape[1]
    assert FF % bf == 0 and M % bm == 0
    return pl.pallas_call(
        functools.partial(_ffn_kernel, final_norm=final_norm),
        grid=(M // bm, FF // bf),
        in_specs=[
            pl.BlockSpec((bm, D), lambda m, f: (m, 0)),
            pl.BlockSpec((1, D), lambda m, f: (0, 0)),
            pl.BlockSpec((D, bf), lambda m, f: (0, f)),
            pl.BlockSpec((D, bf), lambda m, f: (0, f)),
            pl.BlockSpec((bf, D), lambda m, f: (f, 0)),
            pl.BlockSpec((1, D), lambda m, f: (0, 0)),
        ],
        out_specs=pl.BlockSpec((bm, D), lambda m, f: (m, 0)),
        out_shape=jax.ShapeDtypeStruct((M, D), F32),
        scratch_shapes=[pltpu.VMEM((bm, D), BF16)],
        compiler_params=pltpu.CompilerParams(
            dimension_semantics=("arbitrary", "arbitrary"), vmem_limit_bytes=VMEM_LIMIT),
        name="ffn",
    )(x, nw, wg, wu, wd, fw)


def _pool_kernel(x_ref, nw_ref, wp_ref, ps_ref, hist_ref, o_ref, pst_ref, ext_scr, band_scr,
                 *, bm, tps, stride, pos_base):
    m = pl.program_id(0)
    hs = POOL_HIST * stride
    base = _halo_base(hs)
    rb = band_scr.shape[1]
    nd = min(rb, base)
    assert bm % rb == 0
    assert nd == rb or (stride == 1 and nd % SUBLANES == 0 and nd >= max(POOL_WINDOWS))
    grp = wp_ref.shape[1]
    first = (m % tps) == 0

    if nd < rb:
        @pl.when(m == 0)
        def _():
            lag = (lax.broadcasted_iota(jnp.int32, (rb, rb), 0)
                   - lax.broadcasted_iota(jnp.int32, (rb, rb), 1))
            for gi, w in enumerate(POOL_WINDOWS):
                band_scr[gi] = jnp.where((lag >= 0) & (lag < w), 1.0, 0.0).astype(BF16)

    @pl.when(first)
    def _():
        ext_scr[base - hs:base, :] = hist_ref[...]

    @pl.when(jnp.logical_not(first))
    def _():
        ext_scr[base - hs:base, :] = ext_scr[base + bm - hs:base + bm, :]

    ext_scr[base:base + bm, :] = _rmsnorm(x_ref[...], nw_ref[...])
    for r0 in range(0, bm, rb):
        r = (m % tps) * bm + r0 + lax.broadcasted_iota(jnp.int32, (nd, 1), 0)
        pos = pos_base + (r if stride == 1 else lax.shift_right_logical(r, stride.bit_length() - 1))
        for gi, w in enumerate(POOL_WINDOWS):
            cols = slice(gi * grp, (gi + 1) * grp)
            hb = ext_scr[base + r0:base + r0 + rb, cols]
            tot = hb[:nd]
            for kk in range(1, w):
                off = base + r0 - kk * stride
                tot = tot + ext_scr[off:off + nd, cols]
            d = tot / jnp.minimum(pos + 1, w).astype(F32) - hb[:nd]
            if nd < rb:
                hi = hb.astype(BF16)
                lo = (hb - hi.astype(F32)).astype(BF16)
                band = band_scr[gi]
                full = (jnp.dot(band, hi, preferred_element_type=F32)
                        + jnp.dot(band, lo, preferred_element_type=F32))
                d = jnp.concatenate([d, full[nd:] * (1.0 / w) - hb[nd:]], axis=0)
            out = jnp.dot(d.astype(BF16), wp_ref[gi], preferred_element_type=F32)
            o_ref[r0:r0 + rb, cols] = x_ref[r0:r0 + rb, cols] + out * ps_ref[:, cols]
    pst_ref[...] = ext_scr[base + bm - hs:base + bm, :]


def _pool(x, nw, w_pool, pool_scale, hist, *, layer, bm, tps, stride, pos_base):
    M, D = x.shape
    n_seq = hist.shape[0]
    hs = POOL_HIST * stride
    assert stride & (stride - 1) == 0 and hist.shape == (n_seq, hs, D) and M % bm == 0
    base = _halo_base(hs)
    _, ng, grp, _ = w_pool.shape
    assert ng == len(POOL_WINDOWS) and ng * grp == D
    kern = functools.partial(_pool_kernel, bm=bm, tps=tps, stride=stride, pos_base=pos_base)
    return pl.pallas_call(
        kern,
        grid=(M // bm,),
        in_specs=[
            pl.BlockSpec((bm, D), lambda m: (m, 0)),
            pl.BlockSpec((1, D), lambda m: (0, 0)),
            pl.BlockSpec((None, ng, grp, grp), lambda m: (layer, 0, 0, 0)),
            pl.BlockSpec((1, D), lambda m: (0, 0)),
            pl.BlockSpec((None, hs, D), lambda m: (m // tps, 0, 0)),
        ],
        out_specs=[
            pl.BlockSpec((bm, D), lambda m: (m, 0)),
            pl.BlockSpec((None, hs, D), lambda m: (m // tps, 0, 0)),
        ],
        out_shape=[
            jax.ShapeDtypeStruct((M, D), F32),
            jax.ShapeDtypeStruct((n_seq, hs, D), F32),
        ],
        scratch_shapes=[pltpu.VMEM((base + bm, D), F32),
                        pltpu.VMEM((ng, min(bm, MXU_DIM), min(bm, MXU_DIM)), BF16)],
        compiler_params=pltpu.CompilerParams(
            dimension_semantics=("arbitrary",), vmem_limit_bytes=VMEM_LIMIT),
        name="pool",
    )(x, nw, w_pool, pool_scale, hist)


def _head_match(n_rows, n_cols, n_heads, t_pad):
    rh = lax.broadcasted_iota(jnp.int32, (n_rows, n_cols), 0) // t_pad
    ch = lax.broadcasted_iota(jnp.int32, (n_rows, n_cols), 1) % n_heads
    return rh == ch


def _fold_heads(x, n_heads, t_pad):
    out = x[0:t_pad]
    for hh in range(1, n_heads):
        out = out + x[hh * t_pad:(hh + 1) * t_pad]
    return out


def _lane_class(x, n_heads, op):
    sh = n_heads
    while sh < x.shape[-1]:
        x = op(x, pltpu.roll(x, sh, axis=x.ndim - 1))
        sh *= 2
    return x


def _lane_tiles(x, op):
    out = x[:, 0:LANES]
    for t in range(1, x.shape[1] // LANES):
        out = op(out, x[:, t * LANES:(t + 1) * LANES])
    return out


def _sa_logits_kernel(pt_ref, q_ref, *refs, n_pages, n_heads, t_pad, pages_per_block):
    k_refs = refs[:n_pages]
    r_ref, g_ref = refs[n_pages:]
    qa = q_ref[...]
    rows = k_refs[0].shape[0] * n_heads
    match = _head_match(qa.shape[0], rows, n_heads, t_pad)
    for pg in range(n_pages):
        k2 = k_refs[pg][...].reshape(rows, HEAD_DIM).astype(BF16)
        l2 = lax.dot_general(qa, k2, _NT, preferred_element_type=F32)
        r_ref[pg] = _fold_heads(jnp.where(match, l2, 0.0), n_heads, t_pad)
    for bb in range(n_pages // pages_per_block):
        tot = _lane_tiles(r_ref[bb * pages_per_block], jnp.add)
        for pg in range(1, pages_per_block):
            tot = tot + _lane_tiles(r_ref[bb * pages_per_block + pg], jnp.add)
        g_ref[bb] = _lane_class(tot, n_heads, jnp.add)


def _sa_values_kernel(pt_ref, r_ref, g_ref, q_ref, kn_ref, vn_ref, *refs,
                      n_pages, n_heads, t_pad, pages_per_block):
    v_refs = refs[:n_pages]
    o_ref, p_scr, sel_scr, l_scr, acc_scr = refs[n_pages:]
    c = pl.program_id(1)
    n_blocks = g_ref.shape[0]
    rows = v_refs[0].shape[0] * n_heads
    n_rows = q_ref.shape[0]
    match = _head_match(n_rows, rows, n_heads, t_pad)

    def spread(p):
        return jnp.where(match[:, :p.shape[1]], jnp.concatenate([p] * n_heads, axis=0), 0.0).astype(BF16)

    @pl.when(c == 0)
    def _():
        gates = g_ref[...]
        idx = lax.broadcasted_iota(jnp.int32, gates.shape, 0)
        sel = jnp.zeros(gates.shape, F32)
        for _ in range(MOBA_TOPK):
            mx = jnp.max(gates, axis=0, keepdims=True)
            first = jnp.min(jnp.where(gates == mx, idx, n_blocks), axis=0, keepdims=True)
            hit = idx == first
            sel = jnp.where(hit, 1.0, sel)
            gates = jnp.where(hit, -jnp.inf, gates)
        sel_scr[...] = sel

        own_w = kn_ref.shape[0]
        lo = lax.dot_general(q_ref[...], kn_ref[...].astype(BF16), _NT, preferred_element_type=F32)
        ro = _fold_heads(jnp.where(match[:, :own_w], lo, 0.0), n_heads, t_pad)
        t_key = lax.broadcasted_iota(jnp.int32, ro.shape, 1) // n_heads
        t_qry = lax.broadcasted_iota(jnp.int32, ro.shape, 0)
        ro = jnp.where(t_key <= t_qry, ro, NEG_INF)

        def max_body(b, mx):
            here = _lane_tiles(r_ref[b * pages_per_block], jnp.maximum)
            for pg in range(1, pages_per_block):
                here = jnp.maximum(here, _lane_tiles(r_ref[b * pages_per_block + pg], jnp.maximum))
            return jnp.maximum(mx, jnp.where(sel_scr[b] > 0.0, here, NEG_INF))

        mx = lax.fori_loop(0, n_blocks, max_body, ro)
        mx = _lane_class(mx, n_heads, jnp.maximum)
        width = r_ref.shape[2]
        mx_w = jnp.concatenate([mx] * (width // LANES), axis=1)

        def exp_body(b, tot):
            keep = jnp.concatenate([sel_scr[b]] * (width // LANES), axis=1) > 0.0
            for pg in range(pages_per_block):
                p = jnp.where(keep, jnp.exp(r_ref[b * pages_per_block + pg] - mx_w), 0.0)
                p_scr[b * pages_per_block + pg] = p
                tot = tot + _lane_tiles(p, jnp.add)
            return tot

        po = jnp.exp(ro - mx)
        tot = lax.fori_loop(0, n_blocks, exp_body, po)
        l_scr[...] = _lane_class(tot, n_heads, jnp.add)
        acc_scr[...] = jnp.dot(spread(po), vn_ref[...].astype(BF16), preferred_element_type=F32)

    acc = acc_scr[...]
    for pg in range(n_pages):
        v2 = v_refs[pg][...].reshape(rows, HEAD_DIM).astype(BF16)
        acc = acc + jnp.dot(spread(p_scr[c * n_pages + pg]), v2, preferred_element_type=F32)
    acc_scr[...] = acc

    @pl.when(c == pl.num_programs(1) - 1)
    def _():
        l = l_scr[...]
        den = jnp.concatenate([l[:, hh:hh + 1] for hh in range(n_heads)], axis=0)
        o_ref[...] = acc_scr[...] / den


def _sample_attention(q_all, k_new, v_new, cache_k, cache_v, page_table, *, layer, n_heads, t_pad):
    n_dec, n_pt = page_table.shape
    page = cache_k.shape[2]
    ppb = MOBA_BLOCK // page
    P = PAGES_PER_STEP
    assert MOBA_BLOCK % page == 0 and n_pt % P == 0 and P % ppb == 0
    n_steps = n_pt // P
    n_blocks = n_pt // ppb
    width = page * n_heads
    n_rows = n_heads * t_pad

    def page_spec(i):
        return pl.BlockSpec((None, None, page, n_heads, HEAD_DIM),
                            lambda b, c, pt: (layer, pt[b, c * P + i], 0, 0, 0))

    params = pltpu.CompilerParams(dimension_semantics=("arbitrary", "arbitrary"),
                                  vmem_limit_bytes=VMEM_LIMIT)
    common = dict(n_pages=P, n_heads=n_heads, t_pad=t_pad, pages_per_block=ppb)
    logits, gates = pl.pallas_call(
        functools.partial(_sa_logits_kernel, **common),
        grid_spec=pltpu.PrefetchScalarGridSpec(
            num_scalar_prefetch=1,
            grid=(n_dec, n_steps),
            in_specs=[pl.BlockSpec((None, n_rows, HEAD_DIM), lambda b, c, pt: (b, 0, 0))]
                     + [page_spec(i) for i in range(P)],
            out_specs=[
                pl.BlockSpec((None, P, t_pad, width), lambda b, c, pt: (b, c, 0, 0)),
                pl.BlockSpec((None, P // ppb, t_pad, LANES), lambda b, c, pt: (b, c, 0, 0)),
            ],
        ),
        out_shape=[
            jax.ShapeDtypeStruct((n_dec, n_pt, t_pad, width), F32),
            jax.ShapeDtypeStruct((n_dec, n_blocks, t_pad, LANES), F32),
        ],
        compiler_params=params,
        name="sample_logits",
    )(page_table, q_all, *([cache_k] * P))

    own = k_new.shape[1]
    return pl.pallas_call(
        functools.partial(_sa_values_kernel, **common),
        grid_spec=pltpu.PrefetchScalarGridSpec(
            num_scalar_prefetch=1,
            grid=(n_dec, n_steps),
            in_specs=[
                pl.BlockSpec((None, n_pt, t_pad, width), lambda b, c, pt: (b, 0, 0, 0)),
                pl.BlockSpec((None, n_blocks, t_pad, LANES), lambda b, c, pt: (b, 0, 0, 0)),
                pl.BlockSpec((None, n_rows, HEAD_DIM), lambda b, c, pt: (b, 0, 0)),
                pl.BlockSpec((None, own, HEAD_DIM), lambda b, c, pt: (b, 0, 0)),
                pl.BlockSpec((None, own, HEAD_DIM), lambda b, c, pt: (b, 0, 0)),
            ] + [page_spec(i) for i in range(P)],
            out_specs=pl.BlockSpec((None, n_rows, HEAD_DIM), lambda b, c, pt: (b, 0, 0)),
            scratch_shapes=[
                pltpu.VMEM((n_pt, t_pad, width), F32),
                pltpu.VMEM((n_blocks, t_pad, LANES), F32),
                pltpu.VMEM((t_pad, LANES), F32),
                pltpu.VMEM((n_rows, HEAD_DIM), F32),
            ],
        ),
        out_shape=jax.ShapeDtypeStruct((n_dec, n_rows, HEAD_DIM), F32),
        compiler_params=params,
        name="sample_values",
    )(page_table, logits, gates, q_all, k_new, v_new, *([cache_v] * P))


def _rope_tables(pos):
    half = HEAD_DIM // 2
    inv = ROPE_THETA ** (-jnp.arange(half, dtype=F32) / half)
    ang = pos.astype(F32)[:, None] * inv[None, :]
    cos, sin = jnp.cos(ang), jnp.sin(ang)
    return jnp.concatenate([cos, cos], axis=1), jnp.concatenate([-sin, sin], axis=1)


def kernel(x_prompt, x_sample, cache_k, cache_v, page_table, state_conv, state_pool,
           norm_mix, norm_ffn, norm_final, w_in, conv_w, w_o, w_pool, pool_scale,
           w_gate, w_up, w_down):
    n_batch, n_seq, d = x_prompt.shape
    n_dec, n_new, _ = x_sample.shape
    depth = norm_mix.shape[0]
    seg = conv_w.shape[2]
    n_heads = seg // HEAD_DIM
    page = cache_k.shape[2]
    past_len = page_table.shape[1] * page
    assert n_dec == SUBLANES and n_new <= SUBLANES and past_len % MOBA_BLOCK == 0
    assert cache_k.shape[3] == n_heads and cache_k.shape[4] == HEAD_DIM
    q_scale = HEAD_DIM ** -0.5
    q_scale_log2 = q_scale * LOG2_E
    t_pad = SUBLANES
    own_keys = LANES // n_heads
    assert n_new <= own_keys

    w_in_b, w_pool_b = w_in.astype(BF16), w_pool.astype(BF16)
    ffn_bf16 = {}
    row = lambda a: a.reshape(1, -1)

    xp = x_prompt.reshape(n_batch * n_seq, d)
    xs = x_sample.transpose(1, 0, 2).reshape(n_new * n_dec, d)
    ms = n_new * n_dec
    bm = ROW_TILE
    assert n_seq % bm == 0 and (n_batch * n_seq) % FFN_ROW_TILE == 0
    tps = n_seq // bm
    cos_p, sin_p = _rope_tables(jnp.arange(n_seq, dtype=jnp.int32))
    pos_s = past_len + jnp.repeat(jnp.arange(n_new, dtype=jnp.int32), n_dec)
    cos_s, sin_s = _rope_tables(pos_s)

    def tm(a):
        return a.transpose(1, 0, 2).reshape(1, a.shape[1] * n_dec, a.shape[2])

    def untm(a, rows):
        return a.reshape(rows, n_dec, a.shape[-1]).transpose(1, 0, 2)

    k_p, v_p, k_s, v_s, conv_p, conv_s, pool_p, pool_s = [], [], [], [], [], [], [], []
    for layer in range(depth):
        if layer % 2 == 0:
            e = layer // 2
            nxt = layer + 1
            ahead = nxt < depth
            q, k, v, yc, cst, r_in = _inproj(
                xp, row(norm_mix[layer]), w_in_b, cos_p, sin_p, conv_w[e],
                jnp.zeros((n_batch, CONV_HIST, seg), F32),
                layer=e, bm=bm, tps=tps, stride=1, q_scale=q_scale_log2,
                round_qkv=((w_gate, nxt),) if ahead else (),
                round_gconv=((w_o, e),) + (((w_up, nxt),) if ahead else ()))
            w_o_e = r_in.pop(len(r_in) - (2 if ahead else 1))
            attn, ffn_bf16[layer] = _moba_prompt(
                q, k, v, n_batch=n_batch, seq=n_seq, n_heads=n_heads,
                round_weights=tuple((w, layer) for w in (w_gate, w_up, w_down)))
            xp, r_out = _outproj(attn, yc, w_o_e, xp, bm=bm, bn=d,
                                 round_weights=((w_down, nxt),) if ahead else ())
            if ahead:
                ffn_bf16[nxt] = r_in + r_out
            xs, attn = lax.optimization_barrier((xs, attn))
            k_p.append(k.reshape(n_batch, n_seq, n_heads, HEAD_DIM))
            v_p.append(v.reshape(n_batch, n_seq, n_heads, HEAD_DIM))
            conv_p.append(cst)
            q, k, v, yc, cst, _ = _inproj(
                xs, row(norm_mix[layer]), w_in_b, cos_s, sin_s, conv_w[e], tm(state_conv[e]),
                layer=e, bm=ms, tps=1, stride=n_dec, q_scale=q_scale)
            heads = lambda a: a.reshape(n_new, n_dec, n_heads, HEAD_DIM)
            q_all = jnp.pad(heads(q).transpose(1, 2, 0, 3), ((0, 0), (0, 0), (0, t_pad - n_new), (0, 0)))
            q_all = q_all.reshape(n_dec, n_heads * t_pad, HEAD_DIM)
            k_new = heads(k).transpose(1, 0, 2, 3)
            v_new = heads(v).transpose(1, 0, 2, 3)
            own = lambda a: jnp.pad(a, ((0, 0), (0, own_keys - n_new), (0, 0), (0, 0))).reshape(
                n_dec, own_keys * n_heads, HEAD_DIM)
            o = _sample_attention(q_all, own(k_new), own(v_new), cache_k, cache_v, page_table,
                                  layer=e, n_heads=n_heads, t_pad=t_pad)
            o = o.reshape(n_dec, n_heads, t_pad, HEAD_DIM)[:, :, :n_new]
            attn = o.transpose(2, 0, 1, 3).reshape(ms, seg).astype(BF16)
            xs, _ = _outproj(attn, yc, w_o_e, xs, bm=ms, bn=d)
            k_s.append(k_new)
            v_s.append(v_new)
            conv_s.append(untm(cst[0], CONV_HIST))
        else:
            o_ = layer // 2
            xp, pst = _pool(xp, row(norm_mix[layer]), w_pool_b, row(pool_scale[o_]),
                            jnp.zeros((n_batch, POOL_HIST, d), F32),
                            layer=o_, bm=bm, tps=tps, stride=1, pos_base=0)
            pool_p.append(pst)
            xs, pst = _pool(xs, row(norm_mix[layer]), w_pool_b, row(pool_scale[o_]),
                            tm(state_pool[o_]), layer=o_, bm=ms, tps=1, stride=n_dec,
                            pos_base=past_len)
            pool_s.append(untm(pst[0], POOL_HIST))
        last = layer == depth - 1
        wg, wu, wd = ffn_bf16[layer]
        ffn = functools.partial(_ffn, nw=row(norm_ffn[layer]), wg=wg, wu=wu, wd=wd,
                                fw=row(norm_final), bf=FFN_FF_TILE, final_norm=last)
        xp = ffn(xp, bm=FFN_ROW_TILE)
        xs = ffn(xs, bm=ms)

    y_prompt = xp.reshape(n_batch, n_seq, d)
    y_sample = xs.reshape(n_new, n_dec, d).transpose(1, 0, 2)
    return (y_prompt, y_sample, jnp.stack(k_p), jnp.stack(v_p), jnp.stack(k_s), jnp.stack(v_s),
            jnp.stack(conv_p), jnp.stack(conv_s), jnp.stack(pool_p), jnp.stack(pool_s))
```

```python
import functools

import jax
import jax.numpy as jnp
from jax import lax
from jax.experimental import pallas as pl
from jax.experimental.pallas import tpu as pltpu

F32 = jnp.float32
BF16 = jnp.bfloat16

HEAD_DIM = 128
MOBA_BLOCK = 256
MOBA_TOPK = 3
CONV_K = 3
CONV_HIST = CONV_K - 1
ROPE_THETA = 10000.0
POOL_WINDOWS = (2, 4, 8, 16)
POOL_HIST = max(POOL_WINDOWS) - 1
RMS_EPS = 1e-6
NEG_INF = -1e30
LOG2_E = 1.4426950408889634

SUBLANES = 8
LANES = 128
BF16_ROWS = 16
MXU_DIM = 256
ROW_TILE = 512
FFN_ROW_TILE = 1024
FFN_FF_TILE = 512
VMEM_LIMIT = 56 * 1024 * 1024
PAGES_PER_STEP = 32

_NT = (((1,), (1,)), ((), ()))


def _rmsnorm(x, g):
    ms = jnp.mean(x * x, axis=-1, keepdims=True)
    return x * lax.rsqrt(ms + RMS_EPS) * g


def _halo_base(rows):
    return -(-rows // SUBLANES) * SUBLANES


class _CastPlan:
    def __init__(self, stacked, layer, n_steps):
        _, rows, cols = stacked.shape
        assert cols % LANES == 0 and rows % BF16_ROWS == 0
        self.count = max(d for d in range(1, min(n_steps, rows // BF16_ROWS) + 1)
                         if (rows // BF16_ROWS) % d == 0)
        self.block = (rows // self.count, cols)
        self.layer = layer
        self.shape = (rows, cols)

    def _index(self, step):
        return jnp.minimum(step, self.count - 1), 0

    def specs(self, step_of):
        def src(*g):
            return (self.layer,) + self._index(step_of(*g))
        def dst(*g):
            return self._index(step_of(*g))
        return pl.BlockSpec((None,) + self.block, src), pl.BlockSpec(self.block, dst)


def _cast_step(src_refs, dst_refs):
    for src, dst in zip(src_refs, dst_refs):
        dst[...] = src[...].astype(BF16)


def _with_rounding(body, n_in, n_out, n_cast):
    def kernel(*refs, **kw):
        a, b, c = n_in + n_cast, n_in + n_cast + n_out, n_in + 2 * n_cast + n_out
        _cast_step(refs[n_in:a], refs[b:c])
        return body(*refs[:n_in], *refs[a:b], *refs[c:], **kw)
    return kernel


def _rounding_specs(todo, n_steps, step_of):
    plans = [_CastPlan(w, layer, n_steps) for w, layer in todo]
    specs = [p.specs(step_of) for p in plans]
    return ([s for s, _ in specs], [d for _, d in specs],
            [jax.ShapeDtypeStruct(p.shape, BF16) for p in plans])


def _qkv_kernel(x_ref, nw_ref, w_ref, cos_ref, sin_ref, q_ref, k_ref, v_ref, *, n_heads, q_scale):
    seg = n_heads * HEAD_DIM
    h = _rmsnorm(x_ref[...], nw_ref[...]).astype(BF16)
    cos = cos_ref[...]
    sin = sin_ref[...]

    def rope(a, hh):
        part = a[:, hh * HEAD_DIM:(hh + 1) * HEAD_DIM]
        return part * cos + pltpu.roll(part, HEAD_DIM // 2, axis=1) * sin

    aq = jnp.dot(h, w_ref[:, 0:seg], preferred_element_type=F32)
    for hh in range(n_heads):
        q_ref[:, hh * HEAD_DIM:(hh + 1) * HEAD_DIM] = (rope(aq, hh) * q_scale).astype(BF16)
    ak = jnp.dot(h, w_ref[:, seg:2 * seg], preferred_element_type=F32)
    for hh in range(n_heads):
        k_ref[:, hh * HEAD_DIM:(hh + 1) * HEAD_DIM] = rope(ak, hh)
    v_ref[...] = jnp.dot(h, w_ref[:, 2 * seg:3 * seg], preferred_element_type=F32)


def _gconv_kernel(x_ref, nw_ref, w_ref, cw_ref, hist_ref, yc_ref, cst_ref, ue_scr,
                  *, bm, tps, stride):
    m = pl.program_id(0)
    seg = yc_ref.shape[1]
    hs = CONV_HIST * stride
    base = _halo_base(hs)
    h = _rmsnorm(x_ref[...], nw_ref[...]).astype(BF16)
    u = (jnp.dot(h, w_ref[:, seg:2 * seg], preferred_element_type=F32)
         * jnp.dot(h, w_ref[:, 2 * seg:3 * seg], preferred_element_type=F32))
    first = (m % tps) == 0

    @pl.when(first)
    def _():
        ue_scr[base - hs:base, :] = hist_ref[...]

    @pl.when(jnp.logical_not(first))
    def _():
        ue_scr[base - hs:base, :] = ue_scr[base + bm - hs:base + bm, :]

    ue_scr[base:base + bm, :] = u
    cw = cw_ref[...]
    conv = u * cw[CONV_K - 1:CONV_K, :]
    for j in range(CONV_K - 1):
        off = base - (CONV_K - 1 - j) * stride
        conv = conv + ue_scr[off:off + bm, :] * cw[j:j + 1, :]
    gate_b = jnp.dot(h, w_ref[:, 0:seg], preferred_element_type=F32)
    yc_ref[...] = (gate_b * conv).astype(BF16)
    cst_ref[...] = ue_scr[base + bm - hs:base + bm, :]


def _inproj(x, nw, w_in, cos, sin, conv_w, hist, *, layer, bm, tps, stride, q_scale,
            round_qkv=(), round_gconv=()):
    M, D = x.shape
    seg = conv_w.shape[1]
    assert w_in.shape[2] == 6 * seg and M % bm == 0 and (M // bm) % tps == 0
    n_seq = hist.shape[0]
    hs = CONV_HIST * stride
    assert hist.shape == (n_seq, hs, seg) and bm >= hs
    base = _halo_base(hs)
    params = pltpu.CompilerParams(dimension_semantics=("arbitrary",), vmem_limit_bytes=VMEM_LIMIT)
    rows = lambda width: pl.BlockSpec((bm, width), lambda m: (m, 0))
    half = lambda n: pl.BlockSpec((None, D, 3 * seg), lambda m: (layer, 0, n),
                                  pipeline_mode=pl.Buffered(1))
    norm = pl.BlockSpec((1, D), lambda m: (0, 0))
    table = pl.BlockSpec((bm, HEAD_DIM), lambda m: (m % tps, 0))
    n_steps = M // bm
    src, dst, shp = _rounding_specs(round_qkv, n_steps, lambda m: m)
    q, k, v, *rounded_a = pl.pallas_call(
        _with_rounding(functools.partial(_qkv_kernel, n_heads=seg // HEAD_DIM, q_scale=q_scale),
                       5, 3, len(round_qkv)),
        grid=(n_steps,),
        in_specs=[rows(D), norm, half(0), table, table] + src,
        out_specs=[rows(seg), rows(seg), rows(seg)] + dst,
        out_shape=[
            jax.ShapeDtypeStruct((M, seg), BF16),
            jax.ShapeDtypeStruct((M, seg), F32),
            jax.ShapeDtypeStruct((M, seg), F32),
        ] + shp,
        compiler_params=params,
        name="inproj_qkv",
    )(x, nw, w_in, cos, sin, *[w for w, _ in round_qkv])
    state = pl.BlockSpec((None, hs, seg), lambda m: (m // tps, 0, 0))
    src, dst, shp = _rounding_specs(round_gconv, n_steps, lambda m: m)
    yc, cst, *rounded_b = pl.pallas_call(
        _with_rounding(functools.partial(_gconv_kernel, bm=bm, tps=tps, stride=stride),
                       5, 2, len(round_gconv)),
        grid=(n_steps,),
        in_specs=[rows(D), norm, half(1), pl.BlockSpec((CONV_K, seg), lambda m: (0, 0)), state]
                 + src,
        out_specs=[rows(seg), state] + dst,
        out_shape=[
            jax.ShapeDtypeStruct((M, seg), BF16),
            jax.ShapeDtypeStruct((n_seq, hs, seg), F32),
        ] + shp,
        scratch_shapes=[pltpu.VMEM((base + bm, seg), F32)],
        compiler_params=params,
        name="inproj_gconv",
    )(x, nw, w_in, conv_w, hist, *[w for w, _ in round_gconv])
    return q, k, v, yc, cst, rounded_a + rounded_b


def _group_reduce(x, op):
    return op(x.reshape(x.shape[0] // SUBLANES, SUBLANES, x.shape[1]), axis=0)


def _moba_kernel(q_ref, k_ref, v_ref, o_ref, kb_scr, vt_scr, ks_scr, qt_scr, sel_scr, s_scr,
               *, nq):
    i = pl.program_id(2)
    blk = MOBA_BLOCK

    @pl.when(i == 0)
    def _():
        for j in range(nq):
            kf = k_ref[j * blk:(j + 1) * blk, :]
            kb_scr[j] = kf.astype(BF16)
            vt_scr[j] = v_ref[j * blk:(j + 1) * blk, :].T.astype(BF16)
            ks_scr[j:j + 1, :] = jnp.sum(kf, axis=0, keepdims=True)
        ks = ks_scr[...]
        ks_hi = ks.astype(BF16)
        ks_lo = (ks - ks_hi.astype(F32)).astype(BF16)
        blk_id = lax.broadcasted_iota(jnp.int32, (nq, blk), 0)
        for own in range(nq):
            qt = q_ref[own * blk:(own + 1) * blk, :].astype(F32).T.astype(BF16)
            qt_scr[own] = qt
            gate = (jnp.dot(ks_hi, qt, preferred_element_type=F32)
                    + jnp.dot(ks_lo, qt, preferred_element_type=F32))
            g = jnp.where(blk_id < own, gate, -jnp.inf)
            sel = jnp.zeros((nq, blk), jnp.bool_)
            for _ in range(min(MOBA_TOPK, own)):
                top = jnp.max(g, axis=0, keepdims=True)
                first = jnp.min(jnp.where(g == top, blk_id, nq), axis=0, keepdims=True)
                hit = blk_id == first
                sel = sel | hit
                g = jnp.where(hit, -jnp.inf, g)
            sel_scr[own] = jnp.where(sel, 1.0, 0.0)

    for pair in range(nq // 2):
        @pl.when(i == pair)
        def _(pair=pair):
            _moba_pair(pair, o_ref, kb_scr, vt_scr, qt_scr, sel_scr, s_scr, nq=nq)


def _moba_pair(pair, o_ref, kb_scr, vt_scr, qt_scr, sel_scr, s_scr, *, nq):
    blk = MOBA_BLOCK
    owns = (pair, nq - 1 - pair)

    tiles = [(w, j) for w, own in enumerate(owns) for j in range(own)]
    n_gen = len(tiles)

    mx = [None, None]
    for t, (w, j) in enumerate(tiles):
        s = jnp.dot(kb_scr[j], qt_scr[owns[w]], preferred_element_type=F32)
        s = jnp.where(sel_scr[owns[w], j:j + 1, :] > 0.0, s, NEG_INF)
        s_scr[t] = s
        cm = _group_reduce(s, jnp.max)
        mx[w] = cm if mx[w] is None else jnp.maximum(mx[w], cm)
    key_id = lax.broadcasted_iota(jnp.int32, (blk, blk), 0)
    qry_id = lax.broadcasted_iota(jnp.int32, (blk, blk), 1)
    for w, own in enumerate(owns):
        s = jnp.dot(kb_scr[own], qt_scr[own], preferred_element_type=F32)
        s = jnp.where(key_id <= qry_id, s, NEG_INF)
        s_scr[n_gen + w] = s
        cm = _group_reduce(s, jnp.max)
        mx[w] = cm if mx[w] is None else jnp.maximum(mx[w], cm)
    mrow = [jnp.max(m, axis=0, keepdims=True) for m in mx]

    acc = [None, None]
    den = [None, None]
    order = list(enumerate(tiles)) + [(n_gen + w, (w, own)) for w, own in enumerate(owns)]
    for t, (w, j) in order:
        p = jnp.exp2(s_scr[t] - mrow[w])
        cs = _group_reduce(p, jnp.sum)
        den[w] = cs if den[w] is None else den[w] + cs
        r = jnp.dot(vt_scr[j], p.astype(BF16), preferred_element_type=F32)
        acc[w] = r if acc[w] is None else acc[w] + r
    for w, own in enumerate(owns):
        tot = jnp.sum(den[w], axis=0, keepdims=True)
        o_ref[own * blk:(own + 1) * blk, :] = (acc[w] / tot).T.astype(BF16)


def _moba_prompt(q, k, v, *, n_batch, seq, n_heads, round_weights=()):
    blk = MOBA_BLOCK
    assert seq % (2 * blk) == 0
    nq = seq // blk
    width = n_heads * HEAD_DIM
    grid = (n_batch, n_heads, nq // 2)
    src, dst, shp = _rounding_specs(round_weights, grid[0] * grid[1] * grid[2],
                                    lambda b, h, i: (b * grid[1] + h) * grid[2] + i)
    col = pl.BlockSpec((None, seq, HEAD_DIM), lambda b, h, i: (b, 0, h))
    out, *rounded = pl.pallas_call(
        _with_rounding(functools.partial(_moba_kernel, nq=nq), 3, 1, len(round_weights)),
        grid=grid,
        in_specs=[col, col, col] + src,
        out_specs=[col] + dst,
        out_shape=[jax.ShapeDtypeStruct((n_batch, seq, width), BF16)] + shp,
        scratch_shapes=[
            pltpu.VMEM((nq, blk, HEAD_DIM), BF16),
            pltpu.VMEM((nq, HEAD_DIM, blk), BF16),
            pltpu.VMEM((nq, HEAD_DIM), F32),
            pltpu.VMEM((nq, HEAD_DIM, blk), BF16),
            pltpu.VMEM((nq, nq, blk), F32),
            pltpu.VMEM((nq + 1, blk, blk), F32),
        ],
        compiler_params=pltpu.CompilerParams(
            dimension_semantics=("arbitrary", "arbitrary", "arbitrary"),
            vmem_limit_bytes=VMEM_LIMIT),
        name="moba_prompt",
    )(q.reshape(n_batch, seq, width), k.reshape(n_batch, seq, width),
      v.reshape(n_batch, seq, width), *[w for w, _ in round_weights])
    return out.reshape(n_batch * seq, width), rounded


def _outproj_kernel(a_ref, c_ref, wt_ref, wb_ref, x_ref, o_ref):
    o_ref[...] = (x_ref[...]
                  + jnp.dot(a_ref[...], wt_ref[...], preferred_element_type=F32)
                  + jnp.dot(c_ref[...], wb_ref[...], preferred_element_type=F32))


def _outproj(a, c, w_o, x, *, bm, bn, round_weights=()):
    M, D = x.shape
    half = a.shape[1]
    assert w_o.shape == (2 * half, D) and c.shape == a.shape and D % bn == 0 and M % bm == 0
    w_mode = pl.Buffered(1) if bn == D else None
    grid = (M // bm, D // bn)
    src, dst, shp = _rounding_specs(round_weights, grid[0] * grid[1],
                                    lambda m, n: m * grid[1] + n)
    out, *rounded = pl.pallas_call(
        _with_rounding(_outproj_kernel, 5, 1, len(round_weights)),
        grid=grid,
        in_specs=[
            pl.BlockSpec((bm, half), lambda m, n: (m, 0)),
            pl.BlockSpec((bm, half), lambda m, n: (m, 0)),
            pl.BlockSpec((half, bn), lambda m, n: (0, n), pipeline_mode=w_mode),
            pl.BlockSpec((half, bn), lambda m, n: (1, n), pipeline_mode=w_mode),
            pl.BlockSpec((bm, bn), lambda m, n: (m, n)),
        ] + src,
        out_specs=[pl.BlockSpec((bm, bn), lambda m, n: (m, n))] + dst,
        out_shape=[jax.ShapeDtypeStruct((M, D), F32)] + shp,
        compiler_params=pltpu.CompilerParams(
            dimension_semantics=("arbitrary", "arbitrary"), vmem_limit_bytes=VMEM_LIMIT),
        name="outproj",
    )(a, c, w_o, w_o, x, *[w for w, _ in round_weights])
    return out, rounded


def _ffn_kernel(x_ref, nw_ref, wg_ref, wu_ref, wd_ref, fw_ref, o_ref, h_scr, *, final_norm):
    f = pl.program_id(1)

    @pl.when(f == 0)
    def _():
        x = x_ref[...]
        h_scr[...] = _rmsnorm(x, nw_ref[...]).astype(BF16)
        o_ref[...] = x

    h = h_scr[...]
    g = jnp.dot(h, wg_ref[...], preferred_element_type=F32)
    u = jnp.dot(h, wu_ref[...], preferred_element_type=F32)
    a = (g * jax.nn.sigmoid(g) * u).astype(BF16)
    o_ref[...] += jnp.dot(a, wd_ref[...], preferred_element_type=F32)

    if final_norm:
        @pl.when(f == pl.num_programs(1) - 1)
        def _():
            o_ref[...] = _rmsnorm(o_ref[...], fw_ref[...])


def _ffn(x, nw, wg, wu, wd, fw, *, bm, bf, final_norm):
    M, D = x.shape
    FF = wg.shape[1]
    assert FF % bf == 0 and M % bm == 0
    return pl.pallas_call(
        functools.partial(_ffn_kernel, final_norm=final_norm),
        grid=(M // bm, FF // bf),
        in_specs=[
            pl.BlockSpec((bm, D), lambda m, f: (m, 0)),
            pl.BlockSpec((1, D), lambda m, f: (0, 0)),
            pl.BlockSpec((D, bf), lambda m, f: (0, f)),
            pl.BlockSpec((D, bf), lambda m, f: (0, f)),
            pl.BlockSpec((bf, D), lambda m, f: (f, 0)),
            pl.BlockSpec((1, D), lambda m, f: (0, 0)),
        ],
        out_specs=pl.BlockSpec((bm, D), lambda m, f: (m, 0)),
        out_shape=jax.ShapeDtypeStruct((M, D), F32),
        scratch_shapes=[pltpu.VMEM((bm, D), BF16)],
        compiler_params=pltpu.CompilerParams(
            dimension_semantics=("arbitrary", "arbitrary"), vmem_limit_bytes=VMEM_LIMIT),
        name="ffn",
    )(x, nw, wg, wu, wd, fw)


def _pool_kernel(x_ref, nw_ref, wp_ref, ps_ref, hist_ref, o_ref, pst_ref, ext_scr, band_scr,
                 *, bm, tps, stride, pos_base):
    m = pl.program_id(0)
    hs = POOL_HIST * stride
    base = _halo_base(hs)
    rb = band_scr.shape[1]
    nd = min(rb, base)
    assert bm % rb == 0
    assert nd == rb or (stride == 1 and nd % SUBLANES == 0 and nd >= max(POOL_WINDOWS))
    grp = wp_ref.shape[1]
    first = (m % tps) == 0

    if nd < rb:
        @pl.when(m == 0)
        def _():
            lag = (lax.broadcasted_iota(jnp.int32, (rb, rb), 0)
                   - lax.broadcasted_iota(jnp.int32, (rb, rb), 1))
            for gi, w in enumerate(POOL_WINDOWS):
                band_scr[gi] = jnp.where((lag >= 0) & (lag < w), 1.0, 0.0).astype(BF16)

    @pl.when(first)
    def _():
        ext_scr[base - hs:base, :] = hist_ref[...]

    @pl.when(jnp.logical_not(first))
    def _():
        ext_scr[base - hs:base, :] = ext_scr[base + bm - hs:base + bm, :]

    ext_scr[base:base + bm, :] = _rmsnorm(x_ref[...], nw_ref[...])
    for r0 in range(0, bm, rb):
        r = (m % tps) * bm + r0 + lax.broadcasted_iota(jnp.int32, (nd, 1), 0)
        pos = pos_base + (r if stride == 1 else lax.shift_right_logical(r, stride.bit_length() - 1))
        for gi, w in enumerate(POOL_WINDOWS):
            cols = slice(gi * grp, (gi + 1) * grp)
            hb = ext_scr[base + r0:base + r0 + rb, cols]
            tot = hb[:nd]
            for kk in range(1, w):
                off = base + r0 - kk * stride
                tot = tot + ext_scr[off:off + nd, cols]
            d = tot / jnp.minimum(pos + 1, w).astype(F32) - hb[:nd]
            if nd < rb:
                hi = hb.astype(BF16)
                lo = (hb - hi.astype(F32)).astype(BF16)
                band = band_scr[gi]
                full = (jnp.dot(band, hi, preferred_element_type=F32)
                        + jnp.dot(band, lo, preferred_element_type=F32))
                d = jnp.concatenate([d, full[nd:] * (1.0 / w) - hb[nd:]], axis=0)
            out = jnp.dot(d.astype(BF16), wp_ref[gi], preferred_element_type=F32)
            o_ref[r0:r0 + rb, cols] = x_ref[r0:r0 + rb, cols] + out * ps_ref[:, cols]
    pst_ref[...] = ext_scr[base + bm - hs:base + bm, :]


def _pool(x, nw, w_pool, pool_scale, hist, *, layer, bm, tps, stride, pos_base):
    M, D = x.shape
    n_seq = hist.shape[0]
    hs = POOL_HIST * stride
    assert stride & (stride - 1) == 0 and hist.shape == (n_seq, hs, D) and M % bm == 0
    base = _halo_base(hs)
    _, ng, grp, _ = w_pool.shape
    assert ng == len(POOL_WINDOWS) and ng * grp == D
    kern = functools.partial(_pool_kernel, bm=bm, tps=tps, stride=stride, pos_base=pos_base)
    return pl.pallas_call(
        kern,
        grid=(M // bm,),
        in_specs=[
            pl.BlockSpec((bm, D), lambda m: (m, 0)),
            pl.BlockSpec((1, D), lambda m: (0, 0)),
            pl.BlockSpec((None, ng, grp, grp), lambda m: (layer, 0, 0, 0)),
            pl.BlockSpec((1, D), lambda m: (0, 0)),
            pl.BlockSpec((None, hs, D), lambda m: (m // tps, 0, 0)),
        ],
        out_specs=[
            pl.BlockSpec((bm, D), lambda m: (m, 0)),
            pl.BlockSpec((None, hs, D), lambda m: (m // tps, 0, 0)),
        ],
        out_shape=[
            jax.ShapeDtypeStruct((M, D), F32),
            jax.ShapeDtypeStruct((n_seq, hs, D), F32),
        ],
        scratch_shapes=[pltpu.VMEM((base + bm, D), F32),
                        pltpu.VMEM((ng, min(bm, MXU_DIM), min(bm, MXU_DIM)), BF16)],
        compiler_params=pltpu.CompilerParams(
            dimension_semantics=("arbitrary",), vmem_limit_bytes=VMEM_LIMIT),
        name="pool",
    )(x, nw, w_pool, pool_scale, hist)


def _head_match(n_rows, n_cols, n_heads, t_pad):
    rh = lax.broadcasted_iota(jnp.int32, (n_rows, n_cols), 0) // t_pad
    ch = lax.broadcasted_iota(jnp.int32, (n_rows, n_cols), 1) % n_heads
    return rh == ch


def _fold_heads(x, n_heads, t_pad):
    out = x[0:t_pad]
    for hh in range(1, n_heads):
        out = out + x[hh * t_pad:(hh + 1) * t_pad]
    return out


def _lane_class(x, n_heads, op):
    sh = n_heads
    while sh < x.shape[-1]:
        x = op(x, pltpu.roll(x, sh, axis=x.ndim - 1))
        sh *= 2
    return x


def _lane_tiles(x, op):
    out = x[:, 0:LANES]
    for t in range(1, x.shape[1] // LANES):
        out = op(out, x[:, t * LANES:(t + 1) * LANES])
    return out


def _sa_logits_kernel(pt_ref, q_ref, *refs, n_pages, n_heads, t_pad, pages_per_block):
    k_refs = refs[:n_pages]
    r_ref, g_ref = refs[n_pages:]
    qa = q_ref[...]
    rows = k_refs[0].shape[0] * n_heads
    match = _head_match(qa.shape[0], rows, n_heads, t_pad)
    for pg in range(n_pages):
        k2 = k_refs[pg][...].reshape(rows, HEAD_DIM).astype(BF16)
        l2 = lax.dot_general(qa, k2, _NT, preferred_element_type=F32)
        r_ref[pg] = _fold_heads(jnp.where(match, l2, 0.0), n_heads, t_pad)
    for bb in range(n_pages // pages_per_block):
        tot = _lane_tiles(r_ref[bb * pages_per_block], jnp.add)
        for pg in range(1, pages_per_block):
            tot = tot + _lane_tiles(r_ref[bb * pages_per_block + pg], jnp.add)
        g_ref[bb] = _lane_class(tot, n_heads, jnp.add)


def _sa_select_kernel(g_ref, sel_ref):
    gates = g_ref[...]
    n_blocks = gates.shape[0]
    idx = lax.broadcasted_iota(jnp.int32, gates.shape, 0)
    sel = jnp.zeros(gates.shape, F32)
    for _ in range(MOBA_TOPK):
        mx = jnp.max(gates, axis=0, keepdims=True)
        first = jnp.min(jnp.where(gates == mx, idx, n_blocks), axis=0, keepdims=True)
        hit = idx == first
        sel = jnp.where(hit, 1.0, sel)
        gates = jnp.where(hit, -jnp.inf, gates)
    sel_ref[...] = sel


def _sa_values_kernel(tbl_ref, r_ref, sel_ref, q_ref, kn_ref, vn_ref, *refs,
                      n_pages, n_heads, t_pad, pages_per_block):
    v_refs = refs[:n_pages]
    o_ref, p_scr, l_scr, acc_scr = refs[n_pages:]
    c = pl.program_id(1)
    n_blocks = sel_ref.shape[0]
    rows = v_refs[0].shape[0] * n_heads
    n_rows = q_ref.shape[0]
    match = _head_match(n_rows, rows, n_heads, t_pad)

    def spread(p):
        return jnp.where(match[:, :p.shape[1]], jnp.concatenate([p] * n_heads, axis=0), 0.0).astype(BF16)

    @pl.when(c == 0)
    def _():
        own_w = kn_ref.shape[0]
        lo = lax.dot_general(q_ref[...], kn_ref[...].astype(BF16), _NT, preferred_element_type=F32)
        ro = _fold_heads(jnp.where(match[:, :own_w], lo, 0.0), n_heads, t_pad)
        t_key = lax.broadcasted_iota(jnp.int32, ro.shape, 1) // n_heads
        t_qry = lax.broadcasted_iota(jnp.int32, ro.shape, 0)
        ro = jnp.where(t_key <= t_qry, ro, NEG_INF)

        def max_body(b, mx):
            here = _lane_tiles(r_ref[b * pages_per_block], jnp.maximum)
            for pg in range(1, pages_per_block):
                here = jnp.maximum(here, _lane_tiles(r_ref[b * pages_per_block + pg], jnp.maximum))
            return jnp.maximum(mx, jnp.where(sel_ref[b] > 0.0, here, NEG_INF))

        mx = lax.fori_loop(0, n_blocks, max_body, ro)
        mx = _lane_class(mx, n_heads, jnp.maximum)
        width = r_ref.shape[2]
        mx_w = jnp.concatenate([mx] * (width // LANES), axis=1)

        def exp_body(b, tot):
            keep = jnp.concatenate([sel_ref[b]] * (width // LANES), axis=1) > 0.0
            for pg in range(pages_per_block):
                p = jnp.where(keep, jnp.exp(r_ref[b * pages_per_block + pg] - mx_w), 0.0)
                p_scr[b * pages_per_block + pg] = p
                tot = tot + _lane_tiles(p, jnp.add)
            return tot

        po = jnp.exp(ro - mx)
        tot = lax.fori_loop(0, n_blocks, exp_body, po)
        l_scr[...] = _lane_class(tot, n_heads, jnp.add)
        acc_scr[...] = jnp.dot(spread(po), vn_ref[...].astype(BF16), preferred_element_type=F32)

    acc = acc_scr[...]
    for pg in range(n_pages):
        v2 = v_refs[pg][...].reshape(rows, HEAD_DIM).astype(BF16)
        acc = acc + jnp.dot(spread(p_scr[c * n_pages + pg]), v2, preferred_element_type=F32)
    acc_scr[...] = acc

    @pl.when(c == pl.num_programs(1) - 1)
    def _():
        l = l_scr[...]
        den = jnp.concatenate([l[:, hh:hh + 1] for hh in range(n_heads)], axis=0)
        o_ref[...] = acc_scr[...] / den


def _sample_attention(q_all, k_new, v_new, cache_k, cache_v, page_table, *, layer, n_heads, t_pad):
    n_dec, n_pt = page_table.shape
    page = cache_k.shape[2]
    ppb = MOBA_BLOCK // page
    P = PAGES_PER_STEP
    assert MOBA_BLOCK % page == 0 and n_pt % P == 0 and P % ppb == 0
    n_steps = n_pt // P
    n_blocks = n_pt // ppb
    width = page * n_heads
    n_rows = n_heads * t_pad

    def page_spec(i):
        return pl.BlockSpec((None, None, page, n_heads, HEAD_DIM),
                            lambda b, c, pt: (layer, pt[b, c * P + i], 0, 0, 0))

    params = pltpu.CompilerParams(dimension_semantics=("arbitrary", "arbitrary"),
                                  vmem_limit_bytes=VMEM_LIMIT)
    common = dict(n_pages=P, n_heads=n_heads, t_pad=t_pad, pages_per_block=ppb)
    logits, gates = pl.pallas_call(
        functools.partial(_sa_logits_kernel, **common),
        grid_spec=pltpu.PrefetchScalarGridSpec(
            num_scalar_prefetch=1,
            grid=(n_dec, n_steps),
            in_specs=[pl.BlockSpec((None, n_rows, HEAD_DIM), lambda b, c, pt: (b, 0, 0))]
                     + [page_spec(i) for i in range(P)],
            out_specs=[
                pl.BlockSpec((None, P, t_pad, width), lambda b, c, pt: (b, c, 0, 0)),
                pl.BlockSpec((None, P // ppb, t_pad, LANES), lambda b, c, pt: (b, c, 0, 0)),
            ],
        ),
        out_shape=[
            jax.ShapeDtypeStruct((n_dec, n_pt, t_pad, width), F32),
            jax.ShapeDtypeStruct((n_dec, n_blocks, t_pad, LANES), F32),
        ],
        compiler_params=params,
        name="sample_logits",
    )(page_table, q_all, *([cache_k] * P))

    blocks = pl.BlockSpec((None, n_blocks, t_pad, LANES), lambda b: (b, 0, 0, 0))
    sel = pl.pallas_call(
        _sa_select_kernel,
        grid=(n_dec,),
        in_specs=[blocks],
        out_specs=blocks,
        out_shape=jax.ShapeDtypeStruct(gates.shape, F32),
        name="sample_select",
    )(gates)

    need = jnp.repeat(jnp.max(sel, axis=(2, 3)) > 0.0, ppb, axis=1)
    need = need.reshape(n_dec * n_steps, P)
    phys = page_table.reshape(n_dec * n_steps, P)
    step_id = lax.broadcasted_iota(jnp.int32, need.shape, 0)
    last = lax.cummax(jnp.where(need, step_id, 0), axis=0)
    table = jnp.take_along_axis(phys, last, axis=0)

    def v_spec(i):
        return pl.BlockSpec((None, None, page, n_heads, HEAD_DIM),
                            lambda b, c, tbl: (layer, tbl[b * n_steps + c, i], 0, 0, 0))

    own = k_new.shape[1]
    return pl.pallas_call(
        functools.partial(_sa_values_kernel, **common),
        grid_spec=pltpu.PrefetchScalarGridSpec(
            num_scalar_prefetch=1,
            grid=(n_dec, n_steps),
            in_specs=[
                pl.BlockSpec((None, n_pt, t_pad, width), lambda b, c, tbl: (b, 0, 0, 0)),
                pl.BlockSpec((None, n_blocks, t_pad, LANES), lambda b, c, tbl: (b, 0, 0, 0)),
                pl.BlockSpec((None, n_rows, HEAD_DIM), lambda b, c, tbl: (b, 0, 0)),
                pl.BlockSpec((None, own, HEAD_DIM), lambda b, c, tbl: (b, 0, 0)),
                pl.BlockSpec((None, own, HEAD_DIM), lambda b, c, tbl: (b, 0, 0)),
            ] + [v_spec(i) for i in range(P)],
            out_specs=pl.BlockSpec((None, n_rows, HEAD_DIM), lambda b, c, tbl: (b, 0, 0)),
            scratch_shapes=[
                pltpu.VMEM((n_pt, t_pad, width), F32),
                pltpu.VMEM((t_pad, LANES), F32),
                pltpu.VMEM((n_rows, HEAD_DIM), F32),
            ],
        ),
        out_shape=jax.ShapeDtypeStruct((n_dec, n_rows, HEAD_DIM), F32),
        compiler_params=params,
        name="sample_values",
    )(table, logits, sel, q_all, k_new, v_new, *([cache_v] * P))


def _rope_tables(pos):
    half = HEAD_DIM // 2
    inv = ROPE_THETA ** (-jnp.arange(half, dtype=F32) / half)
    ang = pos.astype(F32)[:, None] * inv[None, :]
    cos, sin = jnp.cos(ang), jnp.sin(ang)
    return jnp.concatenate([cos, cos], axis=1), jnp.concatenate([-sin, sin], axis=1)


def kernel(x_prompt, x_sample, cache_k, cache_v, page_table, state_conv, state_pool,
           norm_mix, norm_ffn, norm_final, w_in, conv_w, w_o, w_pool, pool_scale,
           w_gate, w_up, w_down):
    n_batch, n_seq, d = x_prompt.shape
    n_dec, n_new, _ = x_sample.shape
    depth = norm_mix.shape[0]
    seg = conv_w.shape[2]
    n_heads = seg // HEAD_DIM
    page = cache_k.shape[2]
    past_len = page_table.shape[1] * page
    assert n_dec == SUBLANES and n_new <= SUBLANES and past_len % MOBA_BLOCK == 0
    assert cache_k.shape[3] == n_heads and cache_k.shape[4] == HEAD_DIM
    q_scale = HEAD_DIM ** -0.5
    q_scale_log2 = q_scale * LOG2_E
    t_pad = SUBLANES
    own_keys = LANES // n_heads
    assert n_new <= own_keys

    w_in_b, w_pool_b = w_in.astype(BF16), w_pool.astype(BF16)
    ffn_bf16 = {}
    row = lambda a: a.reshape(1, -1)

    xp = x_prompt.reshape(n_batch * n_seq, d)
    xs = x_sample.transpose(1, 0, 2).reshape(n_new * n_dec, d)
    ms = n_new * n_dec
    bm = ROW_TILE
    assert n_seq % bm == 0 and (n_batch * n_seq) % FFN_ROW_TILE == 0
    tps = n_seq // bm
    cos_p, sin_p = _rope_tables(jnp.arange(n_seq, dtype=jnp.int32))
    pos_s = past_len + jnp.repeat(jnp.arange(n_new, dtype=jnp.int32), n_dec)
    cos_s, sin_s = _rope_tables(pos_s)

    def tm(a):
        return a.transpose(1, 0, 2).reshape(1, a.shape[1] * n_dec, a.shape[2])

    def untm(a, rows):
        return a.reshape(rows, n_dec, a.shape[-1]).transpose(1, 0, 2)

    k_p, v_p, k_s, v_s, conv_p, conv_s, pool_p, pool_s = [], [], [], [], [], [], [], []
    for layer in range(depth):
        if layer % 2 == 0:
            e = layer // 2
            nxt = layer + 1
            ahead = nxt < depth
            q, k, v, yc, cst, r_in = _inproj(
                xp, row(norm_mix[layer]), w_in_b, cos_p, sin_p, conv_w[e],
                jnp.zeros((n_batch, CONV_HIST, seg), F32),
                layer=e, bm=bm, tps=tps, stride=1, q_scale=q_scale_log2,
                round_qkv=((w_gate, nxt),) if ahead else (),
                round_gconv=((w_o, e),) + (((w_up, nxt),) if ahead else ()))
            w_o_e = r_in.pop(len(r_in) - (2 if ahead else 1))
            attn, ffn_bf16[layer] = _moba_prompt(
                q, k, v, n_batch=n_batch, seq=n_seq, n_heads=n_heads,
                round_weights=tuple((w, layer) for w in (w_gate, w_up, w_down)))
            xp, r_out = _outproj(attn, yc, w_o_e, xp, bm=bm, bn=d,
                                 round_weights=((w_down, nxt),) if ahead else ())
            if ahead:
                ffn_bf16[nxt] = r_in + r_out
            xs, attn = lax.optimization_barrier((xs, attn))
            k_p.append(k.reshape(n_batch, n_seq, n_heads, HEAD_DIM))
            v_p.append(v.reshape(n_batch, n_seq, n_heads, HEAD_DIM))
            conv_p.append(cst)
            q, k, v, yc, cst, _ = _inproj(
                xs, row(norm_mix[layer]), w_in_b, cos_s, sin_s, conv_w[e], tm(state_conv[e]),
                layer=e, bm=ms, tps=1, stride=n_dec, q_scale=q_scale)
            heads = lambda a: a.reshape(n_new, n_dec, n_heads, HEAD_DIM)
            q_all = jnp.pad(heads(q).transpose(1, 2, 0, 3), ((0, 0), (0, 0), (0, t_pad - n_new), (0, 0)))
            q_all = q_all.reshape(n_dec, n_heads * t_pad, HEAD_DIM)
            k_new = heads(k).transpose(1, 0, 2, 3)
            v_new = heads(v).transpose(1, 0, 2, 3)
            own = lambda a: jnp.pad(a, ((0, 0), (0, own_keys - n_new), (0, 0), (0, 0))).reshape(
                n_dec, own_keys * n_heads, HEAD_DIM)
            o = _sample_attention(q_all, own(k_new), own(v_new), cache_k, cache_v, page_table,
                                  layer=e, n_heads=n_heads, t_pad=t_pad)
            o = o.reshape(n_dec, n_heads, t_pad, HEAD_DIM)[:, :, :n_new]
            attn = o.transpose(2, 0, 1, 3).reshape(ms, seg).astype(BF16)
            xs, _ = _outproj(attn, yc, w_o_e, xs, bm=ms, bn=d)
            k_s.append(k_new)
            v_s.append(v_new)
            conv_s.append(untm(cst[0], CONV_HIST))
        else:
            o_ = layer // 2
            xp, pst = _pool(xp, row(norm_mix[layer]), w_pool_b, row(pool_scale[o_]),
                            jnp.zeros((n_batch, POOL_HIST, d), F32),
                            layer=o_, bm=bm, tps=tps, stride=1, pos_base=0)
            pool_p.append(pst)
            xs, pst = _pool(xs, row(norm_mix[layer]), w_pool_b, row(pool_scale[o_]),
                            tm(state_pool[o_]), layer=o_, bm=ms, tps=1, stride=n_dec,
                            pos_base=past_len)
            pool_s.append(untm(pst[0], POOL_HIST))
        last = layer == depth - 1
        wg, wu, wd = ffn_bf16[layer]
        ffn = functools.partial(_ffn, nw=row(norm_ffn[layer]), wg=wg, wu=wu, wd=wd,
                                fw=row(norm_final), bf=FFN_FF_TILE, final_norm=last)
        xp = ffn(xp, bm=FFN_ROW_TILE)
        xs = ffn(xs, bm=ms)

    y_prompt = xp.reshape(n_batch, n_seq, d)
    y_sample = xs.reshape(n_new, n_dec, d).transpose(1, 0, 2)
    return (y_prompt, y_sample, jnp.stack(k_p), jnp.stack(v_p), jnp.stack(k_s), jnp.stack(v_s),
            jnp.stack(conv_p), jnp.stack(conv_s), jnp.stack(pool_p), jnp.stack(pool_s))
```

```python
import functools

import jax
import jax.numpy as jnp
from jax import lax
from jax.experimental import pallas as pl
from jax.experimental.pallas import tpu as pltpu

F32 = jnp.float32
BF16 = jnp.bfloat16

HEAD_DIM = 128
MOBA_BLOCK = 256
MOBA_TOPK = 3
CONV_K = 3
CONV_HIST = CONV_K - 1
ROPE_THETA = 10000.0
POOL_WINDOWS = (2, 4, 8, 16)
POOL_HIST = max(POOL_WINDOWS) - 1
RMS_EPS = 1e-6
NEG_INF = -1e30
LOG2_E = 1.4426950408889634

SUBLANES = 8
LANES = 128
BF16_ROWS = 16
MXU_DIM = 256
ROW_TILE = 512
FFN_ROW_TILE = 1024
FFN_FF_TILE = 512
SAMPLE_FF_TILE = 1408
VMEM_LIMIT = 56 * 1024 * 1024
PAGES_PER_STEP = 32

_NT = (((1,), (1,)), ((), ()))


def _rmsnorm(x, g):
    ms = jnp.mean(x * x, axis=-1, keepdims=True)
    return x * lax.rsqrt(ms + RMS_EPS) * g


def _halo_base(rows):
    return -(-rows // SUBLANES) * SUBLANES


class _CastPlan:
    def __init__(self, stacked, layer, n_steps):
        _, rows, cols = stacked.shape
        assert cols % LANES == 0 and rows % BF16_ROWS == 0
        self.count = max(d for d in range(1, min(n_steps, rows // BF16_ROWS) + 1)
                         if (rows // BF16_ROWS) % d == 0)
        self.block = (rows // self.count, cols)
        self.layer = layer
        self.shape = (rows, cols)

    def _index(self, step):
        return jnp.minimum(step, self.count - 1), 0

    def specs(self, step_of):
        def src(*g):
            return (self.layer,) + self._index(step_of(*g))
        def dst(*g):
            return self._index(step_of(*g))
        return pl.BlockSpec((None,) + self.block, src), pl.BlockSpec(self.block, dst)


def _cast_step(src_refs, dst_refs):
    for src, dst in zip(src_refs, dst_refs):
        dst[...] = src[...].astype(BF16)


def _with_rounding(body, n_in, n_out, n_cast):
    def kernel(*refs, **kw):
        a, b, c = n_in + n_cast, n_in + n_cast + n_out, n_in + 2 * n_cast + n_out
        _cast_step(refs[n_in:a], refs[b:c])
        return body(*refs[:n_in], *refs[a:b], *refs[c:], **kw)
    return kernel


def _rounding_specs(todo, n_steps, step_of):
    plans = [_CastPlan(w, layer, n_steps) for w, layer in todo]
    specs = [p.specs(step_of) for p in plans]
    return ([s for s, _ in specs], [d for _, d in specs],
            [jax.ShapeDtypeStruct(p.shape, BF16) for p in plans])


def _qkv_kernel(x_ref, nw_ref, w_ref, cos_ref, sin_ref, q_ref, k_ref, v_ref, *, n_heads, q_scale):
    seg = n_heads * HEAD_DIM
    h = _rmsnorm(x_ref[...], nw_ref[...]).astype(BF16)
    cos = cos_ref[...]
    sin = sin_ref[...]

    def rope(a, hh):
        part = a[:, hh * HEAD_DIM:(hh + 1) * HEAD_DIM]
        return part * cos + pltpu.roll(part, HEAD_DIM // 2, axis=1) * sin

    aq = jnp.dot(h, w_ref[:, 0:seg], preferred_element_type=F32)
    for hh in range(n_heads):
        q_ref[:, hh * HEAD_DIM:(hh + 1) * HEAD_DIM] = (rope(aq, hh) * q_scale).astype(BF16)
    ak = jnp.dot(h, w_ref[:, seg:2 * seg], preferred_element_type=F32)
    for hh in range(n_heads):
        k_ref[:, hh * HEAD_DIM:(hh + 1) * HEAD_DIM] = rope(ak, hh)
    v_ref[...] = jnp.dot(h, w_ref[:, 2 * seg:3 * seg], preferred_element_type=F32)


def _gconv_kernel(x_ref, nw_ref, w_ref, cw_ref, hist_ref, yc_ref, cst_ref, ue_scr,
                  *, bm, tps, stride):
    m = pl.program_id(0)
    seg = yc_ref.shape[1]
    hs = CONV_HIST * stride
    base = _halo_base(hs)
    h = _rmsnorm(x_ref[...], nw_ref[...]).astype(BF16)
    u = (jnp.dot(h, w_ref[:, seg:2 * seg], preferred_element_type=F32)
         * jnp.dot(h, w_ref[:, 2 * seg:3 * seg], preferred_element_type=F32))
    first = (m % tps) == 0

    @pl.when(first)
    def _():
        ue_scr[base - hs:base, :] = hist_ref[...]

    @pl.when(jnp.logical_not(first))
    def _():
        ue_scr[base - hs:base, :] = ue_scr[base + bm - hs:base + bm, :]

    ue_scr[base:base + bm, :] = u
    cw = cw_ref[...]
    conv = u * cw[CONV_K - 1:CONV_K, :]
    for j in range(CONV_K - 1):
        off = base - (CONV_K - 1 - j) * stride
        conv = conv + ue_scr[off:off + bm, :] * cw[j:j + 1, :]
    gate_b = jnp.dot(h, w_ref[:, 0:seg], preferred_element_type=F32)
    yc_ref[...] = (gate_b * conv).astype(BF16)
    cst_ref[...] = ue_scr[base + bm - hs:base + bm, :]


def _inproj(x, nw, w_in, cos, sin, conv_w, hist, *, layer, bm, tps, stride, q_scale,
            round_qkv=(), round_gconv=()):
    M, D = x.shape
    seg = conv_w.shape[1]
    assert w_in.shape[2] == 6 * seg and M % bm == 0 and (M // bm) % tps == 0
    n_seq = hist.shape[0]
    hs = CONV_HIST * stride
    assert hist.shape == (n_seq, hs, seg) and bm >= hs
    base = _halo_base(hs)
    params = pltpu.CompilerParams(dimension_semantics=("arbitrary",), vmem_limit_bytes=VMEM_LIMIT)
    rows = lambda width: pl.BlockSpec((bm, width), lambda m: (m, 0))
    half = lambda n: pl.BlockSpec((None, D, 3 * seg), lambda m: (layer, 0, n),
                                  pipeline_mode=pl.Buffered(1))
    norm = pl.BlockSpec((1, D), lambda m: (0, 0))
    table = pl.BlockSpec((bm, HEAD_DIM), lambda m: (m % tps, 0))
    n_steps = M // bm
    src, dst, shp = _rounding_specs(round_qkv, n_steps, lambda m: m)
    q, k, v, *rounded_a = pl.pallas_call(
        _with_rounding(functools.partial(_qkv_kernel, n_heads=seg // HEAD_DIM, q_scale=q_scale),
                       5, 3, len(round_qkv)),
        grid=(n_steps,),
        in_specs=[rows(D), norm, half(0), table, table] + src,
        out_specs=[rows(seg), rows(seg), rows(seg)] + dst,
        out_shape=[
            jax.ShapeDtypeStruct((M, seg), BF16),
            jax.ShapeDtypeStruct((M, seg), F32),
            jax.ShapeDtypeStruct((M, seg), F32),
        ] + shp,
        compiler_params=params,
        name="inproj_qkv",
    )(x, nw, w_in, cos, sin, *[w for w, _ in round_qkv])
    state = pl.BlockSpec((None, hs, seg), lambda m: (m // tps, 0, 0))
    src, dst, shp = _rounding_specs(round_gconv, n_steps, lambda m: m)
    yc, cst, *rounded_b = pl.pallas_call(
        _with_rounding(functools.partial(_gconv_kernel, bm=bm, tps=tps, stride=stride),
                       5, 2, len(round_gconv)),
        grid=(n_steps,),
        in_specs=[rows(D), norm, half(1), pl.BlockSpec((CONV_K, seg), lambda m: (0, 0)), state]
                 + src,
        out_specs=[rows(seg), state] + dst,
        out_shape=[
            jax.ShapeDtypeStruct((M, seg), BF16),
            jax.ShapeDtypeStruct((n_seq, hs, seg), F32),
        ] + shp,
        scratch_shapes=[pltpu.VMEM((base + bm, seg), F32)],
        compiler_params=params,
        name="inproj_gconv",
    )(x, nw, w_in, conv_w, hist, *[w for w, _ in round_gconv])
    return q, k, v, yc, cst, rounded_a + rounded_b


def _group_reduce(x, op):
    return op(x.reshape(x.shape[0] // SUBLANES, SUBLANES, x.shape[1]), axis=0)


def _moba_kernel(q_ref, k_ref, v_ref, o_ref, kb_scr, vt_scr, ks_scr, qt_scr, sel_scr, s_scr,
               *, nq):
    i = pl.program_id(2)
    blk = MOBA_BLOCK

    @pl.when(i == 0)
    def _():
        for j in range(nq):
            kf = k_ref[j * blk:(j + 1) * blk, :]
            kb_scr[j] = kf.astype(BF16)
            vt_scr[j] = v_ref[j * blk:(j + 1) * blk, :].T.astype(BF16)
            ks_scr[j:j + 1, :] = jnp.sum(kf, axis=0, keepdims=True)
        ks = ks_scr[...]
        ks_hi = ks.astype(BF16)
        ks_lo = (ks - ks_hi.astype(F32)).astype(BF16)
        blk_id = lax.broadcasted_iota(jnp.int32, (nq, blk), 0)
        for own in range(nq):
            qt = q_ref[own * blk:(own + 1) * blk, :].astype(F32).T.astype(BF16)
            qt_scr[own] = qt
            gate = (jnp.dot(ks_hi, qt, preferred_element_type=F32)
                    + jnp.dot(ks_lo, qt, preferred_element_type=F32))
            g = jnp.where(blk_id < own, gate, -jnp.inf)
            sel = jnp.zeros((nq, blk), jnp.bool_)
            for _ in range(min(MOBA_TOPK, own)):
                top = jnp.max(g, axis=0, keepdims=True)
                first = jnp.min(jnp.where(g == top, blk_id, nq), axis=0, keepdims=True)
                hit = blk_id == first
                sel = sel | hit
                g = jnp.where(hit, -jnp.inf, g)
            sel_scr[own] = jnp.where(sel, 1.0, 0.0)

    for pair in range(nq // 2):
        @pl.when(i == pair)
        def _(pair=pair):
            _moba_pair(pair, o_ref, kb_scr, vt_scr, qt_scr, sel_scr, s_scr, nq=nq)


def _moba_pair(pair, o_ref, kb_scr, vt_scr, qt_scr, sel_scr, s_scr, *, nq):
    blk = MOBA_BLOCK
    owns = (pair, nq - 1 - pair)

    tiles = [(w, j) for w, own in enumerate(owns) for j in range(own)]
    n_gen = len(tiles)

    mx = [None, None]
    for t, (w, j) in enumerate(tiles):
        s = jnp.dot(kb_scr[j], qt_scr[owns[w]], preferred_element_type=F32)
        s = jnp.where(sel_scr[owns[w], j:j + 1, :] > 0.0, s, NEG_INF)
        s_scr[t] = s
        cm = _group_reduce(s, jnp.max)
        mx[w] = cm if mx[w] is None else jnp.maximum(mx[w], cm)
    key_id = lax.broadcasted_iota(jnp.int32, (blk, blk), 0)
    qry_id = lax.broadcasted_iota(jnp.int32, (blk, blk), 1)
    for w, own in enumerate(owns):
        s = jnp.dot(kb_scr[own], qt_scr[own], preferred_element_type=F32)
        s = jnp.where(key_id <= qry_id, s, NEG_INF)
        s_scr[n_gen + w] = s
        cm = _group_reduce(s, jnp.max)
        mx[w] = cm if mx[w] is None else jnp.maximum(mx[w], cm)
    mrow = [jnp.max(m, axis=0, keepdims=True) for m in mx]

    acc = [None, None]
    den = [None, None]
    order = list(enumerate(tiles)) + [(n_gen + w, (w, own)) for w, own in enumerate(owns)]
    for t, (w, j) in order:
        p = jnp.exp2(s_scr[t] - mrow[w])
        cs = _group_reduce(p, jnp.sum)
        den[w] = cs if den[w] is None else den[w] + cs
        r = jnp.dot(vt_scr[j], p.astype(BF16), preferred_element_type=F32)
        acc[w] = r if acc[w] is None else acc[w] + r
    for w, own in enumerate(owns):
        tot = jnp.sum(den[w], axis=0, keepdims=True)
        o_ref[own * blk:(own + 1) * blk, :] = (acc[w] / tot).T.astype(BF16)


def _moba_prompt(q, k, v, *, n_batch, seq, n_heads, round_weights=()):
    blk = MOBA_BLOCK
    assert seq % (2 * blk) == 0
    nq = seq // blk
    width = n_heads * HEAD_DIM
    grid = (n_batch, n_heads, nq // 2)
    src, dst, shp = _rounding_specs(round_weights, grid[0] * grid[1] * grid[2],
                                    lambda b, h, i: (b * grid[1] + h) * grid[2] + i)
    col = pl.BlockSpec((None, seq, HEAD_DIM), lambda b, h, i: (b, 0, h))
    out, *rounded = pl.pallas_call(
        _with_rounding(functools.partial(_moba_kernel, nq=nq), 3, 1, len(round_weights)),
        grid=grid,
        in_specs=[col, col, col] + src,
        out_specs=[col] + dst,
        out_shape=[jax.ShapeDtypeStruct((n_batch, seq, width), BF16)] + shp,
        scratch_shapes=[
            pltpu.VMEM((nq, blk, HEAD_DIM), BF16),
            pltpu.VMEM((nq, HEAD_DIM, blk), BF16),
            pltpu.VMEM((nq, HEAD_DIM), F32),
            pltpu.VMEM((nq, HEAD_DIM, blk), BF16),
            pltpu.VMEM((nq, nq, blk), F32),
            pltpu.VMEM((nq + 1, blk, blk), F32),
        ],
        compiler_params=pltpu.CompilerParams(
            dimension_semantics=("arbitrary", "arbitrary", "arbitrary"),
            vmem_limit_bytes=VMEM_LIMIT),
        name="moba_prompt",
    )(q.reshape(n_batch, seq, width), k.reshape(n_batch, seq, width),
      v.reshape(n_batch, seq, width), *[w for w, _ in round_weights])
    return out.reshape(n_batch * seq, width), rounded


def _outproj_kernel(a_ref, c_ref, wt_ref, wb_ref, x_ref, o_ref):
    o_ref[...] = (x_ref[...]
                  + jnp.dot(a_ref[...], wt_ref[...], preferred_element_type=F32)
                  + jnp.dot(c_ref[...], wb_ref[...], preferred_element_type=F32))


def _outproj(a, c, w_o, x, *, bm, bn, round_weights=()):
    M, D = x.shape
    half = a.shape[1]
    assert w_o.shape == (2 * half, D) and c.shape == a.shape and D % bn == 0 and M % bm == 0
    w_mode = pl.Buffered(1) if bn == D else None
    grid = (M // bm, D // bn)
    src, dst, shp = _rounding_specs(round_weights, grid[0] * grid[1],
                                    lambda m, n: m * grid[1] + n)
    out, *rounded = pl.pallas_call(
        _with_rounding(_outproj_kernel, 5, 1, len(round_weights)),
        grid=grid,
        in_specs=[
            pl.BlockSpec((bm, half), lambda m, n: (m, 0)),
            pl.BlockSpec((bm, half), lambda m, n: (m, 0)),
            pl.BlockSpec((half, bn), lambda m, n: (0, n), pipeline_mode=w_mode),
            pl.BlockSpec((half, bn), lambda m, n: (1, n), pipeline_mode=w_mode),
            pl.BlockSpec((bm, bn), lambda m, n: (m, n)),
        ] + src,
        out_specs=[pl.BlockSpec((bm, bn), lambda m, n: (m, n))] + dst,
        out_shape=[jax.ShapeDtypeStruct((M, D), F32)] + shp,
        compiler_params=pltpu.CompilerParams(
            dimension_semantics=("arbitrary", "arbitrary"), vmem_limit_bytes=VMEM_LIMIT),
        name="outproj",
    )(a, c, w_o, w_o, x, *[w for w, _ in round_weights])
    return out, rounded


def _ffn_kernel(x_ref, nw_ref, wg_ref, wu_ref, wd_ref, fw_ref, o_ref, h_scr, *, final_norm):
    f = pl.program_id(1)

    @pl.when(f == 0)
    def _():
        x = x_ref[...]
        h_scr[...] = _rmsnorm(x, nw_ref[...]).astype(BF16)
        o_ref[...] = x

    h = h_scr[...]
    g = jnp.dot(h, wg_ref[...], preferred_element_type=F32)
    u = jnp.dot(h, wu_ref[...], preferred_element_type=F32)
    a = (g * jax.nn.sigmoid(g) * u).astype(BF16)
    o_ref[...] += jnp.dot(a, wd_ref[...], preferred_element_type=F32)

    if final_norm:
        @pl.when(f == pl.num_programs(1) - 1)
        def _():
            o_ref[...] = _rmsnorm(o_ref[...], fw_ref[...])


def _ffn(x, nw, wg, wu, wd, fw, *, bm, bf, final_norm):
    M, D = x.shape
    FF = wg.shape[1]
    assert FF % bf == 0 and M % bm == 0
    return pl.pallas_call(
        functools.partial(_ffn_kernel, final_norm=final_norm),
        grid=(M // bm, FF // bf),
        in_specs=[
            pl.BlockSpec((bm, D), lambda m, f: (m, 0)),
            pl.BlockSpec((1, D), lambda m, f: (0, 0)),
            pl.BlockSpec((D, bf), lambda m, f: (0, f)),
            pl.BlockSpec((D, bf), lambda m, f: (0, f)),
            pl.BlockSpec((bf, D), lambda m, f: (f, 0)),
            pl.BlockSpec((1, D), lambda m, f: (0, 0)),
        ],
        out_specs=pl.BlockSpec((bm, D), lambda m, f: (m, 0)),
        out_shape=jax.ShapeDtypeStruct((M, D), F32),
        scratch_shapes=[pltpu.VMEM((bm, D), BF16)],
        compiler_params=pltpu.CompilerParams(
            dimension_semantics=("arbitrary", "arbitrary"), vmem_limit_bytes=VMEM_LIMIT),
        name="ffn",
    )(x, nw, wg, wu, wd, fw)


def _pool_kernel(x_ref, nw_ref, wp_ref, ps_ref, hist_ref, o_ref, pst_ref, ext_scr, band_scr,
                 *, bm, tps, stride, pos_base):
    m = pl.program_id(0)
    hs = POOL_HIST * stride
    base = _halo_base(hs)
    rb = band_scr.shape[1]
    nd = min(rb, base)
    assert bm % rb == 0
    assert nd == rb or (stride == 1 and nd % SUBLANES == 0 and nd >= max(POOL_WINDOWS))
    grp = wp_ref.shape[1]
    first = (m % tps) == 0

    if nd < rb:
        @pl.when(m == 0)
        def _():
            lag = (lax.broadcasted_iota(jnp.int32, (rb, rb), 0)
                   - lax.broadcasted_iota(jnp.int32, (rb, rb), 1))
            for gi, w in enumerate(POOL_WINDOWS):
                band_scr[gi] = jnp.where((lag >= 0) & (lag < w), 1.0, 0.0).astype(BF16)

    @pl.when(first)
    def _():
        ext_scr[base - hs:base, :] = hist_ref[...]

    @pl.when(jnp.logical_not(first))
    def _():
        ext_scr[base - hs:base, :] = ext_scr[base + bm - hs:base + bm, :]

    ext_scr[base:base + bm, :] = _rmsnorm(x_ref[...], nw_ref[...])
    for r0 in range(0, bm, rb):
        r = (m % tps) * bm + r0 + lax.broadcasted_iota(jnp.int32, (nd, 1), 0)
        pos = pos_base + (r if stride == 1 else lax.shift_right_logical(r, stride.bit_length() - 1))
        for gi, w in enumerate(POOL_WINDOWS):
            cols = slice(gi * grp, (gi + 1) * grp)
            hb = ext_scr[base + r0:base + r0 + rb, cols]
            tot = hb[:nd]
            for kk in range(1, w):
                off = base + r0 - kk * stride
                tot = tot + ext_scr[off:off + nd, cols]
            d = tot / jnp.minimum(pos + 1, w).astype(F32) - hb[:nd]
            if nd < rb:
                hi = hb.astype(BF16)
                lo = (hb - hi.astype(F32)).astype(BF16)
                band = band_scr[gi]
                full = (jnp.dot(band, hi, preferred_element_type=F32)
                        + jnp.dot(band, lo, preferred_element_type=F32))
                d = jnp.concatenate([d, full[nd:] * (1.0 / w) - hb[nd:]], axis=0)
            out = jnp.dot(d.astype(BF16), wp_ref[gi], preferred_element_type=F32)
            o_ref[r0:r0 + rb, cols] = x_ref[r0:r0 + rb, cols] + out * ps_ref[:, cols]
    pst_ref[...] = ext_scr[base + bm - hs:base + bm, :]


def _pool(x, nw, w_pool, pool_scale, hist, *, layer, bm, tps, stride, pos_base):
    M, D = x.shape
    n_seq = hist.shape[0]
    hs = POOL_HIST * stride
    assert stride & (stride - 1) == 0 and hist.shape == (n_seq, hs, D) and M % bm == 0
    base = _halo_base(hs)
    _, ng, grp, _ = w_pool.shape
    assert ng == len(POOL_WINDOWS) and ng * grp == D
    kern = functools.partial(_pool_kernel, bm=bm, tps=tps, stride=stride, pos_base=pos_base)
    return pl.pallas_call(
        kern,
        grid=(M // bm,),
        in_specs=[
            pl.BlockSpec((bm, D), lambda m: (m, 0)),
            pl.BlockSpec((1, D), lambda m: (0, 0)),
            pl.BlockSpec((None, ng, grp, grp), lambda m: (layer, 0, 0, 0)),
            pl.BlockSpec((1, D), lambda m: (0, 0)),
            pl.BlockSpec((None, hs, D), lambda m: (m // tps, 0, 0)),
        ],
        out_specs=[
            pl.BlockSpec((bm, D), lambda m: (m, 0)),
            pl.BlockSpec((None, hs, D), lambda m: (m // tps, 0, 0)),
        ],
        out_shape=[
            jax.ShapeDtypeStruct((M, D), F32),
            jax.ShapeDtypeStruct((n_seq, hs, D), F32),
        ],
        scratch_shapes=[pltpu.VMEM((base + bm, D), F32),
                        pltpu.VMEM((ng, min(bm, MXU_DIM), min(bm, MXU_DIM)), BF16)],
        compiler_params=pltpu.CompilerParams(
            dimension_semantics=("arbitrary",), vmem_limit_bytes=VMEM_LIMIT),
        name="pool",
    )(x, nw, w_pool, pool_scale, hist)


def _head_match(n_rows, n_cols, n_heads, t_pad):
    rh = lax.broadcasted_iota(jnp.int32, (n_rows, n_cols), 0) // t_pad
    ch = lax.broadcasted_iota(jnp.int32, (n_rows, n_cols), 1) % n_heads
    return rh == ch


def _fold_heads(x, n_heads, t_pad):
    out = x[0:t_pad]
    for hh in range(1, n_heads):
        out = out + x[hh * t_pad:(hh + 1) * t_pad]
    return out


def _lane_class(x, n_heads, op):
    sh = n_heads
    while sh < x.shape[-1]:
        x = op(x, pltpu.roll(x, sh, axis=x.ndim - 1))
        sh *= 2
    return x


def _lane_tiles(x, op):
    out = x[:, 0:LANES]
    for t in range(1, x.shape[1] // LANES):
        out = op(out, x[:, t * LANES:(t + 1) * LANES])
    return out


def _sa_logits_kernel(pt_ref, q_ref, *refs, n_pages, n_heads, t_pad, pages_per_block):
    k_refs = refs[:n_pages]
    r_ref, g_ref = refs[n_pages:]
    qa = q_ref[...]
    rows = k_refs[0].shape[0] * n_heads
    match = _head_match(qa.shape[0], rows, n_heads, t_pad)
    for pg in range(n_pages):
        k2 = k_refs[pg][...].reshape(rows, HEAD_DIM).astype(BF16)
        l2 = lax.dot_general(qa, k2, _NT, preferred_element_type=F32)
        r_ref[pg] = _fold_heads(jnp.where(match, l2, 0.0), n_heads, t_pad)
    for bb in range(n_pages // pages_per_block):
        tot = _lane_tiles(r_ref[bb * pages_per_block], jnp.add)
        for pg in range(1, pages_per_block):
            tot = tot + _lane_tiles(r_ref[bb * pages_per_block + pg], jnp.add)
        g_ref[bb] = _lane_class(tot, n_heads, jnp.add)


def _sa_values_kernel(pt_ref, r_ref, g_ref, q_ref, kn_ref, vn_ref, *refs,
                      n_pages, n_heads, t_pad, pages_per_block):
    v_refs = refs[:n_pages]
    o_ref, p_scr, sel_scr, l_scr, acc_scr = refs[n_pages:]
    c = pl.program_id(1)
    n_blocks = g_ref.shape[0]
    rows = v_refs[0].shape[0] * n_heads
    n_rows = q_ref.shape[0]
    match = _head_match(n_rows, rows, n_heads, t_pad)

    def spread(p):
        return jnp.where(match[:, :p.shape[1]], jnp.concatenate([p] * n_heads, axis=0), 0.0).astype(BF16)

    @pl.when(c == 0)
    def _():
        gates = g_ref[...]
        idx = lax.broadcasted_iota(jnp.int32, gates.shape, 0)
        sel = jnp.zeros(gates.shape, F32)
        for _ in range(MOBA_TOPK):
            mx = jnp.max(gates, axis=0, keepdims=True)
            first = jnp.min(jnp.where(gates == mx, idx, n_blocks), axis=0, keepdims=True)
            hit = idx == first
            sel = jnp.where(hit, 1.0, sel)
            gates = jnp.where(hit, -jnp.inf, gates)
        sel_scr[...] = sel

        own_w = kn_ref.shape[0]
        lo = lax.dot_general(q_ref[...], kn_ref[...].astype(BF16), _NT, preferred_element_type=F32)
        ro = _fold_heads(jnp.where(match[:, :own_w], lo, 0.0), n_heads, t_pad)
        t_key = lax.broadcasted_iota(jnp.int32, ro.shape, 1) // n_heads
        t_qry = lax.broadcasted_iota(jnp.int32, ro.shape, 0)
        ro = jnp.where(t_key <= t_qry, ro, NEG_INF)

        def max_body(b, mx):
            here = _lane_tiles(r_ref[b * pages_per_block], jnp.maximum)
            for pg in range(1, pages_per_block):
                here = jnp.maximum(here, _lane_tiles(r_ref[b * pages_per_block + pg], jnp.maximum))
            return jnp.maximum(mx, jnp.where(sel_scr[b] > 0.0, here, NEG_INF))

        mx = lax.fori_loop(0, n_blocks, max_body, ro)
        mx = _lane_class(mx, n_heads, jnp.maximum)
        width = r_ref.shape[2]
        mx_w = jnp.concatenate([mx] * (width // LANES), axis=1)

        def exp_body(b, tot):
            keep = jnp.concatenate([sel_scr[b]] * (width // LANES), axis=1) > 0.0
            for pg in range(pages_per_block):
                p = jnp.where(keep, jnp.exp(r_ref[b * pages_per_block + pg] - mx_w), 0.0)
                p_scr[b * pages_per_block + pg] = p
                tot = tot + _lane_tiles(p, jnp.add)
            return tot

        po = jnp.exp(ro - mx)
        tot = lax.fori_loop(0, n_blocks, exp_body, po)
        l_scr[...] = _lane_class(tot, n_heads, jnp.add)
        acc_scr[...] = jnp.dot(spread(po), vn_ref[...].astype(BF16), preferred_element_type=F32)

    acc = acc_scr[...]
    for pg in range(n_pages):
        v2 = v_refs[pg][...].reshape(rows, HEAD_DIM).astype(BF16)
        acc = acc + jnp.dot(spread(p_scr[c * n_pages + pg]), v2, preferred_element_type=F32)
    acc_scr[...] = acc

    @pl.when(c == pl.num_programs(1) - 1)
    def _():
        l = l_scr[...]
        den = jnp.concatenate([l[:, hh:hh + 1] for hh in range(n_heads)], axis=0)
        o_ref[...] = acc_scr[...] / den


def _sample_attention(q_all, k_new, v_new, cache_k, cache_v, page_table, *, layer, n_heads, t_pad):
    n_dec, n_pt = page_table.shape
    page = cache_k.shape[2]
    ppb = MOBA_BLOCK // page
    P = PAGES_PER_STEP
    assert MOBA_BLOCK % page == 0 and n_pt % P == 0 and P % ppb == 0
    n_steps = n_pt // P
    n_blocks = n_pt // ppb
    width = page * n_heads
    n_rows = n_heads * t_pad

    def page_spec(i):
        return pl.BlockSpec((None, None, page, n_heads, HEAD_DIM),
                            lambda b, c, pt: (layer, pt[b, c * P + i], 0, 0, 0))

    params = pltpu.CompilerParams(dimension_semantics=("arbitrary", "arbitrary"),
                                  vmem_limit_bytes=VMEM_LIMIT)
    common = dict(n_pages=P, n_heads=n_heads, t_pad=t_pad, pages_per_block=ppb)
    logits, gates = pl.pallas_call(
        functools.partial(_sa_logits_kernel, **common),
        grid_spec=pltpu.PrefetchScalarGridSpec(
            num_scalar_prefetch=1,
            grid=(n_dec, n_steps),
            in_specs=[pl.BlockSpec((None, n_rows, HEAD_DIM), lambda b, c, pt: (b, 0, 0))]
                     + [page_spec(i) for i in range(P)],
            out_specs=[
                pl.BlockSpec((None, P, t_pad, width), lambda b, c, pt: (b, c, 0, 0)),
                pl.BlockSpec((None, P // ppb, t_pad, LANES), lambda b, c, pt: (b, c, 0, 0)),
            ],
        ),
        out_shape=[
            jax.ShapeDtypeStruct((n_dec, n_pt, t_pad, width), F32),
            jax.ShapeDtypeStruct((n_dec, n_blocks, t_pad, LANES), F32),
        ],
        compiler_params=params,
        name="sample_logits",
    )(page_table, q_all, *([cache_k] * P))

    own = k_new.shape[1]
    return pl.pallas_call(
        functools.partial(_sa_values_kernel, **common),
        grid_spec=pltpu.PrefetchScalarGridSpec(
            num_scalar_prefetch=1,
            grid=(n_dec, n_steps),
            in_specs=[
                pl.BlockSpec((None, n_pt, t_pad, width), lambda b, c, pt: (b, 0, 0, 0)),
                pl.BlockSpec((None, n_blocks, t_pad, LANES), lambda b, c, pt: (b, 0, 0, 0)),
                pl.BlockSpec((None, n_rows, HEAD_DIM), lambda b, c, pt: (b, 0, 0)),
                pl.BlockSpec((None, own, HEAD_DIM), lambda b, c, pt: (b, 0, 0)),
                pl.BlockSpec((None, own, HEAD_DIM), lambda b, c, pt: (b, 0, 0)),
            ] + [page_spec(i) for i in range(P)],
            out_specs=pl.BlockSpec((None, n_rows, HEAD_DIM), lambda b, c, pt: (b, 0, 0)),
            scratch_shapes=[
                pltpu.VMEM((n_pt, t_pad, width), F32),
                pltpu.VMEM((n_blocks, t_pad, LANES), F32),
                pltpu.VMEM((t_pad, LANES), F32),
                pltpu.VMEM((n_rows, HEAD_DIM), F32),
            ],
        ),
        out_shape=jax.ShapeDtypeStruct((n_dec, n_rows, HEAD_DIM), F32),
        compiler_params=params,
        name="sample_values",
    )(page_table, logits, gates, q_all, k_new, v_new, *([cache_v] * P))


def _rope_tables(pos):
    half = HEAD_DIM // 2
    inv = ROPE_THETA ** (-jnp.arange(half, dtype=F32) / half)
    ang = pos.astype(F32)[:, None] * inv[None, :]
    cos, sin = jnp.cos(ang), jnp.sin(ang)
    return jnp.concatenate([cos, cos], axis=1), jnp.concatenate([-sin, sin], axis=1)


def kernel(x_prompt, x_sample, cache_k, cache_v, page_table, state_conv, state_pool,
           norm_mix, norm_ffn, norm_final, w_in, conv_w, w_o, w_pool, pool_scale,
           w_gate, w_up, w_down):
    n_batch, n_seq, d = x_prompt.shape
    n_dec, n_new, _ = x_sample.shape
    depth = norm_mix.shape[0]
    seg = conv_w.shape[2]
    n_heads = seg // HEAD_DIM
    page = cache_k.shape[2]
    past_len = page_table.shape[1] * page
    assert n_dec == SUBLANES and n_new <= SUBLANES and past_len % MOBA_BLOCK == 0
    assert cache_k.shape[3] == n_heads and cache_k.shape[4] == HEAD_DIM
    q_scale = HEAD_DIM ** -0.5
    q_scale_log2 = q_scale * LOG2_E
    t_pad = SUBLANES
    own_keys = LANES // n_heads
    assert n_new <= own_keys

    w_in_b, w_pool_b = w_in.astype(BF16), w_pool.astype(BF16)
    ffn_bf16 = {}
    row = lambda a: a.reshape(1, -1)

    xp = x_prompt.reshape(n_batch * n_seq, d)
    xs = x_sample.transpose(1, 0, 2).reshape(n_new * n_dec, d)
    ms = n_new * n_dec
    bm = ROW_TILE
    assert n_seq % bm == 0 and (n_batch * n_seq) % FFN_ROW_TILE == 0
    tps = n_seq // bm
    cos_p, sin_p = _rope_tables(jnp.arange(n_seq, dtype=jnp.int32))
    pos_s = past_len + jnp.repeat(jnp.arange(n_new, dtype=jnp.int32), n_dec)
    cos_s, sin_s = _rope_tables(pos_s)

    def tm(a):
        return a.transpose(1, 0, 2).reshape(1, a.shape[1] * n_dec, a.shape[2])

    def untm(a, rows):
        return a.reshape(rows, n_dec, a.shape[-1]).transpose(1, 0, 2)

    k_p, v_p, k_s, v_s, conv_p, conv_s, pool_p, pool_s = [], [], [], [], [], [], [], []
    for layer in range(depth):
        if layer % 2 == 0:
            e = layer // 2
            nxt = layer + 1
            ahead = nxt < depth
            q, k, v, yc, cst, r_in = _inproj(
                xp, row(norm_mix[layer]), w_in_b, cos_p, sin_p, conv_w[e],
                jnp.zeros((n_batch, CONV_HIST, seg), F32),
                layer=e, bm=bm, tps=tps, stride=1, q_scale=q_scale_log2,
                round_qkv=((w_gate, nxt),) if ahead else (),
                round_gconv=((w_o, e),) + (((w_up, nxt), (w_down, nxt)) if ahead else ()))
            w_o_e = r_in.pop(1 if ahead else 0)
            attn, ffn_bf16[layer] = _moba_prompt(
                q, k, v, n_batch=n_batch, seq=n_seq, n_heads=n_heads,
                round_weights=tuple((w, layer) for w in (w_gate, w_up, w_down)))
            xp, _ = _outproj(attn, yc, w_o_e, xp, bm=bm, bn=d)
            if ahead:
                ffn_bf16[nxt] = r_in
            xs, attn = lax.optimization_barrier((xs, attn))
            k_p.append(k.reshape(n_batch, n_seq, n_heads, HEAD_DIM))
            v_p.append(v.reshape(n_batch, n_seq, n_heads, HEAD_DIM))
            conv_p.append(cst)
            q, k, v, yc, cst, _ = _inproj(
                xs, row(norm_mix[layer]), w_in_b, cos_s, sin_s, conv_w[e], tm(state_conv[e]),
                layer=e, bm=ms, tps=1, stride=n_dec, q_scale=q_scale)
            heads = lambda a: a.reshape(n_new, n_dec, n_heads, HEAD_DIM)
            q_all = jnp.pad(heads(q).transpose(1, 2, 0, 3), ((0, 0), (0, 0), (0, t_pad - n_new), (0, 0)))
            q_all = q_all.reshape(n_dec, n_heads * t_pad, HEAD_DIM)
            k_new = heads(k).transpose(1, 0, 2, 3)
            v_new = heads(v).transpose(1, 0, 2, 3)
            own = lambda a: jnp.pad(a, ((0, 0), (0, own_keys - n_new), (0, 0), (0, 0))).reshape(
                n_dec, own_keys * n_heads, HEAD_DIM)
            o = _sample_attention(q_all, own(k_new), own(v_new), cache_k, cache_v, page_table,
                                  layer=e, n_heads=n_heads, t_pad=t_pad)
            o = o.reshape(n_dec, n_heads, t_pad, HEAD_DIM)[:, :, :n_new]
            attn = o.transpose(2, 0, 1, 3).reshape(ms, seg).astype(BF16)
            xs, _ = _outproj(attn, yc, w_o_e, xs, bm=ms, bn=d)
            k_s.append(k_new)
            v_s.append(v_new)
            conv_s.append(untm(cst[0], CONV_HIST))
        else:
            o_ = layer // 2
            xp, pst = _pool(xp, row(norm_mix[layer]), w_pool_b, row(pool_scale[o_]),
                            jnp.zeros((n_batch, POOL_HIST, d), F32),
                            layer=o_, bm=bm, tps=tps, stride=1, pos_base=0)
            pool_p.append(pst)
            xs, pst = _pool(xs, row(norm_mix[layer]), w_pool_b, row(pool_scale[o_]),
                            tm(state_pool[o_]), layer=o_, bm=ms, tps=1, stride=n_dec,
                            pos_base=past_len)
            pool_s.append(untm(pst[0], POOL_HIST))
        last = layer == depth - 1
        wg, wu, wd = ffn_bf16[layer]
        ffn = functools.partial(_ffn, nw=row(norm_ffn[layer]), wg=wg, wu=wu, wd=wd,
                                fw=row(norm_final), final_norm=last)
        xp = ffn(xp, bm=FFN_ROW_TILE, bf=FFN_FF_TILE)
        xs = ffn(xs, bm=ms, bf=SAMPLE_FF_TILE)

    y_prompt = xp.reshape(n_batch, n_seq, d)
    y_sample = xs.reshape(n_new, n_dec, d).transpose(1, 0, 2)
    return (y_prompt, y_sample, jnp.stack(k_p), jnp.stack(v_p), jnp.stack(k_s), jnp.stack(v_s),
            jnp.stack(conv_p), jnp.stack(conv_s), jnp.stack(pool_p), jnp.stack(pool_s))
```
